```python
import jax
import jax.numpy as jnp
from jax import lax
import numpy as np

D_MODEL = 4096
BATCH = 4
SEQ = 2048
DEPTH = 1

HEAD_DIM = 64
N_Q_HEADS = 32
N_KV_HEADS = 4
Q_GROUP = N_Q_HEADS // N_KV_HEADS
WINDOW = 128
ATTN_BLOCK = 128
ROPE_THETA = 10000.0
CONV_CH = 2048
CONV_WIDTH = 31
N_EXPERTS = 32
TOP_K = 4
D_FF = 2048
SWIGLU_ALPHA = 1.702
SWIGLU_LIMIT = 7.0
MOE_BLOCK = 128
NORM_EPS = 1e-5

Q_WIDTH = N_Q_HEADS * HEAD_DIM
KV_WIDTH = N_KV_HEADS * HEAD_DIM
IN_WIDTH = Q_WIDTH + 2 * KV_WIDTH + 2 * CONV_CH + 2 * D_MODEL

kernel_name = 'hybrid_swa_sink_conformer_moe'


def _rmsnorm(x, g):
    xf = x.astype(jnp.float32)
    y = xf * lax.rsqrt(jnp.mean(xf * xf, axis=-1, keepdims=True) + NORM_EPS)
    return (y * g.astype(jnp.float32)).astype(x.dtype)


def _layernorm(x, g, b):
    xf = x.astype(jnp.float32)
    mu = jnp.mean(xf, axis=-1, keepdims=True)
    xc = xf - mu
    y = xc * lax.rsqrt(jnp.mean(xc * xc, axis=-1, keepdims=True) + NORM_EPS)
    return (y * g.astype(jnp.float32) + b.astype(jnp.float32)).astype(x.dtype)


def _rope(t):
    seq = t.shape[1]
    inv_freq = 1.0 / (ROPE_THETA ** (jnp.arange(0, HEAD_DIM, 2, dtype=jnp.float32) / HEAD_DIM))
    ang = jnp.arange(seq, dtype=jnp.float32)[:, None] * inv_freq[None, :]
    cos = jnp.cos(ang)[None, :, None, :]
    sin = jnp.sin(ang)[None, :, None, :]
    t1, t2 = jnp.split(t.astype(jnp.float32), 2, axis=-1)
    return jnp.concatenate([t1 * cos - t2 * sin, t2 * cos + t1 * sin], axis=-1).astype(t.dtype)


def _sliding_window_attention(q, k, v, sinks):
    b, s = q.shape[0], q.shape[1]
    nb = s // ATTN_BLOCK
    qb = q.reshape(b, nb, ATTN_BLOCK, N_KV_HEADS, Q_GROUP, HEAD_DIM)

    def band(t):
        tb = t.reshape(b, nb, ATTN_BLOCK, N_KV_HEADS, HEAD_DIM)
        prev = jnp.pad(tb[:, :-1], ((0, 0), (1, 0), (0, 0), (0, 0), (0, 0)))
        return jnp.concatenate([prev, tb], axis=2)

    kw, vw = band(k), band(v)
    scores = jnp.einsum('bnqhgd,bnkhd->bnhgqk', qb, kw,
                        preferred_element_type=jnp.float32) * (HEAD_DIM ** -0.5)
    qi = jnp.arange(ATTN_BLOCK)[:, None]
    kj = jnp.arange(2 * ATTN_BLOCK)[None, :]
    rel = qi + ATTN_BLOCK - kj
    key_pos = (jnp.arange(nb) * ATTN_BLOCK - ATTN_BLOCK)[:, None, None] + kj[None]
    mask = ((rel >= 0) & (rel < WINDOW))[None] & (key_pos >= 0)
    scores = jnp.where(mask[None, :, None, None], scores, -jnp.inf)
    sink = jnp.broadcast_to(
        sinks.astype(jnp.float32).reshape(1, 1, N_KV_HEADS, Q_GROUP, 1, 1),
        scores.shape[:-1] + (1,))
    probs = jax.nn.softmax(jnp.concatenate([scores, sink], axis=-1), axis=-1)[..., :-1]
    out = jnp.einsum('bnhgqk,bnkhd->bnqhgd', probs.astype(v.dtype), vw)
    return out.reshape(b, s, Q_WIDTH)


def _conformer_conv(u, dw_w, dw_b, ln_g, ln_b):
    a, g = jnp.split(u, 2, axis=-1)
    z = a * jax.nn.sigmoid(g)
    z = lax.conv_general_dilated(
        z, dw_w[:, None, :].astype(z.dtype), window_strides=(1,),
        padding=((CONV_WIDTH - 1, 0),),
        dimension_numbers=('NWC', 'WIO', 'NWC'),
        feature_group_count=CONV_CH) + dw_b
    z = _layernorm(z, ln_g, ln_b)
    return z * jax.nn.sigmoid(z)


def _moe(h, w_router, b_router, w_gate_up, b_gate_up, w_down, b_down):
    t = h.shape[0]
    logits = (h @ w_router + b_router).astype(jnp.float32)
    top_vals, top_idx = lax.top_k(logits, TOP_K)
    gates = jax.nn.softmax(top_vals, axis=-1)
    n_assign = t * TOP_K
    flat_e = top_idx.reshape(-1).astype(jnp.int32)
    flat_tok = jnp.repeat(jnp.arange(t, dtype=jnp.int32), TOP_K)
    flat_g = gates.reshape(-1)
    order = jnp.argsort(flat_e)
    sorted_e = flat_e[order]
    counts = jnp.bincount(flat_e, length=N_EXPERTS).astype(jnp.int32)
    starts = jnp.cumsum(counts) - counts
    padded = (counts + MOE_BLOCK - 1) // MOE_BLOCK * MOE_BLOCK
    pad_ends = jnp.cumsum(padded)
    pad_starts = pad_ends - padded
    dest = pad_starts[sorted_e] + jnp.arange(n_assign, dtype=jnp.int32) - starts[sorted_e]
    n_rows = n_assign + N_EXPERTS * MOE_BLOCK
    n_blocks = n_rows // MOE_BLOCK
    row_tok = jnp.zeros((n_rows,), jnp.int32).at[dest].set(flat_tok[order])
    row_gate = jnp.zeros((n_rows,), jnp.float32).at[dest].set(flat_g[order])
    blk_e = jnp.minimum(
        jnp.searchsorted(pad_ends, jnp.arange(n_blocks, dtype=jnp.int32) * MOE_BLOCK, side='right'),
        N_EXPERTS - 1).astype(jnp.int32)
    xs = h[row_tok].reshape(n_blocks, MOE_BLOCK, h.shape[-1])

    def expert_block(args):
        xb, e = args
        gu = xb @ w_gate_up[e] + b_gate_up[e]
        glu, lin = jnp.split(gu, 2, axis=-1)
        glu = jnp.minimum(glu, SWIGLU_LIMIT)
        lin = jnp.clip(lin, -SWIGLU_LIMIT, SWIGLU_LIMIT)
        act = glu * jax.nn.sigmoid(SWIGLU_ALPHA * glu) * (lin + 1.0)
        return act @ w_down[e] + b_down[e]

    ys = lax.map(expert_block, (xs, blk_e)).reshape(n_rows, h.shape[-1])
    return jnp.zeros_like(h).at[row_tok].add(ys * row_gate[:, None].astype(ys.dtype))


def setup_inputs(seed: int = 0) -> dict:
    key = jax.random.key(seed)
    ks = jax.random.split(key, 21)

    def nrm(k, shape, scale):
        return jax.random.normal(k, shape, jnp.float32) * scale

    L = DEPTH
    return {
        'x': nrm(ks[0], (BATCH, SEQ, D_MODEL), 1.0),
        'norm1_g': 1.0 + nrm(ks[1], (L, D_MODEL), 0.02),
        'w_in': nrm(ks[2], (L, D_MODEL, IN_WIDTH), D_MODEL ** -0.5),
        'b_in': nrm(ks[3], (L, IN_WIDTH), 0.02),
        'q_norm_g': 1.0 + nrm(ks[4], (L, HEAD_DIM), 0.02),
        'k_norm_g': 1.0 + nrm(ks[5], (L, HEAD_DIM), 0.02),
        'attn_sinks': nrm(ks[6], (L, N_Q_HEADS), 0.5),
        'w_attn_o': nrm(ks[7], (L, Q_WIDTH, D_MODEL), Q_WIDTH ** -0.5),
        'conv_dw_w': nrm(ks[8], (L, CONV_WIDTH, CONV_CH), CONV_WIDTH ** -0.5),
        'conv_dw_b': nrm(ks[9], (L, CONV_CH), 0.02),
        'conv_ln_g': 1.0 + nrm(ks[10], (L, CONV_CH), 0.02),
        'conv_ln_b': nrm(ks[11], (L, CONV_CH), 0.02),
        'w_conv_o': nrm(ks[12], (L, CONV_CH, D_MODEL), CONV_CH ** -0.5),
        'w_out': nrm(ks[13], (L, D_MODEL, D_MODEL), D_MODEL ** -0.5),
        'norm2_g': 1.0 + nrm(ks[14], (L, D_MODEL), 0.02),
        'w_router': nrm(ks[15], (L, D_MODEL, N_EXPERTS), D_MODEL ** -0.5),
        'b_router': nrm(ks[16], (L, N_EXPERTS), 0.01),
        'w_gate_up': nrm(ks[17], (L, N_EXPERTS, D_MODEL, 2 * D_FF), D_MODEL ** -0.5),
        'b_gate_up': nrm(ks[18], (L, N_EXPERTS, 2 * D_FF), 0.02),
        'w_down': nrm(ks[19], (L, N_EXPERTS, D_FF, D_MODEL), D_FF ** -0.5),
        'b_down': nrm(ks[20], (L, N_EXPERTS, D_MODEL), 0.02),
    }


def reference(x, norm1_g, w_in, b_in, q_norm_g, k_norm_g, attn_sinks, w_attn_o,
              conv_dw_w, conv_dw_b, conv_ln_g, conv_ln_b, w_conv_o, w_out, norm2_g,
              w_router, b_router, w_gate_up, b_gate_up, w_down, b_down):
    b, s, d = x.shape
    split_at = [Q_WIDTH, Q_WIDTH + KV_WIDTH, Q_WIDTH + 2 * KV_WIDTH,
                Q_WIDTH + 2 * KV_WIDTH + 2 * CONV_CH, Q_WIDTH + 2 * KV_WIDTH + 2 * CONV_CH + D_MODEL]
    for l in range(DEPTH):
        h = _rmsnorm(x, norm1_g[l])
        proj = h @ w_in[l] + b_in[l]
        q, k, v, conv_in, gate_a, gate_c = jnp.split(proj, split_at, axis=-1)
        q = _rope(_rmsnorm(q.reshape(b, s, N_Q_HEADS, HEAD_DIM), q_norm_g[l]))
        k = _rope(_rmsnorm(k.reshape(b, s, N_KV_HEADS, HEAD_DIM), k_norm_g[l]))
        v = v.reshape(b, s, N_KV_HEADS, HEAD_DIM)
        attn = _sliding_window_attention(q, k, v, attn_sinks[l]) @ w_attn_o[l]
        conv = _conformer_conv(conv_in, conv_dw_w[l], conv_dw_b[l],
                               conv_ln_g[l], conv_ln_b[l]) @ w_conv_o[l]
        merged = jax.nn.sigmoid(gate_a) * attn + jax.nn.sigmoid(gate_c) * conv
        x = x + merged @ w_out[l]
        h2 = _rmsnorm(x, norm2_g[l]).reshape(b * s, d)
        x = x + _moe(h2, w_router[l], b_router[l], w_gate_up[l], b_gate_up[l],
                     w_down[l], b_down[l]).reshape(b, s, d)
    return x
```

```python
import functools

import jax
import jax.numpy as jnp
from jax import lax
from jax.experimental import pallas as pl
from jax.experimental.pallas import tpu as pltpu

D_MODEL = 4096
HEAD_DIM = 64
N_Q_HEADS = 32
N_KV_HEADS = 4
WINDOW = 128
ATTN_BLOCK = 128
ROPE_THETA = 10000.0
CONV_CH = 2048
CONV_WIDTH = 31
N_EXPERTS = 32
TOP_K = 4
D_FF = 2048
SWIGLU_ALPHA = 1.702
SWIGLU_LIMIT = 7.0
MOE_BLOCK = 128
NORM_EPS = 1e-5

Q_WIDTH = N_Q_HEADS * HEAD_DIM
KV_WIDTH = N_KV_HEADS * HEAD_DIM
QKV_WIDTH = Q_WIDTH + 2 * KV_WIDTH

LANES = 128
VMEM_LIMIT = 56 * 1024 * 1024

BF16 = jnp.bfloat16
F32 = jnp.float32


def _cparams(sem, vmem=VMEM_LIMIT):
    return pltpu.CompilerParams(dimension_semantics=sem, vmem_limit_bytes=vmem)


def _rmsnorm_kernel(x_ref, g_ref, o_ref):
    x = x_ref[...]
    ms = jnp.mean(x * x, axis=-1, keepdims=True)
    o_ref[...] = (x * lax.rsqrt(ms + NORM_EPS) * g_ref[...]).astype(o_ref.dtype)


def _rmsnorm(x, g, out_dtype, tm=512):
    t, d = x.shape
    return pl.pallas_call(
        _rmsnorm_kernel,
        out_shape=jax.ShapeDtypeStruct((t, d), out_dtype),
        grid=(t // tm,),
        in_specs=[pl.BlockSpec((tm, d), lambda i: (i, 0)),
                  pl.BlockSpec((1, d), lambda i: (0, 0))],
        out_specs=pl.BlockSpec((tm, d), lambda i: (i, 0)),
        compiler_params=_cparams(("parallel",)),
        name="rmsnorm",
    )(x, g.reshape(1, d))


def _mm_bias_kernel(a_ref, w_ref, b_ref, o_ref, *, sigmoid):
    acc = jnp.dot(a_ref[...], w_ref[...], preferred_element_type=F32) + b_ref[...]
    if sigmoid:
        acc = jax.nn.sigmoid(acc)
    o_ref[...] = acc.astype(o_ref.dtype)


def _mm_bias(a, w, b, out_dtype, *, tm, tn, sigmoid=False, name):
    m, k = a.shape
    n = w.shape[1]
    return pl.pallas_call(
        functools.partial(_mm_bias_kernel, sigmoid=sigmoid),
        out_shape=jax.ShapeDtypeStruct((m, n), out_dtype),
        grid=(m // tm, n // tn),
        in_specs=[pl.BlockSpec((tm, k), lambda i, j: (i, 0)),
                  pl.BlockSpec((k, tn), lambda i, j: (0, j)),
                  pl.BlockSpec((1, tn), lambda i, j: (0, j))],
        out_specs=pl.BlockSpec((tm, tn), lambda i, j: (i, j)),
        compiler_params=_cparams(("parallel", "parallel")),
        name=name,
    )(a, w, b.reshape(1, n))


def _mm_glu_kernel(a_ref, wa_ref, wg_ref, ba_ref, bg_ref, o_ref):
    a = a_ref[...]
    u = jnp.dot(a, wa_ref[...], preferred_element_type=F32) + ba_ref[...]
    g = jnp.dot(a, wg_ref[...], preferred_element_type=F32) + bg_ref[...]
    o_ref[...] = (u * jax.nn.sigmoid(g)).astype(o_ref.dtype)


def _mm_glu(a, wa, wg, ba, bg, *, tm, tn):
    m, k = a.shape
    n = wa.shape[1]
    return pl.pallas_call(
        _mm_glu_kernel,
        out_shape=jax.ShapeDtypeStruct((m, n), F32),
        grid=(m // tm, n // tn),
        in_specs=[pl.BlockSpec((tm, k), lambda i, j: (i, 0)),
                  pl.BlockSpec((k, tn), lambda i, j: (0, j)),
                  pl.BlockSpec((k, tn), lambda i, j: (0, j)),
                  pl.BlockSpec((1, tn), lambda i, j: (0, j)),
                  pl.BlockSpec((1, tn), lambda i, j: (0, j))],
        out_specs=pl.BlockSpec((tm, tn), lambda i, j: (i, j)),
        compiler_params=_cparams(("parallel", "parallel")),
        name="inproj_glu",
    )(a, wa, wg, ba.reshape(1, n), bg.reshape(1, n))


def _merge_kernel(a_ref, c_ref, wa_ref, wc_ref, ga_ref, gc_ref, o_ref):
    pa = jnp.dot(a_ref[...], wa_ref[...], preferred_element_type=F32)
    pc = jnp.dot(c_ref[...], wc_ref[...], preferred_element_type=F32)
    o_ref[...] = (ga_ref[...].astype(F32) * pa + gc_ref[...].astype(F32) * pc).astype(o_ref.dtype)


def _merge(attn, conv, wa, wc, gates, *, tm, tn):
    m, k = attn.shape
    n = wa.shape[1]
    nj = n // tn
    return pl.pallas_call(
        _merge_kernel,
        out_shape=jax.ShapeDtypeStruct((m, n), BF16),
        grid=(m // tm, nj),
        in_specs=[pl.BlockSpec((tm, k), lambda i, j: (i, 0)),
                  pl.BlockSpec((tm, k), lambda i, j: (i, 0)),
                  pl.BlockSpec((k, tn), lambda i, j: (0, j)),
                  pl.BlockSpec((k, tn), lambda i, j: (0, j)),
                  pl.BlockSpec((tm, tn), lambda i, j: (i, j)),
                  pl.BlockSpec((tm, tn), lambda i, j: (i, j + nj))],
        out_specs=pl.BlockSpec((tm, tn), lambda i, j: (i, j)),
        compiler_params=_cparams(("parallel", "parallel")),
        name="merge",
    )(attn, conv, wa, wc, gates, gates)


def _mm_residual_kernel(a_ref, w_ref, x_ref, o_ref):
    o_ref[...] = x_ref[...] + jnp.dot(a_ref[...], w_ref[...], preferred_element_type=F32)


def _mm_residual(a, w, x, *, tm, tn):
    m, k = a.shape
    n = w.shape[1]
    return pl.pallas_call(
        _mm_residual_kernel,
        out_shape=jax.ShapeDtypeStruct((m, n), F32),
        grid=(m // tm, n // tn),
        in_specs=[pl.BlockSpec((tm, k), lambda i, j: (i, 0)),
                  pl.BlockSpec((k, tn), lambda i, j: (0, j)),
                  pl.BlockSpec((tm, tn), lambda i, j: (i, j))],
        out_specs=pl.BlockSpec((tm, tn), lambda i, j: (i, j)),
        compiler_params=_cparams(("parallel", "parallel")),
        name="outproj",
    )(a, w, x)


def _attn_kernel(sink_ref, q_ref, k_ref, v_ref, cos_ref, sin_ref, qg_ref, kg_ref, o_ref,
                 kwin, vwin, kb, vb):
    n = pl.program_id(1)
    blk = ATTN_BLOCK

    @pl.when(n == 0)
    def _():
        kwin[0:blk, :] = jnp.zeros((blk, KV_WIDTH), F32)
        vwin[0:blk, :] = jnp.zeros((blk, KV_WIDTH), F32)

    cos = cos_ref[...]
    sin = sin_ref[...]
    lane = lax.broadcasted_iota(jnp.int32, (blk, LANES), 1)
    first_half = (lane % HEAD_DIM) < (HEAD_DIM // 2)
    seg_r = lax.broadcasted_iota(jnp.int32, (LANES, LANES), 0) // HEAD_DIM
    seg_c = lax.broadcasted_iota(jnp.int32, (LANES, LANES), 1) // HEAD_DIM
    seg = jnp.where(seg_r == seg_c, 1.0, 0.0).astype(BF16)

    def norm_rope(t, g):
        t2 = t * t
        hi = t2.astype(BF16)
        lo = (t2 - hi.astype(F32)).astype(BF16)
        ss = (jnp.dot(hi, seg, preferred_element_type=F32)
              + jnp.dot(lo, seg, preferred_element_type=F32))
        tn = t * lax.rsqrt(ss * (1.0 / HEAD_DIM) + NORM_EPS) * g
        rot = jnp.where(first_half, pltpu.roll(tn, LANES - HEAD_DIM // 2, 1),
                        pltpu.roll(tn, HEAD_DIM // 2, 1))
        return tn * cos + rot * sin

    kg = kg_ref[...]
    qg = qg_ref[...]
    for c in range(KV_WIDTH // LANES):
        cols = slice(c * LANES, (c + 1) * LANES)
        kwin[blk:2 * blk, cols] = norm_rope(k_ref[:, cols], kg)
    vwin[blk:2 * blk, :] = v_ref[...]

    lane2 = lax.broadcasted_iota(jnp.int32, (2 * blk, LANES), 1)
    lo_half = lane2 < HEAD_DIM
    for c in range(KV_WIDTH // LANES):
        cols = slice(c * LANES, (c + 1) * LANES)
        for win, dst in ((kwin, kb), (vwin, vb)):
            x = win[:, cols]
            xs = pltpu.roll(x, HEAD_DIM, 1)
            dst[2 * c, 0:2 * blk, :] = jnp.where(lo_half, x, 0.0).astype(BF16)
            dst[2 * c, 2 * blk:4 * blk, :] = jnp.where(lo_half, 0.0, xs).astype(BF16)
            dst[2 * c + 1, 0:2 * blk, :] = jnp.where(lo_half, xs, 0.0).astype(BF16)
            dst[2 * c + 1, 2 * blk:4 * blk, :] = jnp.where(lo_half, 0.0, x).astype(BF16)

    qi = lax.broadcasted_iota(jnp.int32, (blk, 4 * blk), 0)
    kj = lax.broadcasted_iota(jnp.int32, (blk, 4 * blk), 1) % (2 * blk)
    rel = qi + blk - kj
    first_key = jnp.where(n == 0, blk, 0)
    mask = (rel >= 0) & (rel < WINDOW) & (kj >= first_key)
    lo_out = lane < HEAD_DIM

    for p in range(N_Q_HEADS // 2):
        g = (2 * p) // (N_Q_HEADS // N_KV_HEADS)
        cols = slice(p * LANES, (p + 1) * LANES)
        q2 = (norm_rope(q_ref[:, cols], qg) * (HEAD_DIM ** -0.5)).astype(BF16)
        s = lax.dot_general(q2, kb[g], (((1,), (1,)), ((), ())), preferred_element_type=F32)
        s = jnp.where(mask, s, -jnp.inf)
        es, invs = [], []
        for h in range(2):
            sh = s[:, h * 2 * blk:(h + 1) * 2 * blk]
            sink = sink_ref[2 * p + h]
            m = jnp.maximum(jnp.max(sh, axis=-1, keepdims=True), sink)
            e = jnp.exp(sh - m)
            den = jnp.sum(e, axis=-1, keepdims=True) + jnp.exp(sink - m)
            es.append(e)
            invs.append(1.0 / den)
        pcat = jnp.concatenate(es, axis=1).astype(BF16)
        o2 = jnp.dot(pcat, vb[g], preferred_element_type=F32)
        inv = jnp.where(lo_out, invs[0], invs[1])
        o_ref[:, cols] = (o2 * inv).astype(o_ref.dtype)

    kwin[0:blk, :] = kwin[blk:2 * blk, :]
    vwin[0:blk, :] = vwin[blk:2 * blk, :]


def _attention(qkv, sinks, cos_t, sin_t, qg, kg, batch, seq):
    t = qkv.shape[0]
    blk = ATTN_BLOCK
    nb = seq // blk
    kcol = Q_WIDTH // KV_WIDTH
    return pl.pallas_call(
        _attn_kernel,
        out_shape=jax.ShapeDtypeStruct((t, Q_WIDTH), BF16),
        grid=(batch, nb),
        in_specs=[pl.BlockSpec(memory_space=pltpu.SMEM),
                  pl.BlockSpec((blk, Q_WIDTH), lambda b, n: (b * nb + n, 0)),
                  pl.BlockSpec((blk, KV_WIDTH), lambda b, n: (b * nb + n, kcol)),
                  pl.BlockSpec((blk, KV_WIDTH), lambda b, n: (b * nb + n, kcol + 1)),
                  pl.BlockSpec((blk, LANES), lambda b, n: (n, 0)),
                  pl.BlockSpec((blk, LANES), lambda b, n: (n, 0)),
                  pl.BlockSpec((1, LANES), lambda b, n: (0, 0)),
                  pl.BlockSpec((1, LANES), lambda b, n: (0, 0))],
        out_specs=pl.BlockSpec((blk, Q_WIDTH), lambda b, n: (b * nb + n, 0)),
        scratch_shapes=[pltpu.VMEM((2 * blk, KV_WIDTH), F32),
                        pltpu.VMEM((2 * blk, KV_WIDTH), F32),
                        pltpu.VMEM((N_KV_HEADS, 4 * blk, LANES), BF16),
                        pltpu.VMEM((N_KV_HEADS, 4 * blk, LANES), BF16)],
        compiler_params=_cparams(("arbitrary", "arbitrary")),
        name="swa_attention",
    )(sinks, qkv, qkv, qkv, cos_t, sin_t, qg, kg)


CONV_TS = 256
CONV_HALO = 32
CONV_RB = 128


def _conv_kernel(z_ref, w_ref, b_ref, lg_ref, lb_ref, o_ref, zbuf, cbuf):
    n = pl.program_id(1)

    @pl.when(n == 0)
    def _():
        zbuf[0:CONV_HALO, :] = jnp.zeros((CONV_HALO, CONV_CH), F32)

    zbuf[CONV_HALO:CONV_HALO + CONV_TS, :] = z_ref[...]
    shift = CONV_HALO - (CONV_WIDTH - 1)

    def chunk(c, carry):
        c0 = pl.multiple_of(c * LANES, LANES)
        for rb in range(CONV_TS // CONV_RB):
            acc = jnp.broadcast_to(b_ref[:, pl.ds(c0, LANES)], (CONV_RB, LANES))
            for j in range(CONV_WIDTH):
                r0 = rb * CONV_RB + shift + j
                acc = acc + zbuf[r0:r0 + CONV_RB, pl.ds(c0, LANES)] * w_ref[j:j + 1, pl.ds(c0, LANES)]
            cbuf[rb * CONV_RB:(rb + 1) * CONV_RB, pl.ds(c0, LANES)] = acc
        return carry

    lax.fori_loop(0, CONV_CH // LANES, chunk, 0)

    zbuf[0:CONV_HALO, :] = zbuf[CONV_TS:CONV_TS + CONV_HALO, :]

    y = cbuf[...]
    mu = jnp.mean(y, axis=-1, keepdims=True)
    yc = y - mu
    var = jnp.mean(yc * yc, axis=-1, keepdims=True)
    yn = yc * lax.rsqrt(var + NORM_EPS) * lg_ref[...] + lb_ref[...]
    o_ref[...] = (yn * jax.nn.sigmoid(yn)).astype(o_ref.dtype)


def _conformer_conv(z, dw_w, dw_b, ln_g, ln_b, batch, seq):
    t = z.shape[0]
    ns = seq // CONV_TS
    vec = lambda a: a.reshape(1, CONV_CH)
    return pl.pallas_call(
        _conv_kernel,
        out_shape=jax.ShapeDtypeStruct((t, CONV_CH), BF16),
        grid=(batch, ns),
        in_specs=[pl.BlockSpec((CONV_TS, CONV_CH), lambda b, n: (b * ns + n, 0)),
                  pl.BlockSpec((CONV_WIDTH, CONV_CH), lambda b, n: (0, 0)),
                  pl.BlockSpec((1, CONV_CH), lambda b, n: (0, 0)),
                  pl.BlockSpec((1, CONV_CH), lambda b, n: (0, 0)),
                  pl.BlockSpec((1, CONV_CH), lambda b, n: (0, 0))],
        out_specs=pl.BlockSpec((CONV_TS, CONV_CH), lambda b, n: (b * ns + n, 0)),
        scratch_shapes=[pltpu.VMEM((CONV_HALO + CONV_TS, CONV_CH), F32),
                        pltpu.VMEM((CONV_TS, CONV_CH), F32)],
        compiler_params=_cparams(("arbitrary", "arbitrary")),
        name="conformer_conv",
    )(z, dw_w, vec(dw_b), vec(ln_g), vec(ln_b))


ROUTER_TM = 256


def _router_kernel(x_ref, g_ref, wr_ref, br_ref, h_ref, idx_ref, gate_ref):
    x = x_ref[...]
    ms = jnp.mean(x * x, axis=-1, keepdims=True)
    h = x * lax.rsqrt(ms + NORM_EPS) * g_ref[...]
    h_ref[...] = h
    vals = jnp.dot(h.astype(BF16), wr_ref[...], preferred_element_type=F32) + br_ref[...]
    lane = lax.broadcasted_iota(jnp.int32, vals.shape, 1).astype(F32)
    tops, idxs = [], []
    for _ in range(TOP_K):
        m = jnp.max(vals, axis=-1, keepdims=True)
        idx = jnp.min(jnp.where(vals == m, lane, float(LANES)), axis=-1, keepdims=True)
        tops.append(m)
        idxs.append(idx)
        vals = jnp.where(lane == idx, -jnp.inf, vals)
    es = [jnp.exp(v - tops[0]) for v in tops]
    den = es[0] + es[1] + es[2] + es[3]
    idx_out = jnp.zeros(vals.shape, F32)
    gate_out = jnp.zeros(vals.shape, F32)
    for k in range(TOP_K):
        idx_out = jnp.where(lane == float(k), idxs[k], idx_out)
        gate_out = jnp.where(lane == float(k), es[k] / den, gate_out)
    idx_ref[...] = idx_out.astype(jnp.int32)
    gate_ref[...] = gate_out


def _router(x1, g, w_router, b_router):
    t, d = x1.shape
    tm = ROUTER_TM
    wr = jnp.zeros((d, LANES), BF16).at[:, :N_EXPERTS].set(w_router.astype(BF16))
    br = jnp.full((1, LANES), -1e30, F32).at[0, :N_EXPERTS].set(b_router)
    return pl.pallas_call(
        _router_kernel,
        out_shape=(jax.ShapeDtypeStruct((t, d), F32),
                   jax.ShapeDtypeStruct((t, LANES), jnp.int32),
                   jax.ShapeDtypeStruct((t, LANES), F32)),
        grid=(t // tm,),
        in_specs=[pl.BlockSpec((tm, d), lambda i: (i, 0)),
                  pl.BlockSpec((1, d), lambda i: (0, 0)),
                  pl.BlockSpec((d, LANES), lambda i: (0, 0)),
                  pl.BlockSpec((1, LANES), lambda i: (0, 0))],
        out_specs=(pl.BlockSpec((tm, d), lambda i: (i, 0)),
                   pl.BlockSpec((tm, LANES), lambda i: (i, 0)),
                   pl.BlockSpec((tm, LANES), lambda i: (i, 0))),
        compiler_params=_cparams(("parallel",)),
        name="router",
    )(x1, g.reshape(1, d), wr, br)


GATHER_ROWS = 256


def _gather_kernel(idx_ref, nxt_ref, src_hbm, o_ref, buf, sem):
    i = pl.program_id(0)
    nsteps = pl.num_programs(0)
    slot = lax.rem(i, 2)

    def issue(ids, s):
        def body(r, carry):
            pltpu.make_async_copy(src_hbm.at[pl.ds(ids[0, 0, r], 1), :],
                                  buf.at[s, pl.ds(r, 1), :], sem.at[s]).start()
            return carry
        lax.fori_loop(0, GATHER_ROWS, body, 0)

    @pl.when(i == 0)
    def _():
        issue(idx_ref, 0)

    @pl.when(i + 1 < nsteps)
    def _():
        issue(nxt_ref, 1 - slot)

    pltpu.make_async_copy(src_hbm.at[pl.ds(0, GATHER_ROWS), :], buf.at[slot], sem.at[slot]).wait()
    o_ref[...] = buf[slot].astype(o_ref.dtype)


def _gather_rows(src, row_idx, out_dtype):
    n = row_idx.shape[0]
    d = src.shape[1]
    nsteps = n // GATHER_ROWS
    idx3 = row_idx.reshape(nsteps, 1, GATHER_ROWS)
    return pl.pallas_call(
        _gather_kernel,
        out_shape=jax.ShapeDtypeStruct((n, d), out_dtype),
        grid=(nsteps,),
        in_specs=[pl.BlockSpec((1, 1, GATHER_ROWS), lambda i: (i, 0, 0), memory_space=pltpu.SMEM),
                  pl.BlockSpec((1, 1, GATHER_ROWS), lambda i: (jnp.minimum(i + 1, nsteps - 1), 0, 0),
                               memory_space=pltpu.SMEM),
                  pl.BlockSpec(memory_space=pl.ANY)],
        out_specs=pl.BlockSpec((GATHER_ROWS, d), lambda i: (i, 0)),
        scratch_shapes=[pltpu.VMEM((2, GATHER_ROWS, d), src.dtype),
                        pltpu.SemaphoreType.DMA((2,))],
        compiler_params=_cparams(("arbitrary",)),
        name="dispatch_gather",
    )(idx3, idx3, src)


MOE_NBT = 10
MOE_TM = MOE_NBT * MOE_BLOCK
MOE_SUB = 2 * MOE_BLOCK
MOE_FC = 256
MOE_NC = 512
MOE_S1 = D_FF // MOE_FC
MOE_S2 = D_MODEL // MOE_NC


def _moe_max_blocks(t):
    return t * TOP_K // MOE_BLOCK + N_EXPERTS


def _moe_max_tiles(t):
    return (_moe_max_blocks(t) + N_EXPERTS * (MOE_NBT - 1)) // MOE_NBT + 1


def _moe_kernel(te_ref, tb_ref, tn_ref, nt_ref,
                xs_hbm, wg_ref, wl_ref, bg_ref, bl_ref, wd_ref, bd_ref, ys_hbm,
                xbuf, act, wcat, wdb, obuf, xsem, osem):
    i = pl.program_id(0)
    s = pl.program_id(1)
    ntiles = nt_ref[0]
    nsub = MOE_NBT // 2

    def xs_copy(tile, j):
        row0 = pl.multiple_of((tb_ref[tile] + 2 * j) * MOE_BLOCK, MOE_BLOCK)
        return pltpu.make_async_copy(xs_hbm.at[pl.ds(row0, MOE_SUB), :],
                                     xbuf.at[pl.ds(j * MOE_SUB, MOE_SUB), :], xsem.at[0])

    def out_copy(slot, b, n):
        row0 = pl.multiple_of((tb_ref[i] + b) * MOE_BLOCK, MOE_BLOCK)
        col0 = pl.multiple_of(n * MOE_NC, MOE_NC)
        return pltpu.make_async_copy(obuf.at[slot, pl.ds(b * MOE_BLOCK, MOE_BLOCK), :],
                                     ys_hbm.at[pl.ds(row0, MOE_BLOCK), pl.ds(col0, MOE_NC)],
                                     osem.at[slot])

    @pl.when(i < ntiles)
    def _():
        nb = tn_ref[i]

        @pl.when((i == 0) & (s == 0))
        def _():
            for j in range(nsub):
                @pl.when(2 * j < nb)
                def _():
                    xs_copy(i, j).start()

        @pl.when(s == 0)
        def _():
            for j in range(nsub):
                @pl.when(2 * j < nb)
                def _():
                    xs_copy(i, j).wait()

        @pl.when(s < MOE_S1)
        def _():
            wcat[:, 0:MOE_FC] = wg_ref[...].astype(BF16)
            wcat[:, MOE_FC:2 * MOE_FC] = wl_ref[...].astype(BF16)
            bg = bg_ref[...]
            bl = bl_ref[...]
            c0 = pl.multiple_of(s * MOE_FC, MOE_FC)
            for j in range(nsub):
                @pl.when(2 * j < nb)
                def _():
                    rows = pl.ds(j * MOE_SUB, MOE_SUB)
                    gu = jnp.dot(xbuf[rows, :], wcat[...], preferred_element_type=F32)
                    glu = jnp.minimum(gu[:, 0:MOE_FC] + bg, SWIGLU_LIMIT)
                    lin = jnp.clip(gu[:, MOE_FC:2 * MOE_FC] + bl, -SWIGLU_LIMIT, SWIGLU_LIMIT)
                    a = glu * jax.nn.sigmoid(SWIGLU_ALPHA * glu) * (lin + 1.0)
                    act[rows, pl.ds(c0, MOE_FC)] = a.astype(BF16)

        @pl.when(s >= MOE_S1)
        def _():
            n = s - MOE_S1
            slot = lax.rem(n, 2)

            @pl.when((s == MOE_S1) & (i + 1 < ntiles))
            def _():
                nb_next = tn_ref[i + 1]
                for j in range(nsub):
                    @pl.when(2 * j < nb_next)
                    def _():
                        xs_copy(i + 1, j).start()

            wdb[...] = wd_ref[...].astype(BF16)
            bd = bd_ref[...]
            for j in range(nsub):
                @pl.when(2 * j < nb)
                def _():
                    rows = pl.ds(j * MOE_SUB, MOE_SUB)
                    y = jnp.dot(act[rows, :], wdb[...], preferred_element_type=F32) + bd
                    obuf[slot, rows, :] = y
            for b in range(MOE_NBT):
                @pl.when(b < nb)
                def _():
                    out_copy(slot, b, n).start()

            @pl.when(n > 0)
            def _():
                for b in range(MOE_NBT):
                    @pl.when(b < nb)
                    def _():
                        out_copy(1 - slot, b, n - 1).wait()

            @pl.when(n == MOE_S2 - 1)
            def _():
                for b in range(MOE_NBT):
                    @pl.when(b < nb)
                    def _():
                        out_copy(slot, b, n).wait()

    @pl.when(i == ntiles)
    def _():
        @pl.when(s == 0)
        def _():
            obuf[0, 0:MOE_BLOCK, :] = jnp.zeros((MOE_BLOCK, MOE_NC), F32)

        first_tail = tb_ref[ntiles - 1] + tn_ref[ntiles - 1]
        per_step = -(-N_EXPERTS // (MOE_S1 + MOE_S2))
        for u in range(per_step):
            blk = first_tail + s * per_step + u

            @pl.when(blk < ys_hbm.shape[0] // MOE_BLOCK)
            def _():
                row0 = pl.multiple_of(blk * MOE_BLOCK, MOE_BLOCK)
                fills = [pltpu.make_async_copy(
                    obuf.at[0, pl.ds(0, MOE_BLOCK), :],
                    ys_hbm.at[pl.ds(row0, MOE_BLOCK), pl.ds(c * MOE_NC, MOE_NC)], osem.at[0])
                    for c in range(MOE_S2)]
                for f in fills:
                    f.start()
                for f in fills:
                    f.wait()


def _moe_experts(xs, n_rows, tile_e, tile_b0, tile_nb, n_tiles, w_gate_up, b_gate_up, w_down, b_down):

    def live(i, nt):
        return i < nt[0]

    def f_idx(i, s, nt):
        return jnp.where(live(i, nt), jnp.minimum(s, MOE_S1 - 1), MOE_S1 - 1)

    def n_idx(i, s, nt):
        return jnp.where(live(i, nt), jnp.maximum(s - MOE_S1, 0), MOE_S2 - 1)

    lin_off = D_FF // MOE_FC
    grid_spec = pltpu.PrefetchScalarGridSpec(
        num_scalar_prefetch=4,
        grid=(tile_e.shape[0], MOE_S1 + MOE_S2),
        in_specs=[
            pl.BlockSpec(memory_space=pl.ANY),
            pl.BlockSpec((None, D_MODEL, MOE_FC), lambda i, s, te, tb, tn, nt: (te[i], 0, f_idx(i, s, nt))),
            pl.BlockSpec((None, D_MODEL, MOE_FC),
                         lambda i, s, te, tb, tn, nt: (te[i], 0, lin_off + f_idx(i, s, nt))),
            pl.BlockSpec((None, 1, MOE_FC), lambda i, s, te, tb, tn, nt: (te[i], 0, f_idx(i, s, nt))),
            pl.BlockSpec((None, 1, MOE_FC),
                         lambda i, s, te, tb, tn, nt: (te[i], 0, lin_off + f_idx(i, s, nt))),
            pl.BlockSpec((None, D_FF, MOE_NC), lambda i, s, te, tb, tn, nt: (te[i], 0, n_idx(i, s, nt))),
            pl.BlockSpec((None, 1, MOE_NC), lambda i, s, te, tb, tn, nt: (te[i], 0, n_idx(i, s, nt))),
        ],
        out_specs=pl.BlockSpec(memory_space=pl.ANY),
        scratch_shapes=[pltpu.VMEM((MOE_TM, D_MODEL), BF16),
                        pltpu.VMEM((MOE_TM, D_FF), BF16),
                        pltpu.VMEM((D_MODEL, 2 * MOE_FC), BF16),
                        pltpu.VMEM((D_FF, MOE_NC), BF16),
                        pltpu.VMEM((2, MOE_TM, MOE_NC), F32),
                        pltpu.SemaphoreType.DMA((1,)),
                        pltpu.SemaphoreType.DMA((2,))],
    )
    return pl.pallas_call(
        _moe_kernel,
        out_shape=jax.ShapeDtypeStruct((n_rows, D_MODEL), F32),
        grid_spec=grid_spec,
        compiler_params=_cparams(("arbitrary", "arbitrary")),
        name="moe_experts",
    )(tile_e, tile_b0, tile_nb, n_tiles, xs, w_gate_up, w_gate_up,
      b_gate_up.reshape(N_EXPERTS, 1, 2 * D_FF), b_gate_up.reshape(N_EXPERTS, 1, 2 * D_FF),
      w_down, b_down.reshape(N_EXPERTS, 1, D_MODEL))


COMBINE_TT = 128


def _combine_kernel(pos_ref, nxt_ref, x_ref, gate_ref, ys_hbm, o_ref, buf, sem):
    i = pl.program_id(0)
    nsteps = pl.num_programs(0)
    slot = lax.rem(i, 2)

    def issue(ids, s):
        for k in range(TOP_K):
            def body(r, carry):
                pltpu.make_async_copy(ys_hbm.at[pl.ds(ids[0, 0, r * TOP_K + k], 1), :],
                                      buf.at[s, k, pl.ds(r, 1), :], sem.at[s]).start()
                return carry
            lax.fori_loop(0, COMBINE_TT, body, 0)

    @pl.when(i == 0)
    def _():
        issue(pos_ref, 0)

    @pl.when(i + 1 < nsteps)
    def _():
        issue(nxt_ref, 1 - slot)

    for k in range(TOP_K):
        pltpu.make_async_copy(ys_hbm.at[pl.ds(0, COMBINE_TT), :], buf.at[slot, k], sem.at[slot]).wait()
    gate = gate_ref[...]
    acc = x_ref[...]
    moe = gate[:, 0:1] * buf[slot, 0]
    for k in range(1, TOP_K):
        moe = moe + gate[:, k:k + 1] * buf[slot, k]
    o_ref[...] = acc + moe


def _combine(x1, gates, ys, pos):
    t, d = x1.shape
    tt = COMBINE_TT
    nsteps = t // tt
    pos3 = pos.reshape(nsteps, 1, tt * TOP_K)
    return pl.pallas_call(
        _combine_kernel,
        out_shape=jax.ShapeDtypeStruct((t, d), F32),
        grid=(nsteps,),
        in_specs=[pl.BlockSpec((1, 1, tt * TOP_K), lambda i: (i, 0, 0), memory_space=pltpu.SMEM),
                  pl.BlockSpec((1, 1, tt * TOP_K), lambda i: (jnp.minimum(i + 1, nsteps - 1), 0, 0),
                               memory_space=pltpu.SMEM),
                  pl.BlockSpec((tt, d), lambda i: (i, 0)),
                  pl.BlockSpec((tt, LANES), lambda i: (i, 0)),
                  pl.BlockSpec(memory_space=pl.ANY)],
        out_specs=pl.BlockSpec((tt, d), lambda i: (i, 0)),
        scratch_shapes=[pltpu.VMEM((2, TOP_K, tt, d), F32),
                        pltpu.SemaphoreType.DMA((2,))],
        compiler_params=_cparams(("arbitrary",)),
        name="combine",
    )(pos3, pos3, x1, gates, ys)


def _routing(top_idx):
    t = top_idx.shape[0]
    n_assign = t * TOP_K
    flat_e = top_idx.reshape(-1)
    flat_tok = jnp.repeat(jnp.arange(t, dtype=jnp.int32), TOP_K)
    order = jnp.argsort(flat_e)
    sorted_e = flat_e[order]
    counts = jnp.bincount(flat_e, length=N_EXPERTS).astype(jnp.int32)
    starts = jnp.cumsum(counts) - counts
    padded = (counts + MOE_BLOCK - 1) // MOE_BLOCK * MOE_BLOCK
    pad_starts = jnp.cumsum(padded) - padded
    dest = pad_starts[sorted_e] + jnp.arange(n_assign, dtype=jnp.int32) - starts[sorted_e]
    n_rows = n_assign + N_EXPERTS * MOE_BLOCK + MOE_TM
    n_rows = (n_rows + GATHER_ROWS - 1) // GATHER_ROWS * GATHER_ROWS
    row_tok = jnp.zeros((n_rows,), jnp.int32).at[dest].set(flat_tok[order])
    pos = jnp.zeros((n_assign,), jnp.int32).at[order].set(dest)

    nblk = padded // MOE_BLOCK
    blk0 = pad_starts // MOE_BLOCK
    ntile = (nblk + MOE_NBT - 1) // MOE_NBT
    tile_end = jnp.cumsum(ntile)
    n_tiles = tile_end[-1]
    ids = jnp.arange(_moe_max_tiles(t), dtype=jnp.int32)
    ids_c = jnp.minimum(ids, n_tiles - 1)
    tile_e = jnp.minimum(jnp.searchsorted(tile_end, ids_c, side='right'), N_EXPERTS - 1).astype(jnp.int32)
    local = ids_c - (tile_end - ntile)[tile_e]
    tile_b0 = (blk0[tile_e] + local * MOE_NBT).astype(jnp.int32)
    tile_nb = jnp.where(ids < n_tiles, jnp.clip(nblk[tile_e] - local * MOE_NBT, 0, MOE_NBT), 0).astype(jnp.int32)
    return row_tok, pos, tile_e, tile_b0, tile_nb, n_tiles.reshape(1).astype(jnp.int32)


def _rope_tables(seq):
    inv_freq = 1.0 / (ROPE_THETA ** (jnp.arange(0, HEAD_DIM, 2, dtype=F32) / HEAD_DIM))
    ang = jnp.arange(seq, dtype=F32)[:, None] * inv_freq[None, :]
    cos = jnp.cos(ang)
    sin = jnp.sin(ang)
    reps = LANES // HEAD_DIM
    cos_t = jnp.tile(jnp.concatenate([cos, cos], axis=-1), (1, reps))
    sin_t = jnp.tile(jnp.concatenate([-sin, sin], axis=-1), (1, reps))
    return cos_t, sin_t


def kernel(x, norm1_g, w_in, b_in, q_norm_g, k_norm_g, attn_sinks, w_attn_o, conv_dw_w, conv_dw_b,
           conv_ln_g, conv_ln_b, w_conv_o, w_out, norm2_g, w_router, b_router, w_gate_up, b_gate_up,
           w_down, b_down):
    b, s, d = x.shape
    t = b * s
    depth = norm1_g.shape[0]
    xt = x.reshape(t, d)
    cos_t, sin_t = _rope_tables(s)
    reps = LANES // HEAD_DIM
    c0 = QKV_WIDTH
    c1 = c0 + CONV_CH
    c2 = c1 + CONV_CH
    for l in range(depth):
        h = _rmsnorm(xt, norm1_g[l], BF16)
        wl, bl = w_in[l], b_in[l]
        qkv = _mm_bias(h, wl[:, :c0].astype(BF16), bl[:c0], F32, tm=1024, tn=512, name="inproj_qkv")
        z = _mm_glu(h, wl[:, c0:c1].astype(BF16), wl[:, c1:c2].astype(BF16), bl[c0:c1], bl[c1:c2],
                    tm=1024, tn=512)
        gates = _mm_bias(h, wl[:, c2:].astype(BF16), bl[c2:], BF16, tm=1024, tn=1024, sigmoid=True,
                         name="inproj_gates")
        attn = _attention(qkv, attn_sinks[l], cos_t, sin_t,
                          jnp.tile(q_norm_g[l], reps).reshape(1, LANES),
                          jnp.tile(k_norm_g[l], reps).reshape(1, LANES), b, s)
        conv = _conformer_conv(z, conv_dw_w[l], conv_dw_b[l], conv_ln_g[l], conv_ln_b[l], b, s)
        merged = _merge(attn, conv, w_attn_o[l].astype(BF16), w_conv_o[l].astype(BF16), gates,
                        tm=1024, tn=512)
        x1 = _mm_residual(merged, w_out[l].astype(BF16), xt, tm=1024, tn=1024)
        h2, idx_pad, gate_pad = _router(x1, norm2_g[l], w_router[l], b_router[l])
        row_tok, pos, tile_e, tile_b0, tile_nb, n_tiles = _routing(idx_pad[:, :TOP_K])
        xs = _gather_rows(h2, row_tok, BF16)
        ys = _moe_experts(xs, _moe_max_blocks(t) * MOE_BLOCK, tile_e, tile_b0, tile_nb, n_tiles,
                          w_gate_up[l], b_gate_up[l], w_down[l], b_down[l])
        xt = _combine(x1, gate_pad, ys, pos)
    return xt.reshape(b, s, d)
```

```python
import functools

import jax
import jax.numpy as jnp
from jax import lax
from jax.experimental import pallas as pl
from jax.experimental.pallas import tpu as pltpu

D_MODEL = 4096
HEAD_DIM = 64
N_Q_HEADS = 32
N_KV_HEADS = 4
WINDOW = 128
ATTN_BLOCK = 128
ROPE_THETA = 10000.0
CONV_CH = 2048
CONV_WIDTH = 31
N_EXPERTS = 32
TOP_K = 4
D_FF = 2048
SWIGLU_ALPHA = 1.702
SWIGLU_LIMIT = 7.0
MOE_BLOCK = 128
NORM_EPS = 1e-5

Q_WIDTH = N_Q_HEADS * HEAD_DIM
KV_WIDTH = N_KV_HEADS * HEAD_DIM
QKV_WIDTH = Q_WIDTH + 2 * KV_WIDTH

LANES = 128
VMEM_LIMIT = 56 * 1024 * 1024

BF16 = jnp.bfloat16
F32 = jnp.float32


def _cparams(sem, vmem=VMEM_LIMIT):
    return pltpu.CompilerParams(dimension_semantics=sem, vmem_limit_bytes=vmem)


def _rmsnorm_kernel(x_ref, g_ref, o_ref):
    x = x_ref[...]
    ms = jnp.mean(x * x, axis=-1, keepdims=True)
    o_ref[...] = (x * lax.rsqrt(ms + NORM_EPS) * g_ref[...]).astype(o_ref.dtype)


def _rmsnorm(x, g, out_dtype, tm=512):
    t, d = x.shape
    return pl.pallas_call(
        _rmsnorm_kernel,
        out_shape=jax.ShapeDtypeStruct((t, d), out_dtype),
        grid=(t // tm,),
        in_specs=[pl.BlockSpec((tm, d), lambda i: (i, 0)),
                  pl.BlockSpec((1, d), lambda i: (0, 0))],
        out_specs=pl.BlockSpec((tm, d), lambda i: (i, 0)),
        compiler_params=_cparams(("parallel",)),
        name="rmsnorm",
    )(x, g.reshape(1, d))


def _mm_bias_kernel(a_ref, w_ref, b_ref, o_ref, *, sigmoid):
    acc = jnp.dot(a_ref[...], w_ref[...], preferred_element_type=F32) + b_ref[...]
    if sigmoid:
        acc = jax.nn.sigmoid(acc)
    o_ref[...] = acc.astype(o_ref.dtype)


def _mm_bias(a, w, b, out_dtype, *, tm, tn, sigmoid=False, name):
    m, k = a.shape
    n = w.shape[1]
    return pl.pallas_call(
        functools.partial(_mm_bias_kernel, sigmoid=sigmoid),
        out_shape=jax.ShapeDtypeStruct((m, n), out_dtype),
        grid=(m // tm, n // tn),
        in_specs=[pl.BlockSpec((tm, k), lambda i, j: (i, 0)),
                  pl.BlockSpec((k, tn), lambda i, j: (0, j)),
                  pl.BlockSpec((1, tn), lambda i, j: (0, j))],
        out_specs=pl.BlockSpec((tm, tn), lambda i, j: (i, j)),
        compiler_params=_cparams(("parallel", "parallel")),
        name=name,
    )(a, w, b.reshape(1, n))


def _mm_glu_kernel(a_ref, wa_ref, wg_ref, ba_ref, bg_ref, o_ref):
    a = a_ref[...]
    u = jnp.dot(a, wa_ref[...], preferred_element_type=F32) + ba_ref[...]
    g = jnp.dot(a, wg_ref[...], preferred_element_type=F32) + bg_ref[...]
    o_ref[...] = (u * jax.nn.sigmoid(g)).astype(o_ref.dtype)


def _mm_glu(a, wa, wg, ba, bg, *, tm, tn):
    m, k = a.shape
    n = wa.shape[1]
    return pl.pallas_call(
        _mm_glu_kernel,
        out_shape=jax.ShapeDtypeStruct((m, n), F32),
        grid=(m // tm, n // tn),
        in_specs=[pl.BlockSpec((tm, k), lambda i, j: (i, 0)),
                  pl.BlockSpec((k, tn), lambda i, j: (0, j)),
                  pl.BlockSpec((k, tn), lambda i, j: (0, j)),
                  pl.BlockSpec((1, tn), lambda i, j: (0, j)),
                  pl.BlockSpec((1, tn), lambda i, j: (0, j))],
        out_specs=pl.BlockSpec((tm, tn), lambda i, j: (i, j)),
        compiler_params=_cparams(("parallel", "parallel")),
        name="inproj_glu",
    )(a, wa, wg, ba.reshape(1, n), bg.reshape(1, n))


def _merge_kernel(a_ref, c_ref, wa_ref, wc_ref, ga_ref, gc_ref, o_ref):
    pa = jnp.dot(a_ref[...], wa_ref[...], preferred_element_type=F32)
    pc = jnp.dot(c_ref[...], wc_ref[...], preferred_element_type=F32)
    o_ref[...] = (ga_ref[...].astype(F32) * pa + gc_ref[...].astype(F32) * pc).astype(o_ref.dtype)


def _merge(attn, conv, wa, wc, gates, *, tm, tn):
    m, k = attn.shape
    n = wa.shape[1]
    nj = n // tn
    return pl.pallas_call(
        _merge_kernel,
        out_shape=jax.ShapeDtypeStruct((m, n), BF16),
        grid=(m // tm, nj),
        in_specs=[pl.BlockSpec((tm, k), lambda i, j: (i, 0)),
                  pl.BlockSpec((tm, k), lambda i, j: (i, 0)),
                  pl.BlockSpec((k, tn), lambda i, j: (0, j)),
                  pl.BlockSpec((k, tn), lambda i, j: (0, j)),
                  pl.BlockSpec((tm, tn), lambda i, j: (i, j)),
                  pl.BlockSpec((tm, tn), lambda i, j: (i, j + nj))],
        out_specs=pl.BlockSpec((tm, tn), lambda i, j: (i, j)),
        compiler_params=_cparams(("parallel", "parallel")),
        name="merge",
    )(attn, conv, wa, wc, gates, gates)


def _mm_residual_kernel(a_ref, w_ref, x_ref, o_ref):
    o_ref[...] = x_ref[...] + jnp.dot(a_ref[...], w_ref[...], preferred_element_type=F32)


def _mm_residual(a, w, x, *, tm, tn):
    m, k = a.shape
    n = w.shape[1]
    return pl.pallas_call(
        _mm_residual_kernel,
        out_shape=jax.ShapeDtypeStruct((m, n), F32),
        grid=(m // tm, n // tn),
        in_specs=[pl.BlockSpec((tm, k), lambda i, j: (i, 0)),
                  pl.BlockSpec((k, tn), lambda i, j: (0, j)),
                  pl.BlockSpec((tm, tn), lambda i, j: (i, j))],
        out_specs=pl.BlockSpec((tm, tn), lambda i, j: (i, j)),
        compiler_params=_cparams(("parallel", "parallel")),
        name="outproj",
    )(a, w, x)


def _attn_kernel(sink_ref, q_ref, k_ref, v_ref, cos_ref, sin_ref, qg_ref, kg_ref, o_ref,
                 kwin, vwin, kb, vb):
    n = pl.program_id(1)
    blk = ATTN_BLOCK

    @pl.when(n == 0)
    def _():
        kwin[0:blk, :] = jnp.zeros((blk, KV_WIDTH), F32)
        vwin[0:blk, :] = jnp.zeros((blk, KV_WIDTH), F32)

    cos = cos_ref[...]
    sin = sin_ref[...]
    lane = lax.broadcasted_iota(jnp.int32, (blk, LANES), 1)
    first_half = (lane % HEAD_DIM) < (HEAD_DIM // 2)
    seg_r = lax.broadcasted_iota(jnp.int32, (LANES, LANES), 0) // HEAD_DIM
    seg_c = lax.broadcasted_iota(jnp.int32, (LANES, LANES), 1) // HEAD_DIM
    seg = jnp.where(seg_r == seg_c, 1.0, 0.0).astype(BF16)

    def norm_rope(t, g):
        t2 = t * t
        hi = t2.astype(BF16)
        lo = (t2 - hi.astype(F32)).astype(BF16)
        ss = (jnp.dot(hi, seg, preferred_element_type=F32)
              + jnp.dot(lo, seg, preferred_element_type=F32))
        tn = t * lax.rsqrt(ss * (1.0 / HEAD_DIM) + NORM_EPS) * g
        rot = jnp.where(first_half, pltpu.roll(tn, LANES - HEAD_DIM // 2, 1),
                        pltpu.roll(tn, HEAD_DIM // 2, 1))
        return tn * cos + rot * sin

    kg = kg_ref[...]
    qg = qg_ref[...]
    for c in range(KV_WIDTH // LANES):
        cols = slice(c * LANES, (c + 1) * LANES)
        kwin[blk:2 * blk, cols] = norm_rope(k_ref[:, cols], kg)
    vwin[blk:2 * blk, :] = v_ref[...]

    lane2 = lax.broadcasted_iota(jnp.int32, (2 * blk, LANES), 1)
    lo_half = lane2 < HEAD_DIM
    for c in range(KV_WIDTH // LANES):
        cols = slice(c * LANES, (c + 1) * LANES)
        for win, dst in ((kwin, kb), (vwin, vb)):
            x = win[:, cols]
            xs = pltpu.roll(x, HEAD_DIM, 1)
            dst[2 * c, 0:2 * blk, :] = jnp.where(lo_half, x, 0.0).astype(BF16)
            dst[2 * c, 2 * blk:4 * blk, :] = jnp.where(lo_half, 0.0, xs).astype(BF16)
            dst[2 * c + 1, 0:2 * blk, :] = jnp.where(lo_half, xs, 0.0).astype(BF16)
            dst[2 * c + 1, 2 * blk:4 * blk, :] = jnp.where(lo_half, 0.0, x).astype(BF16)

    qi = lax.broadcasted_iota(jnp.int32, (blk, 4 * blk), 0)
    kj = lax.broadcasted_iota(jnp.int32, (blk, 4 * blk), 1) % (2 * blk)
    rel = qi + blk - kj
    first_key = jnp.where(n == 0, blk, 0)
    mask = (rel >= 0) & (rel < WINDOW) & (kj >= first_key)
    lo_out = lane < HEAD_DIM

    for p in range(N_Q_HEADS // 2):
        g = (2 * p) // (N_Q_HEADS // N_KV_HEADS)
        cols = slice(p * LANES, (p + 1) * LANES)
        q2 = (norm_rope(q_ref[:, cols], qg) * (HEAD_DIM ** -0.5)).astype(BF16)
        s = lax.dot_general(q2, kb[g], (((1,), (1,)), ((), ())), preferred_element_type=F32)
        s = jnp.where(mask, s, -jnp.inf)
        es, invs = [], []
        for h in range(2):
            sh = s[:, h * 2 * blk:(h + 1) * 2 * blk]
            sink = sink_ref[2 * p + h]
            m = jnp.maximum(jnp.max(sh, axis=-1, keepdims=True), sink)
            e = jnp.exp(sh - m)
            den = jnp.sum(e, axis=-1, keepdims=True) + jnp.exp(sink - m)
            es.append(e)
            invs.append(1.0 / den)
        pcat = jnp.concatenate(es, axis=1).astype(BF16)
        o2 = jnp.dot(pcat, vb[g], preferred_element_type=F32)
        inv = jnp.where(lo_out, invs[0], invs[1])
        o_ref[:, cols] = (o2 * inv).astype(o_ref.dtype)

    kwin[0:blk, :] = kwin[blk:2 * blk, :]
    vwin[0:blk, :] = vwin[blk:2 * blk, :]


def _attention(qkv, sinks, cos_t, sin_t, qg, kg, batch, seq):
    t = qkv.shape[0]
    blk = ATTN_BLOCK
    nb = seq // blk
    kcol = Q_WIDTH // KV_WIDTH
    return pl.pallas_call(
        _attn_kernel,
        out_shape=jax.ShapeDtypeStruct((t, Q_WIDTH), BF16),
        grid=(batch, nb),
        in_specs=[pl.BlockSpec(memory_space=pltpu.SMEM),
                  pl.BlockSpec((blk, Q_WIDTH), lambda b, n: (b * nb + n, 0)),
                  pl.BlockSpec((blk, KV_WIDTH), lambda b, n: (b * nb + n, kcol)),
                  pl.BlockSpec((blk, KV_WIDTH), lambda b, n: (b * nb + n, kcol + 1)),
                  pl.BlockSpec((blk, LANES), lambda b, n: (n, 0)),
                  pl.BlockSpec((blk, LANES), lambda b, n: (n, 0)),
                  pl.BlockSpec((1, LANES), lambda b, n: (0, 0)),
                  pl.BlockSpec((1, LANES), lambda b, n: (0, 0))],
        out_specs=pl.BlockSpec((blk, Q_WIDTH), lambda b, n: (b * nb + n, 0)),
        scratch_shapes=[pltpu.VMEM((2 * blk, KV_WIDTH), F32),
                        pltpu.VMEM((2 * blk, KV_WIDTH), F32),
                        pltpu.VMEM((N_KV_HEADS, 4 * blk, LANES), BF16),
                        pltpu.VMEM((N_KV_HEADS, 4 * blk, LANES), BF16)],
        compiler_params=_cparams(("arbitrary", "arbitrary")),
        name="swa_attention",
    )(sinks, qkv, qkv, qkv, cos_t, sin_t, qg, kg)


CONV_TS = 256
CONV_HALO = 32
CONV_RB = 128


def _conv_kernel(z_ref, w_ref, b_ref, lg_ref, lb_ref, o_ref, zbuf, cbuf):
    n = pl.program_id(1)

    @pl.when(n == 0)
    def _():
        zbuf[0:CONV_HALO, :] = jnp.zeros((CONV_HALO, CONV_CH), F32)

    zbuf[CONV_HALO:CONV_HALO + CONV_TS, :] = z_ref[...]
    shift = CONV_HALO - (CONV_WIDTH - 1)

    def chunk(c, carry):
        c0 = pl.multiple_of(c * LANES, LANES)
        for rb in range(CONV_TS // CONV_RB):
            acc = jnp.broadcast_to(b_ref[:, pl.ds(c0, LANES)], (CONV_RB, LANES))
            for j in range(CONV_WIDTH):
                r0 = rb * CONV_RB + shift + j
                acc = acc + zbuf[r0:r0 + CONV_RB, pl.ds(c0, LANES)] * w_ref[j:j + 1, pl.ds(c0, LANES)]
            cbuf[rb * CONV_RB:(rb + 1) * CONV_RB, pl.ds(c0, LANES)] = acc
        return carry

    lax.fori_loop(0, CONV_CH // LANES, chunk, 0)

    zbuf[0:CONV_HALO, :] = zbuf[CONV_TS:CONV_TS + CONV_HALO, :]

    y = cbuf[...]
    mu = jnp.mean(y, axis=-1, keepdims=True)
    yc = y - mu
    var = jnp.mean(yc * yc, axis=-1, keepdims=True)
    yn = yc * lax.rsqrt(var + NORM_EPS) * lg_ref[...] + lb_ref[...]
    o_ref[...] = (yn * jax.nn.sigmoid(yn)).astype(o_ref.dtype)


def _conformer_conv(z, dw_w, dw_b, ln_g, ln_b, batch, seq):
    t = z.shape[0]
    ns = seq // CONV_TS
    vec = lambda a: a.reshape(1, CONV_CH)
    return pl.pallas_call(
        _conv_kernel,
        out_shape=jax.ShapeDtypeStruct((t, CONV_CH), BF16),
        grid=(batch, ns),
        in_specs=[pl.BlockSpec((CONV_TS, CONV_CH), lambda b, n: (b * ns + n, 0)),
                  pl.BlockSpec((CONV_WIDTH, CONV_CH), lambda b, n: (0, 0)),
                  pl.BlockSpec((1, CONV_CH), lambda b, n: (0, 0)),
                  pl.BlockSpec((1, CONV_CH), lambda b, n: (0, 0)),
                  pl.BlockSpec((1, CONV_CH), lambda b, n: (0, 0))],
        out_specs=pl.BlockSpec((CONV_TS, CONV_CH), lambda b, n: (b * ns + n, 0)),
        scratch_shapes=[pltpu.VMEM((CONV_HALO + CONV_TS, CONV_CH), F32),
                        pltpu.VMEM((CONV_TS, CONV_CH), F32)],
        compiler_params=_cparams(("arbitrary", "arbitrary")),
        name="conformer_conv",
    )(z, dw_w, vec(dw_b), vec(ln_g), vec(ln_b))


ROUTER_TM = 256
HI16 = 0xFFFF0000


def _pack_halves(xb):
    n = xb.shape[1] // 2
    lo = lax.bitcast_convert_type(xb[:, :n].astype(F32), jnp.uint32)
    hi = lax.bitcast_convert_type(xb[:, n:].astype(F32), jnp.uint32)
    return (lo >> 16) | (hi & jnp.uint32(HI16))


def _unpack_halves(w):
    lo = lax.bitcast_convert_type(w << 16, F32).astype(BF16)
    hi = lax.bitcast_convert_type(w & jnp.uint32(HI16), F32).astype(BF16)
    return lo, hi


def _router_kernel(x_ref, g_ref, wr_ref, br_ref, h_ref, idx_ref, gate_ref):
    x = x_ref[...]
    ms = jnp.mean(x * x, axis=-1, keepdims=True)
    hb = (x * lax.rsqrt(ms + NORM_EPS) * g_ref[...]).astype(BF16)
    h_ref[...] = _pack_halves(hb)
    vals = jnp.dot(hb, wr_ref[...], preferred_element_type=F32) + br_ref[...]
    lane = lax.broadcasted_iota(jnp.int32, vals.shape, 1).astype(F32)
    tops, idxs = [], []
    for _ in range(TOP_K):
        m = jnp.max(vals, axis=-1, keepdims=True)
        idx = jnp.min(jnp.where(vals == m, lane, float(LANES)), axis=-1, keepdims=True)
        tops.append(m)
        idxs.append(idx)
        vals = jnp.where(lane == idx, -jnp.inf, vals)
    es = [jnp.exp(v - tops[0]) for v in tops]
    den = es[0] + es[1] + es[2] + es[3]
    idx_out = jnp.zeros(vals.shape, F32)
    gate_out = jnp.zeros(vals.shape, F32)
    for k in range(TOP_K):
        idx_out = jnp.where(lane == float(k), idxs[k], idx_out)
        gate_out = jnp.where(lane == float(k), es[k] / den, gate_out)
    idx_ref[...] = idx_out.astype(jnp.int32)
    gate_ref[...] = gate_out


def _router(x1, g, w_router, b_router):
    t, d = x1.shape
    tm = ROUTER_TM
    wr = jnp.zeros((d, LANES), BF16).at[:, :N_EXPERTS].set(w_router.astype(BF16))
    br = jnp.full((1, LANES), -1e30, F32).at[0, :N_EXPERTS].set(b_router)
    return pl.pallas_call(
        _router_kernel,
        out_shape=(jax.ShapeDtypeStruct((t, d // 2), jnp.uint32),
                   jax.ShapeDtypeStruct((t, LANES), jnp.int32),
                   jax.ShapeDtypeStruct((t, LANES), F32)),
        grid=(t // tm,),
        in_specs=[pl.BlockSpec((tm, d), lambda i: (i, 0)),
                  pl.BlockSpec((1, d), lambda i: (0, 0)),
                  pl.BlockSpec((d, LANES), lambda i: (0, 0)),
                  pl.BlockSpec((1, LANES), lambda i: (0, 0))],
        out_specs=(pl.BlockSpec((tm, d // 2), lambda i: (i, 0)),
                   pl.BlockSpec((tm, LANES), lambda i: (i, 0)),
                   pl.BlockSpec((tm, LANES), lambda i: (i, 0))),
        compiler_params=_cparams(("parallel",)),
        name="router",
    )(x1, g.reshape(1, d), wr, br)


GATHER_ROWS = 256


def _gather_kernel(idx_ref, nxt_ref, src_hbm, o_ref, buf, sem):
    i = pl.program_id(0)
    nsteps = pl.num_programs(0)
    slot = lax.rem(i, 2)

    def issue(ids, s):
        def body(r, carry):
            pltpu.make_async_copy(src_hbm.at[pl.ds(ids[0, 0, r], 1), :],
                                  buf.at[s, pl.ds(r, 1), :], sem.at[s]).start()
            return carry
        lax.fori_loop(0, GATHER_ROWS, body, 0)

    @pl.when(i == 0)
    def _():
        issue(idx_ref, 0)

    @pl.when(i + 1 < nsteps)
    def _():
        issue(nxt_ref, 1 - slot)

    pltpu.make_async_copy(src_hbm.at[pl.ds(0, GATHER_ROWS), :], buf.at[slot], sem.at[slot]).wait()
    half = buf.shape[2]
    lo, hi = _unpack_halves(buf[slot])
    o_ref[:, 0:half] = lo
    o_ref[:, half:2 * half] = hi


def _gather_rows(src, row_idx):
    n = row_idx.shape[0]
    d = src.shape[1]
    nsteps = n // GATHER_ROWS
    idx3 = row_idx.reshape(nsteps, 1, GATHER_ROWS)
    return pl.pallas_call(
        _gather_kernel,
        out_shape=jax.ShapeDtypeStruct((n, 2 * d), BF16),
        grid=(nsteps,),
        in_specs=[pl.BlockSpec((1, 1, GATHER_ROWS), lambda i: (i, 0, 0), memory_space=pltpu.SMEM),
                  pl.BlockSpec((1, 1, GATHER_ROWS), lambda i: (jnp.minimum(i + 1, nsteps - 1), 0, 0),
                               memory_space=pltpu.SMEM),
                  pl.BlockSpec(memory_space=pl.ANY)],
        out_specs=pl.BlockSpec((GATHER_ROWS, 2 * d), lambda i: (i, 0)),
        scratch_shapes=[pltpu.VMEM((2, GATHER_ROWS, d), src.dtype),
                        pltpu.SemaphoreType.DMA((2,))],
        compiler_params=_cparams(("arbitrary",)),
        name="dispatch_gather",
    )(idx3, idx3, src)


MOE_NBT = 10
MOE_TM = MOE_NBT * MOE_BLOCK
MOE_FAST = (7, 8, 9)
MOE_PIECE = 4
MOE_FC = 256
MOE_NC = 512
MOE_S1 = D_FF // MOE_FC
MOE_S2 = D_MODEL // MOE_NC


def _moe_max_blocks(t):
    return t * TOP_K // MOE_BLOCK + N_EXPERTS


def _moe_max_tiles(t):
    return (_moe_max_blocks(t) + N_EXPERTS * (MOE_NBT - 1)) // MOE_NBT + 1


def _moe_kernel(te_ref, tb_ref, tn_ref, nt_ref,
                xs_hbm, wg_ref, wl_ref, bg_ref, bl_ref, wd_ref, bd_ref, ys_hbm,
                xbuf, act, obuf, xsem, osem):
    i = pl.program_id(0)
    s = pl.program_id(1)
    ntiles = nt_ref[0]

    def xs_copy(tile, b):
        row0 = pl.multiple_of((tb_ref[tile] + b) * MOE_BLOCK, MOE_BLOCK)
        return pltpu.make_async_copy(xs_hbm.at[pl.ds(row0, MOE_BLOCK), :],
                                     xbuf.at[pl.ds(b * MOE_BLOCK, MOE_BLOCK), :], xsem.at[0])

    def for_row_pieces(nb, emit):
        general = nb >= 0
        for f in MOE_FAST:
            general = general & (nb != f)

            @pl.when(nb == f)
            def _():
                emit(0, f * MOE_BLOCK)

        @pl.when(general)
        def _():
            nbig = nb // MOE_PIECE
            big = MOE_PIECE * MOE_BLOCK

            def big_piece(q, carry):
                emit(pl.multiple_of(q * big, big), big)
                return carry

            def small_piece(b, carry):
                emit(pl.multiple_of(b * MOE_BLOCK, MOE_BLOCK), MOE_BLOCK)
                return carry

            lax.fori_loop(0, nbig, big_piece, 0)
            lax.fori_loop(nbig * MOE_PIECE, nb, small_piece, 0)

    def out_copy(slot, b, n):
        row0 = pl.multiple_of((tb_ref[i] + b) * MOE_BLOCK, MOE_BLOCK)
        col0 = pl.multiple_of(n * MOE_NC, MOE_NC)
        return pltpu.make_async_copy(obuf.at[slot, pl.ds(b * MOE_BLOCK, MOE_BLOCK), :],
                                     ys_hbm.at[pl.ds(row0, MOE_BLOCK), pl.ds(col0, MOE_NC)],
                                     osem.at[slot])

    @pl.when(i < ntiles)
    def _():
        nb = tn_ref[i]

        @pl.when((i == 0) & (s == 0))
        def _():
            for b in range(MOE_NBT):
                @pl.when(b < nb)
                def _():
                    xs_copy(i, b).start()

        @pl.when(s == 0)
        def _():
            for b in range(MOE_NBT):
                @pl.when(b < nb)
                def _():
                    xs_copy(i, b).wait()

        @pl.when(s < MOE_S1)
        def _():
            c0 = pl.multiple_of(s * MOE_FC, MOE_FC)

            def gate_up(row0, rows):
                x = xbuf[pl.ds(row0, rows), :]
                glu = jnp.dot(x, wg_ref[...].astype(BF16), preferred_element_type=F32) + bg_ref[...]
                lin = jnp.dot(x, wl_ref[...].astype(BF16), preferred_element_type=F32) + bl_ref[...]
                glu = jnp.minimum(glu, SWIGLU_LIMIT)
                lin = jnp.clip(lin, -SWIGLU_LIMIT, SWIGLU_LIMIT)
                a = glu * jax.nn.sigmoid(SWIGLU_ALPHA * glu) * (lin + 1.0)
                act[pl.ds(row0, rows), pl.ds(c0, MOE_FC)] = a.astype(BF16)

            for_row_pieces(nb, gate_up)

        @pl.when(s >= MOE_S1)
        def _():
            n = s - MOE_S1
            slot = lax.rem(n, 2)

            @pl.when((s == MOE_S1) & (i + 1 < ntiles))
            def _():
                nb_next = tn_ref[i + 1]
                for b in range(MOE_NBT):
                    @pl.when(b < nb_next)
                    def _():
                        xs_copy(i + 1, b).start()

            def down(row0, rows):
                y = jnp.dot(act[pl.ds(row0, rows), :], wd_ref[...].astype(BF16),
                            preferred_element_type=F32) + bd_ref[...]
                obuf[slot, pl.ds(row0, rows), :] = y

            for_row_pieces(nb, down)
            for b in range(MOE_NBT):
                @pl.when(b < nb)
                def _():
                    out_copy(slot, b, n).start()

            @pl.when(n > 0)
            def _():
                for b in range(MOE_NBT):
                    @pl.when(b < nb)
                    def _():
                        out_copy(1 - slot, b, n - 1).wait()

            @pl.when(n == MOE_S2 - 1)
            def _():
                for b in range(MOE_NBT):
                    @pl.when(b < nb)
                    def _():
                        out_copy(slot, b, n).wait()

    @pl.when(i == ntiles)
    def _():
        @pl.when(s == 0)
        def _():
            obuf[0, 0:MOE_BLOCK, :] = jnp.zeros((MOE_BLOCK, MOE_NC), F32)

        first_tail = tb_ref[ntiles - 1] + tn_ref[ntiles - 1]
        per_step = -(-N_EXPERTS // (MOE_S1 + MOE_S2))
        for u in range(per_step):
            blk = first_tail + s * per_step + u

            @pl.when(blk < ys_hbm.shape[0] // MOE_BLOCK)
            def _():
                row0 = pl.multiple_of(blk * MOE_BLOCK, MOE_BLOCK)
                fills = [pltpu.make_async_copy(
                    obuf.at[0, pl.ds(0, MOE_BLOCK), :],
                    ys_hbm.at[pl.ds(row0, MOE_BLOCK), pl.ds(c * MOE_NC, MOE_NC)], osem.at[0])
                    for c in range(MOE_S2)]
                for f in fills:
                    f.start()
                for f in fills:
                    f.wait()


def _moe_experts(xs, n_rows, tile_e, tile_b0, tile_nb, n_tiles, w_gate_up, b_gate_up, w_down, b_down):

    def live(i, nt):
        return i < nt[0]

    def f_idx(i, s, nt):
        return jnp.where(live(i, nt), jnp.minimum(s, MOE_S1 - 1), MOE_S1 - 1)

    def n_idx(i, s, nt):
        return jnp.where(live(i, nt), jnp.maximum(s - MOE_S1, 0), MOE_S2 - 1)

    lin_off = D_FF // MOE_FC
    grid_spec = pltpu.PrefetchScalarGridSpec(
        num_scalar_prefetch=4,
        grid=(tile_e.shape[0], MOE_S1 + MOE_S2),
        in_specs=[
            pl.BlockSpec(memory_space=pl.ANY),
            pl.BlockSpec((None, D_MODEL, MOE_FC), lambda i, s, te, tb, tn, nt: (te[i], 0, f_idx(i, s, nt))),
            pl.BlockSpec((None, D_MODEL, MOE_FC),
                         lambda i, s, te, tb, tn, nt: (te[i], 0, lin_off + f_idx(i, s, nt))),
            pl.BlockSpec((None, 1, MOE_FC), lambda i, s, te, tb, tn, nt: (te[i], 0, f_idx(i, s, nt))),
            pl.BlockSpec((None, 1, MOE_FC),
                         lambda i, s, te, tb, tn, nt: (te[i], 0, lin_off + f_idx(i, s, nt))),
            pl.BlockSpec((None, D_FF, MOE_NC), lambda i, s, te, tb, tn, nt: (te[i], 0, n_idx(i, s, nt))),
            pl.BlockSpec((None, 1, MOE_NC), lambda i, s, te, tb, tn, nt: (te[i], 0, n_idx(i, s, nt))),
        ],
        out_specs=pl.BlockSpec(memory_space=pl.ANY),
        scratch_shapes=[pltpu.VMEM((MOE_TM, D_MODEL), BF16),
                        pltpu.VMEM((MOE_TM, D_FF), BF16),
                        pltpu.VMEM((2, MOE_TM, MOE_NC), F32),
                        pltpu.SemaphoreType.DMA((1,)),
                        pltpu.SemaphoreType.DMA((2,))],
    )
    return pl.pallas_call(
        _moe_kernel,
        out_shape=jax.ShapeDtypeStruct((n_rows, D_MODEL), F32),
        grid_spec=grid_spec,
        compiler_params=_cparams(("arbitrary", "arbitrary")),
        name="moe_experts",
    )(tile_e, tile_b0, tile_nb, n_tiles, xs, w_gate_up, w_gate_up,
      b_gate_up.reshape(N_EXPERTS, 1, 2 * D_FF), b_gate_up.reshape(N_EXPERTS, 1, 2 * D_FF),
      w_down, b_down.reshape(N_EXPERTS, 1, D_MODEL))


COMBINE_TT = 128


def _combine_kernel(pos_ref, nxt_ref, x_ref, gate_ref, ys_hbm, o_ref, buf, sem):
    i = pl.program_id(0)
    nsteps = pl.num_programs(0)
    slot = lax.rem(i, 2)

    def issue(ids, s):
        for k in range(TOP_K):
            def body(r, carry):
                pltpu.make_async_copy(ys_hbm.at[pl.ds(ids[0, 0, r * TOP_K + k], 1), :],
                                      buf.at[s, k, pl.ds(r, 1), :], sem.at[s]).start()
                return carry
            lax.fori_loop(0, COMBINE_TT, body, 0)

    @pl.when(i == 0)
    def _():
        issue(pos_ref, 0)

    @pl.when(i + 1 < nsteps)
    def _():
        issue(nxt_ref, 1 - slot)

    for k in range(TOP_K):
        pltpu.make_async_copy(ys_hbm.at[pl.ds(0, COMBINE_TT), :], buf.at[slot, k], sem.at[slot]).wait()
    gate = gate_ref[...]
    acc = x_ref[...]
    moe = gate[:, 0:1] * buf[slot, 0]
    for k in range(1, TOP_K):
        moe = moe + gate[:, k:k + 1] * buf[slot, k]
    o_ref[...] = acc + moe


def _combine(x1, gates, ys, pos):
    t, d = x1.shape
    tt = COMBINE_TT
    nsteps = t // tt
    pos3 = pos.reshape(nsteps, 1, tt * TOP_K)
    return pl.pallas_call(
        _combine_kernel,
        out_shape=jax.ShapeDtypeStruct((t, d), F32),
        grid=(nsteps,),
        in_specs=[pl.BlockSpec((1, 1, tt * TOP_K), lambda i: (i, 0, 0), memory_space=pltpu.SMEM),
                  pl.BlockSpec((1, 1, tt * TOP_K), lambda i: (jnp.minimum(i + 1, nsteps - 1), 0, 0),
                               memory_space=pltpu.SMEM),
                  pl.BlockSpec((tt, d), lambda i: (i, 0)),
                  pl.BlockSpec((tt, LANES), lambda i: (i, 0)),
                  pl.BlockSpec(memory_space=pl.ANY)],
        out_specs=pl.BlockSpec((tt, d), lambda i: (i, 0)),
        scratch_shapes=[pltpu.VMEM((2, TOP_K, tt, d), F32),
                        pltpu.SemaphoreType.DMA((2,))],
        compiler_params=_cparams(("arbitrary",)),
        name="combine",
    )(pos3, pos3, x1, gates, ys)


def _routing(top_idx):
    t = top_idx.shape[0]
    n_assign = t * TOP_K
    flat_e = top_idx.reshape(-1)
    experts = jnp.arange(N_EXPERTS, dtype=jnp.int32)
    order = jnp.argsort(flat_e)
    sorted_pos = jnp.argsort(order).astype(jnp.int32)
    counts = jnp.sum((flat_e[:, None] == experts[None, :]).astype(jnp.int32), axis=0)
    starts = jnp.cumsum(counts) - counts
    padded = (counts + MOE_BLOCK - 1) // MOE_BLOCK * MOE_BLOCK
    pad_ends = jnp.cumsum(padded)
    pad_starts = pad_ends - padded
    shift = pad_starts - starts
    pos = sorted_pos + shift[flat_e]
    n_rows = n_assign + N_EXPERTS * MOE_BLOCK
    n_rows = (n_rows + GATHER_ROWS - 1) // GATHER_ROWS * GATHER_ROWS
    rows = jnp.arange(n_rows, dtype=jnp.int32)
    row_e = jnp.minimum(jnp.sum((rows[:, None] >= pad_ends[None, :]).astype(jnp.int32), axis=1),
                        N_EXPERTS - 1)
    place = rows - shift[row_e]
    valid = (rows - pad_starts[row_e]) < counts[row_e]
    row_tok = jnp.where(valid, order[jnp.clip(place, 0, n_assign - 1)].astype(jnp.int32) // TOP_K, 0)

    nblk = padded // MOE_BLOCK
    blk0 = pad_starts // MOE_BLOCK
    ntile = (nblk + MOE_NBT - 1) // MOE_NBT
    tile_end = jnp.cumsum(ntile)
    n_tiles = tile_end[-1]
    ids = jnp.arange(_moe_max_tiles(t), dtype=jnp.int32)
    ids_c = jnp.minimum(ids, n_tiles - 1)
    tile_e = jnp.minimum(jnp.searchsorted(tile_end, ids_c, side='right'), N_EXPERTS - 1).astype(jnp.int32)
    local = ids_c - (tile_end - ntile)[tile_e]
    tile_b0 = (blk0[tile_e] + local * MOE_NBT).astype(jnp.int32)
    tile_nb = jnp.where(ids < n_tiles, jnp.clip(nblk[tile_e] - local * MOE_NBT, 0, MOE_NBT), 0).astype(jnp.int32)
    return row_tok, pos, tile_e, tile_b0, tile_nb, n_tiles.reshape(1).astype(jnp.int32)


def _rope_tables(seq):
    inv_freq = 1.0 / (ROPE_THETA ** (jnp.arange(0, HEAD_DIM, 2, dtype=F32) / HEAD_DIM))
    ang = jnp.arange(seq, dtype=F32)[:, None] * inv_freq[None, :]
    cos = jnp.cos(ang)
    sin = jnp.sin(ang)
    reps = LANES // HEAD_DIM
    cos_t = jnp.tile(jnp.concatenate([cos, cos], axis=-1), (1, reps))
    sin_t = jnp.tile(jnp.concatenate([-sin, sin], axis=-1), (1, reps))
    return cos_t, sin_t


def kernel(x, norm1_g, w_in, b_in, q_norm_g, k_norm_g, attn_sinks, w_attn_o, conv_dw_w, conv_dw_b,
           conv_ln_g, conv_ln_b, w_conv_o, w_out, norm2_g, w_router, b_router, w_gate_up, b_gate_up,
           w_down, b_down):
    b, s, d = x.shape
    t = b * s
    depth = norm1_g.shape[0]
    xt = x.reshape(t, d)
    cos_t, sin_t = _rope_tables(s)
    reps = LANES // HEAD_DIM
    c0 = QKV_WIDTH
    c1 = c0 + CONV_CH
    c2 = c1 + CONV_CH
    for l in range(depth):
        h = _rmsnorm(xt, norm1_g[l], BF16)
        wl, bl = w_in[l], b_in[l]
        qkv = _mm_bias(h, wl[:, :c0].astype(BF16), bl[:c0], F32, tm=1024, tn=512, name="inproj_qkv")
        z = _mm_glu(h, wl[:, c0:c1].astype(BF16), wl[:, c1:c2].astype(BF16), bl[c0:c1], bl[c1:c2],
                    tm=1024, tn=512)
        gates = _mm_bias(h, wl[:, c2:].astype(BF16), bl[c2:], BF16, tm=1024, tn=1024, sigmoid=True,
                         name="inproj_gates")
        attn = _attention(qkv, attn_sinks[l], cos_t, sin_t,
                          jnp.tile(q_norm_g[l], reps).reshape(1, LANES),
                          jnp.tile(k_norm_g[l], reps).reshape(1, LANES), b, s)
        conv = _conformer_conv(z, conv_dw_w[l], conv_dw_b[l], conv_ln_g[l], conv_ln_b[l], b, s)
        merged = _merge(attn, conv, w_attn_o[l].astype(BF16), w_conv_o[l].astype(BF16), gates,
                        tm=1024, tn=512)
        x1 = _mm_residual(merged, w_out[l].astype(BF16), xt, tm=1024, tn=1024)
        h2, idx_pad, gate_pad = _router(x1, norm2_g[l], w_router[l], b_router[l])
        row_tok, pos, tile_e, tile_b0, tile_nb, n_tiles = _routing(idx_pad[:, :TOP_K])
        xs = _gather_rows(h2, row_tok)
        ys = _moe_experts(xs, _moe_max_blocks(t) * MOE_BLOCK, tile_e, tile_b0, tile_nb, n_tiles,
                          w_gate_up[l], b_gate_up[l], w_down[l], b_down[l])
        xt = _combine(x1, gate_pad, ys, pos)
    return xt.reshape(b, s, d)
```

```python
import functools

import jax
import jax.numpy as jnp
from jax import lax
from jax.experimental import pallas as pl
from jax.experimental.pallas import tpu as pltpu

D_MODEL = 4096
HEAD_DIM = 64
N_Q_HEADS = 32
N_KV_HEADS = 4
WINDOW = 128
ATTN_BLOCK = 128
ROPE_THETA = 10000.0
CONV_CH = 2048
CONV_WIDTH = 31
N_EXPERTS = 32
TOP_K = 4
D_FF = 2048
SWIGLU_ALPHA = 1.702
SWIGLU_LIMIT = 7.0
MOE_BLOCK = 128
NORM_EPS = 1e-5

Q_WIDTH = N_Q_HEADS * HEAD_DIM
KV_WIDTH = N_KV_HEADS * HEAD_DIM
QKV_WIDTH = Q_WIDTH + 2 * KV_WIDTH

LANES = 128
SUBLANES = 8
VMEM_LIMIT = 56 * 1024 * 1024

BF16 = jnp.bfloat16
F32 = jnp.float32


def _cparams(sem, vmem=VMEM_LIMIT):
    return pltpu.CompilerParams(dimension_semantics=sem, vmem_limit_bytes=vmem)


def _rmsnorm_kernel(x_ref, g_ref, o_ref):
    x = x_ref[...]
    ms = jnp.mean(x * x, axis=-1, keepdims=True)
    o_ref[...] = (x * lax.rsqrt(ms + NORM_EPS) * g_ref[...]).astype(o_ref.dtype)


def _rmsnorm(x, g, out_dtype, tm=512):
    t, d = x.shape
    return pl.pallas_call(
        _rmsnorm_kernel,
        out_shape=jax.ShapeDtypeStruct((t, d), out_dtype),
        grid=(t // tm,),
        in_specs=[pl.BlockSpec((tm, d), lambda i: (i, 0)),
                  pl.BlockSpec((1, d), lambda i: (0, 0))],
        out_specs=pl.BlockSpec((tm, d), lambda i: (i, 0)),
        compiler_params=_cparams(("parallel",)),
        name="rmsnorm",
    )(x, g.reshape(1, d))


def _wspec(k, tn, layer, off=0):
    return pl.BlockSpec((None, k, tn), lambda j, i: (layer, 0, j + off))


def _cast_on_first_row_block(pairs):
    @pl.when(pl.program_id(1) == 0)
    def _():
        for w_ref, wb_ref in pairs:
            wb_ref[...] = w_ref[...].astype(BF16)


def _mm_bias_kernel(a_ref, w_ref, b_ref, o_ref, wb, *, sigmoid):
    _cast_on_first_row_block([(w_ref, wb)])
    acc = jnp.dot(a_ref[...], wb[...], preferred_element_type=F32) + b_ref[...]
    if sigmoid:
        acc = jax.nn.sigmoid(acc)
    o_ref[...] = acc.astype(o_ref.dtype)


def _mm_bias(a, w, b, layer, col0, n, out_dtype, *, tm, tn, sigmoid=False, name):
    m, k = a.shape
    off = col0 // tn
    return pl.pallas_call(
        functools.partial(_mm_bias_kernel, sigmoid=sigmoid),
        out_shape=jax.ShapeDtypeStruct((m, n), out_dtype),
        grid=(n // tn, m // tm),
        in_specs=[pl.BlockSpec((tm, k), lambda j, i: (i, 0)),
                  _wspec(k, tn, layer, off),
                  _wspec(1, tn, layer, off)],
        out_specs=pl.BlockSpec((tm, tn), lambda j, i: (i, j)),
        scratch_shapes=[pltpu.VMEM((k, tn), BF16)],
        compiler_params=_cparams(("arbitrary", "arbitrary")),
        name=name,
    )(a, w, b.reshape(b.shape[0], 1, b.shape[1]))


def _mm_glu_kernel(a_ref, wa_ref, wg_ref, ba_ref, bg_ref, o_ref, wab, wgb):
    _cast_on_first_row_block([(wa_ref, wab), (wg_ref, wgb)])
    a = a_ref[...]
    u = jnp.dot(a, wab[...], preferred_element_type=F32) + ba_ref[...]
    g = jnp.dot(a, wgb[...], preferred_element_type=F32) + bg_ref[...]
    o_ref[...] = (u * jax.nn.sigmoid(g)).astype(o_ref.dtype)


def _mm_glu(a, w, b, layer, col0, n, *, tm, tn):
    m, k = a.shape
    off_u = col0 // tn
    off_g = (col0 + n) // tn
    b3 = b.reshape(b.shape[0], 1, b.shape[1])
    return pl.pallas_call(
        _mm_glu_kernel,
        out_shape=jax.ShapeDtypeStruct((m, n), F32),
        grid=(n // tn, m // tm),
        in_specs=[pl.BlockSpec((tm, k), lambda j, i: (i, 0)),
                  _wspec(k, tn, layer, off_u), _wspec(k, tn, layer, off_g),
                  _wspec(1, tn, layer, off_u), _wspec(1, tn, layer, off_g)],
        out_specs=pl.BlockSpec((tm, tn), lambda j, i: (i, j)),
        scratch_shapes=[pltpu.VMEM((k, tn), BF16), pltpu.VMEM((k, tn), BF16)],
        compiler_params=_cparams(("arbitrary", "arbitrary")),
        name="inproj_glu",
    )(a, w, w, b3, b3)


def _merge_kernel(a_ref, c_ref, wa_ref, wc_ref, ga_ref, gc_ref, o_ref, wab, wcb):
    _cast_on_first_row_block([(wa_ref, wab), (wc_ref, wcb)])
    pa = jnp.dot(a_ref[...], wab[...], preferred_element_type=F32)
    pc = jnp.dot(c_ref[...], wcb[...], preferred_element_type=F32)
    o_ref[...] = (ga_ref[...].astype(F32) * pa + gc_ref[...].astype(F32) * pc).astype(o_ref.dtype)


def _merge(attn, conv, wa, wc, layer, gates, *, tm, tn):
    m, k = attn.shape
    n = wa.shape[2]
    nj = n // tn
    return pl.pallas_call(
        _merge_kernel,
        out_shape=jax.ShapeDtypeStruct((m, n), BF16),
        grid=(nj, m // tm),
        in_specs=[pl.BlockSpec((tm, k), lambda j, i: (i, 0)),
                  pl.BlockSpec((tm, k), lambda j, i: (i, 0)),
                  _wspec(k, tn, layer), _wspec(k, tn, layer),
                  pl.BlockSpec((tm, tn), lambda j, i: (i, j)),
                  pl.BlockSpec((tm, tn), lambda j, i: (i, j + nj))],
        out_specs=pl.BlockSpec((tm, tn), lambda j, i: (i, j)),
        scratch_shapes=[pltpu.VMEM((k, tn), BF16), pltpu.VMEM((k, tn), BF16)],
        compiler_params=_cparams(("arbitrary", "arbitrary")),
        name="merge",
    )(attn, conv, wa, wc, gates, gates)


def _mm_residual_kernel(a_ref, w_ref, x_ref, o_ref, wb):
    _cast_on_first_row_block([(w_ref, wb)])
    o_ref[...] = x_ref[...] + jnp.dot(a_ref[...], wb[...], preferred_element_type=F32)


def _mm_residual(a, w, layer, x, *, tm, tn):
    m, k = a.shape
    n = w.shape[2]
    return pl.pallas_call(
        _mm_residual_kernel,
        out_shape=jax.ShapeDtypeStruct((m, n), F32),
        grid=(n // tn, m // tm),
        in_specs=[pl.BlockSpec((tm, k), lambda j, i: (i, 0)),
                  _wspec(k, tn, layer),
                  pl.BlockSpec((tm, tn), lambda j, i: (i, j))],
        out_specs=pl.BlockSpec((tm, tn), lambda j, i: (i, j)),
        scratch_shapes=[pltpu.VMEM((k, tn), BF16)],
        compiler_params=_cparams(("arbitrary", "arbitrary")),
        name="outproj",
    )(a, w, x)


ATTN_RC = 32


def _attn_kernel(sink_ref, q_ref, k_ref, v_ref, cos_ref, sin_ref, qg_ref, kg_ref, o_ref,
                 kwin, vwin, kb, vb, qs, bias_scr, s_scr, p_scr, inv_scr):
    n = pl.program_id(1)
    blk = ATTN_BLOCK

    @pl.when(n == 0)
    def _():
        kwin[0:blk, :] = jnp.zeros((blk, KV_WIDTH), F32)
        vwin[0:blk, :] = jnp.zeros((blk, KV_WIDTH), F32)

    cos = cos_ref[...]
    sin = sin_ref[...]
    lane = lax.broadcasted_iota(jnp.int32, (blk, LANES), 1)
    first_half = (lane % HEAD_DIM) < (HEAD_DIM // 2)
    seg_r = lax.broadcasted_iota(jnp.int32, (LANES, LANES), 0) // HEAD_DIM
    seg_c = lax.broadcasted_iota(jnp.int32, (LANES, LANES), 1) // HEAD_DIM
    seg = jnp.where(seg_r == seg_c, 1.0, 0.0).astype(BF16)

    def norm_rope(t, g):
        t2 = t * t
        hi = t2.astype(BF16)
        lo = (t2 - hi.astype(F32)).astype(BF16)
        ss = (jnp.dot(hi, seg, preferred_element_type=F32)
              + jnp.dot(lo, seg, preferred_element_type=F32))
        tn = t * lax.rsqrt(ss * (1.0 / HEAD_DIM) + NORM_EPS) * g
        rot = jnp.where(first_half, pltpu.roll(tn, LANES - HEAD_DIM // 2, 1),
                        pltpu.roll(tn, HEAD_DIM // 2, 1))
        return tn * cos + rot * sin

    kg = kg_ref[...]
    qg = qg_ref[...]
    for c in range(KV_WIDTH // LANES):
        cols = slice(c * LANES, (c + 1) * LANES)
        kwin[blk:2 * blk, cols] = norm_rope(k_ref[:, cols], kg)
    vwin[blk:2 * blk, :] = v_ref[...]

    lane2 = lax.broadcasted_iota(jnp.int32, (2 * blk, LANES), 1)
    lo_half = lane2 < HEAD_DIM
    for c in range(KV_WIDTH // LANES):
        cols = slice(c * LANES, (c + 1) * LANES)
        for win, dst in ((kwin, kb), (vwin, vb)):
            x = win[:, cols]
            xs = pltpu.roll(x, HEAD_DIM, 1)
            dst[2 * c, 0:2 * blk, :] = jnp.where(lo_half, x, 0.0).astype(BF16)
            dst[2 * c, 2 * blk:4 * blk, :] = jnp.where(lo_half, 0.0, xs).astype(BF16)
            dst[2 * c + 1, 0:2 * blk, :] = jnp.where(lo_half, xs, 0.0).astype(BF16)
            dst[2 * c + 1, 2 * blk:4 * blk, :] = jnp.where(lo_half, 0.0, x).astype(BF16)

    qi = lax.broadcasted_iota(jnp.int32, (blk, 4 * blk), 0)
    kj = lax.broadcasted_iota(jnp.int32, (blk, 4 * blk), 1) % (2 * blk)
    rel = qi + blk - kj
    first_key = jnp.where(n == 0, blk, 0)
    mask = (rel >= 0) & (rel < WINDOW) & (kj >= first_key)
    bias_scr[...] = jnp.where(mask, 0.0, -jnp.inf)

    pairs_per_kv = N_Q_HEADS // N_KV_HEADS // 2
    for p in range(N_Q_HEADS // 2):
        g, pp = divmod(p, pairs_per_kv)
        q2 = norm_rope(q_ref[:, p * LANES:(p + 1) * LANES], qg) * (HEAD_DIM ** -0.5)
        qs[g, pp * blk:(pp + 1) * blk, :] = q2.astype(BF16)

    lo_out = lax.broadcasted_iota(jnp.int32, (ATTN_RC, LANES), 1) < HEAD_DIM
    for g in range(N_KV_HEADS):
        s_scr[g] = lax.dot_general(qs[g], kb[g], (((1,), (1,)), ((), ())), preferred_element_type=F32)
        for r in range(pairs_per_kv * blk // ATTN_RC):
            rows = slice(r * ATTN_RC, (r + 1) * ATTN_RC)
            pp, q0 = divmod(r * ATTN_RC, blk)
            s = s_scr[g, rows, :] + bias_scr[q0:q0 + ATTN_RC, :]
            invs = []
            for h in range(2):
                keys = slice(h * 2 * blk, (h + 1) * 2 * blk)
                sh = s[:, keys]
                sink = sink_ref[2 * (g * pairs_per_kv + pp) + h]
                m = jnp.maximum(jnp.max(sh, axis=-1, keepdims=True), sink)
                e = jnp.exp(sh - m)
                den = jnp.sum(e, axis=-1, keepdims=True) + jnp.exp(sink - m)
                p_scr[g, rows, keys] = e.astype(BF16)
                invs.append(1.0 / den)
            inv_scr[g, rows, :] = jnp.where(lo_out, invs[0], invs[1])
        o = jnp.dot(p_scr[g], vb[g], preferred_element_type=F32) * inv_scr[g]
        for pp in range(pairs_per_kv):
            p = g * pairs_per_kv + pp
            o_ref[:, p * LANES:(p + 1) * LANES] = o[pp * blk:(pp + 1) * blk, :].astype(o_ref.dtype)

    kwin[0:blk, :] = kwin[blk:2 * blk, :]
    vwin[0:blk, :] = vwin[blk:2 * blk, :]


def _attention(qkv, sinks, cos_t, sin_t, qg, kg, batch, seq):
    t = qkv.shape[0]
    blk = ATTN_BLOCK
    nb = seq // blk
    kcol = Q_WIDTH // KV_WIDTH
    return pl.pallas_call(
        _attn_kernel,
        out_shape=jax.ShapeDtypeStruct((t, Q_WIDTH), BF16),
        grid=(batch, nb),
        in_specs=[pl.BlockSpec(memory_space=pltpu.SMEM),
                  pl.BlockSpec((blk, Q_WIDTH), lambda b, n: (b * nb + n, 0)),
                  pl.BlockSpec((blk, KV_WIDTH), lambda b, n: (b * nb + n, kcol)),
                  pl.BlockSpec((blk, KV_WIDTH), lambda b, n: (b * nb + n, kcol + 1)),
                  pl.BlockSpec((blk, LANES), lambda b, n: (n, 0)),
                  pl.BlockSpec((blk, LANES), lambda b, n: (n, 0)),
                  pl.BlockSpec((1, LANES), lambda b, n: (0, 0)),
                  pl.BlockSpec((1, LANES), lambda b, n: (0, 0))],
        out_specs=pl.BlockSpec((blk, Q_WIDTH), lambda b, n: (b * nb + n, 0)),
        scratch_shapes=[pltpu.VMEM((2 * blk, KV_WIDTH), F32),
                        pltpu.VMEM((2 * blk, KV_WIDTH), F32),
                        pltpu.VMEM((N_KV_HEADS, 4 * blk, LANES), BF16),
                        pltpu.VMEM((N_KV_HEADS, 4 * blk, LANES), BF16),
                        pltpu.VMEM((N_KV_HEADS, 4 * blk, LANES), BF16),
                        pltpu.VMEM((blk, 4 * blk), F32),
                        pltpu.VMEM((N_KV_HEADS, 4 * blk, 4 * blk), F32),
                        pltpu.VMEM((N_KV_HEADS, 4 * blk, 4 * blk), BF16),
                        pltpu.VMEM((N_KV_HEADS, 4 * blk, LANES), F32)],
        compiler_params=_cparams(("arbitrary", "arbitrary")),
        name="swa_attention",
    )(sinks, qkv, qkv, qkv, cos_t, sin_t, qg, kg)


CONV_TS = 256
CONV_HALO = 32
CONV_RB = 128


def _conv_kernel(z_ref, w_ref, b_ref, lg_ref, lb_ref, o_ref, zbuf, cbuf):
    n = pl.program_id(1)

    @pl.when(n == 0)
    def _():
        zbuf[0:CONV_HALO, :] = jnp.zeros((CONV_HALO, CONV_CH), F32)

    zbuf[CONV_HALO:CONV_HALO + CONV_TS, :] = z_ref[...]
    shift = CONV_HALO - (CONV_WIDTH - 1)

    def chunk(c, carry):
        c0 = pl.multiple_of(c * LANES, LANES)
        for rb in range(CONV_TS // CONV_RB):
            base = rb * CONV_RB
            acc = jnp.broadcast_to(b_ref[:, pl.ds(c0, LANES)], (CONV_RB, LANES))
            for res in range(SUBLANES):
                extra = SUBLANES if res else 0
                part = None
                for off in range(shift, shift + CONV_WIDTH):
                    if off % SUBLANES != res:
                        continue
                    r0 = base + off - res
                    term = (zbuf[r0:r0 + CONV_RB + extra, pl.ds(c0, LANES)]
                            * w_ref[off - shift:off - shift + 1, pl.ds(c0, LANES)])
                    part = term if part is None else part + term
                acc = acc + part[res:res + CONV_RB]
            cbuf[base:base + CONV_RB, pl.ds(c0, LANES)] = acc
        return carry

    lax.fori_loop(0, CONV_CH // LANES, chunk, 0)

    zbuf[0:CONV_HALO, :] = zbuf[CONV_TS:CONV_TS + CONV_HALO, :]

    y = cbuf[...]
    mu = jnp.mean(y, axis=-1, keepdims=True)
    yc = y - mu
    var = jnp.mean(yc * yc, axis=-1, keepdims=True)
    yn = yc * lax.rsqrt(var + NORM_EPS) * lg_ref[...] + lb_ref[...]
    o_ref[...] = (yn * jax.nn.sigmoid(yn)).astype(o_ref.dtype)


def _conformer_conv(z, dw_w, dw_b, ln_g, ln_b, batch, seq):
    t = z.shape[0]
    ns = seq // CONV_TS
    vec = lambda a: a.reshape(1, CONV_CH)
    return pl.pallas_call(
        _conv_kernel,
        out_shape=jax.ShapeDtypeStruct((t, CONV_CH), BF16),
        grid=(batch, ns),
        in_specs=[pl.BlockSpec((CONV_TS, CONV_CH), lambda b, n: (b * ns + n, 0)),
                  pl.BlockSpec((CONV_WIDTH, CONV_CH), lambda b, n: (0, 0)),
                  pl.BlockSpec((1, CONV_CH), lambda b, n: (0, 0)),
                  pl.BlockSpec((1, CONV_CH), lambda b, n: (0, 0)),
                  pl.BlockSpec((1, CONV_CH), lambda b, n: (0, 0))],
        out_specs=pl.BlockSpec((CONV_TS, CONV_CH), lambda b, n: (b * ns + n, 0)),
        scratch_shapes=[pltpu.VMEM((CONV_HALO + CONV_TS, CONV_CH), F32),
                        pltpu.VMEM((CONV_TS, CONV_CH), F32)],
        compiler_params=_cparams(("arbitrary", "arbitrary")),
        name="conformer_conv",
    )(z, dw_w, vec(dw_b), vec(ln_g), vec(ln_b))


ROUTER_TM = 256
HI16 = 0xFFFF0000


def _pack_halves(xb):
    n = xb.shape[1] // 2
    lo = lax.bitcast_convert_type(xb[:, :n].astype(F32), jnp.uint32)
    hi = lax.bitcast_convert_type(xb[:, n:].astype(F32), jnp.uint32)
    return (lo >> 16) | (hi & jnp.uint32(HI16))


def _unpack_halves(w):
    lo = lax.bitcast_convert_type(w << 16, F32).astype(BF16)
    hi = lax.bitcast_convert_type(w & jnp.uint32(HI16), F32).astype(BF16)
    return lo, hi


def _router_kernel(x_ref, g_ref, wr_ref, br_ref, h_ref, idx_ref, gate_ref):
    x = x_ref[...]
    ms = jnp.mean(x * x, axis=-1, keepdims=True)
    hb = (x * lax.rsqrt(ms + NORM_EPS) * g_ref[...]).astype(BF16)
    h_ref[...] = _pack_halves(hb)
    vals = jnp.dot(hb, wr_ref[...], preferred_element_type=F32) + br_ref[...]
    lane = lax.broadcasted_iota(jnp.int32, vals.shape, 1).astype(F32)
    tops, idxs = [], []
    for _ in range(TOP_K):
        m = jnp.max(vals, axis=-1, keepdims=True)
        idx = jnp.min(jnp.where(vals == m, lane, float(LANES)), axis=-1, keepdims=True)
        tops.append(m)
        idxs.append(idx)
        vals = jnp.where(lane == idx, -jnp.inf, vals)
    es = [jnp.exp(v - tops[0]) for v in tops]
    den = es[0] + es[1] + es[2] + es[3]
    idx_out = jnp.zeros(vals.shape, F32)
    gate_out = jnp.zeros(vals.shape, F32)
    for k in range(TOP_K):
        idx_out = jnp.where(lane == float(k), idxs[k], idx_out)
        gate_out = jnp.where(lane == float(k), es[k] / den, gate_out)
    idx_ref[...] = idx_out.astype(jnp.int32)
    gate_ref[...] = gate_out


def _router(x1, g, w_router, b_router):
    t, d = x1.shape
    tm = ROUTER_TM
    wr = jnp.zeros((d, LANES), BF16).at[:, :N_EXPERTS].set(w_router.astype(BF16))
    br = jnp.full((1, LANES), -1e30, F32).at[0, :N_EXPERTS].set(b_router)
    return pl.pallas_call(
        _router_kernel,
        out_shape=(jax.ShapeDtypeStruct((t, d // 2), jnp.uint32),
                   jax.ShapeDtypeStruct((t, LANES), jnp.int32),
                   jax.ShapeDtypeStruct((t, LANES), F32)),
        grid=(t // tm,),
        in_specs=[pl.BlockSpec((tm, d), lambda i: (i, 0)),
                  pl.BlockSpec((1, d), lambda i: (0, 0)),
                  pl.BlockSpec((d, LANES), lambda i: (0, 0)),
                  pl.BlockSpec((1, LANES), lambda i: (0, 0))],
        out_specs=(pl.BlockSpec((tm, d // 2), lambda i: (i, 0)),
                   pl.BlockSpec((tm, LANES), lambda i: (i, 0)),
                   pl.BlockSpec((tm, LANES), lambda i: (i, 0))),
        compiler_params=_cparams(("parallel",)),
        name="router",
    )(x1, g.reshape(1, d), wr, br)


GATHER_ROWS = 256


def _gather_kernel(idx_ref, nxt_ref, src_hbm, o_ref, buf, sem):
    i = pl.program_id(0)
    nsteps = pl.num_programs(0)
    slot = lax.rem(i, 2)

    def issue(ids, s):
        def body(r, carry):
            pltpu.make_async_copy(src_hbm.at[pl.ds(ids[0, 0, r], 1), :],
                                  buf.at[s, pl.ds(r, 1), :], sem.at[s]).start()
            return carry
        lax.fori_loop(0, GATHER_ROWS, body, 0, unroll=8)

    @pl.when(i == 0)
    def _():
        issue(idx_ref, 0)

    @pl.when(i + 1 < nsteps)
    def _():
        issue(nxt_ref, 1 - slot)

    pltpu.make_async_copy(src_hbm.at[pl.ds(0, GATHER_ROWS), :], buf.at[slot], sem.at[slot]).wait()
    half = buf.shape[2]
    lo, hi = _unpack_halves(buf[slot])
    o_ref[:, 0:half] = lo
    o_ref[:, half:2 * half] = hi


def _gather_rows(src, row_idx):
    n = row_idx.shape[0]
    d = src.shape[1]
    nsteps = n // GATHER_ROWS
    idx3 = row_idx.reshape(nsteps, 1, GATHER_ROWS)
    return pl.pallas_call(
        _gather_kernel,
        out_shape=jax.ShapeDtypeStruct((n, 2 * d), BF16),
        grid=(nsteps,),
        in_specs=[pl.BlockSpec((1, 1, GATHER_ROWS), lambda i: (i, 0, 0), memory_space=pltpu.SMEM),
                  pl.BlockSpec((1, 1, GATHER_ROWS), lambda i: (jnp.minimum(i + 1, nsteps - 1), 0, 0),
                               memory_space=pltpu.SMEM),
                  pl.BlockSpec(memory_space=pl.ANY)],
        out_specs=pl.BlockSpec((GATHER_ROWS, 2 * d), lambda i: (i, 0)),
        scratch_shapes=[pltpu.VMEM((2, GATHER_ROWS, d), src.dtype),
                        pltpu.SemaphoreType.DMA((2,))],
        compiler_params=_cparams(("arbitrary",)),
        name="dispatch_gather",
    )(idx3, idx3, src)


MOE_NBT = 10
MOE_TM = MOE_NBT * MOE_BLOCK
MOE_FAST = (7, 8, 9)
MOE_PIECE = 4
MOE_FC = 256
MOE_NC = 512
MOE_S1 = D_FF // MOE_FC
MOE_S2 = D_MODEL // MOE_NC


def _moe_max_blocks(t):
    return t * TOP_K // MOE_BLOCK + N_EXPERTS


def _moe_max_tiles(t):
    return (_moe_max_blocks(t) + N_EXPERTS * (MOE_NBT - 1)) // MOE_NBT + 1


def _moe_kernel(te_ref, tb_ref, tn_ref, nt_ref,
                xs_hbm, wg_ref, wl_ref, bg_ref, bl_ref, wd_ref, bd_ref, ys_hbm,
                xbuf, act, obuf, xsem, osem):
    i = pl.program_id(0)
    s = pl.program_id(1)
    ntiles = nt_ref[0]

    def xs_copy(tile, b):
        row0 = pl.multiple_of((tb_ref[tile] + b) * MOE_BLOCK, MOE_BLOCK)
        return pltpu.make_async_copy(xs_hbm.at[pl.ds(row0, MOE_BLOCK), :],
                                     xbuf.at[pl.ds(b * MOE_BLOCK, MOE_BLOCK), :], xsem.at[0])

    def for_row_pieces(nb, emit):
        general = nb >= 0
        for f in MOE_FAST:
            general = general & (nb != f)

            @pl.when(nb == f)
            def _():
                emit(0, f * MOE_BLOCK)

        @pl.when(general)
        def _():
            nbig = nb // MOE_PIECE
            big = MOE_PIECE * MOE_BLOCK

            def big_piece(q, carry):
                emit(pl.multiple_of(q * big, big), big)
                return carry

            def small_piece(b, carry):
                emit(pl.multiple_of(b * MOE_BLOCK, MOE_BLOCK), MOE_BLOCK)
                return carry

            lax.fori_loop(0, nbig, big_piece, 0)
            lax.fori_loop(nbig * MOE_PIECE, nb, small_piece, 0)

    def out_copy(slot, b, n):
        row0 = pl.multiple_of((tb_ref[i] + b) * MOE_BLOCK, MOE_BLOCK)
        col0 = pl.multiple_of(n * MOE_NC, MOE_NC)
        return pltpu.make_async_copy(obuf.at[slot, pl.ds(b * MOE_BLOCK, MOE_BLOCK), :],
                                     ys_hbm.at[pl.ds(row0, MOE_BLOCK), pl.ds(col0, MOE_NC)],
                                     osem.at[slot])

    @pl.when(i < ntiles)
    def _():
        nb = tn_ref[i]

        @pl.when((i == 0) & (s == 0))
        def _():
            for b in range(MOE_NBT):
                @pl.when(b < nb)
                def _():
                    xs_copy(i, b).start()

        @pl.when(s == 0)
        def _():
            for b in range(MOE_NBT):
                @pl.when(b < nb)
                def _():
                    xs_copy(i, b).wait()

        @pl.when(s < MOE_S1)
        def _():
            c0 = pl.multiple_of(s * MOE_FC, MOE_FC)

            def gate_up(row0, rows):
                x = xbuf[pl.ds(row0, rows), :]
                glu = jnp.dot(x, wg_ref[...].astype(BF16), preferred_element_type=F32) + bg_ref[...]
                lin = jnp.dot(x, wl_ref[...].astype(BF16), preferred_element_type=F32) + bl_ref[...]
                glu = jnp.minimum(glu, SWIGLU_LIMIT)
                lin = jnp.clip(lin, -SWIGLU_LIMIT, SWIGLU_LIMIT)
                a = glu * jax.nn.sigmoid(SWIGLU_ALPHA * glu) * (lin + 1.0)
                act[pl.ds(row0, rows), pl.ds(c0, MOE_FC)] = a.astype(BF16)

            for_row_pieces(nb, gate_up)

        @pl.when(s >= MOE_S1)
        def _():
            n = s - MOE_S1
            slot = lax.rem(n, 2)

            @pl.when((s == MOE_S1) & (i + 1 < ntiles))
            def _():
                nb_next = tn_ref[i + 1]
                for b in range(MOE_NBT):
                    @pl.when(b < nb_next)
                    def _():
                        xs_copy(i + 1, b).start()

            def down(row0, rows):
                y = jnp.dot(act[pl.ds(row0, rows), :], wd_ref[...].astype(BF16),
                            preferred_element_type=F32) + bd_ref[...]
                obuf[slot, pl.ds(row0, rows), :] = y

            for_row_pieces(nb, down)
            for b in range(MOE_NBT):
                @pl.when(b < nb)
                def _():
                    out_copy(slot, b, n).start()

            @pl.when(n > 0)
            def _():
                for b in range(MOE_NBT):
                    @pl.when(b < nb)
                    def _():
                        out_copy(1 - slot, b, n - 1).wait()

            @pl.when(n == MOE_S2 - 1)
            def _():
                for b in range(MOE_NBT):
                    @pl.when(b < nb)
                    def _():
                        out_copy(slot, b, n).wait()

    @pl.when(i == ntiles)
    def _():
        @pl.when(s == 0)
        def _():
            obuf[0, 0:MOE_BLOCK, :] = jnp.zeros((MOE_BLOCK, MOE_NC), F32)

        first_tail = tb_ref[ntiles - 1] + tn_ref[ntiles - 1]
        per_step = -(-N_EXPERTS // (MOE_S1 + MOE_S2))
        for u in range(per_step):
            blk = first_tail + s * per_step + u

            @pl.when(blk < ys_hbm.shape[0] // MOE_BLOCK)
            def _():
                row0 = pl.multiple_of(blk * MOE_BLOCK, MOE_BLOCK)
                fills = [pltpu.make_async_copy(
                    obuf.at[0, pl.ds(0, MOE_BLOCK), :],
                    ys_hbm.at[pl.ds(row0, MOE_BLOCK), pl.ds(c * MOE_NC, MOE_NC)], osem.at[0])
                    for c in range(MOE_S2)]
                for f in fills:
                    f.start()
                for f in fills:
                    f.wait()


def _moe_experts(xs, n_rows, tile_e, tile_b0, tile_nb, n_tiles, w_gate_up, b_gate_up, w_down, b_down):

    def live(i, nt):
        return i < nt[0]

    def f_idx(i, s, nt):
        return jnp.where(live(i, nt), jnp.minimum(s, MOE_S1 - 1), MOE_S1 - 1)

    def n_idx(i, s, nt):
        return jnp.where(live(i, nt), jnp.maximum(s - MOE_S1, 0), MOE_S2 - 1)

    lin_off = D_FF // MOE_FC
    grid_spec = pltpu.PrefetchScalarGridSpec(
        num_scalar_prefetch=4,
        grid=(tile_e.shape[0], MOE_S1 + MOE_S2),
        in_specs=[
            pl.BlockSpec(memory_space=pl.ANY),
            pl.BlockSpec((None, D_MODEL, MOE_FC), lambda i, s, te, tb, tn, nt: (te[i], 0, f_idx(i, s, nt))),
            pl.BlockSpec((None, D_MODEL, MOE_FC),
                         lambda i, s, te, tb, tn, nt: (te[i], 0, lin_off + f_idx(i, s, nt))),
            pl.BlockSpec((None, 1, MOE_FC), lambda i, s, te, tb, tn, nt: (te[i], 0, f_idx(i, s, nt))),
            pl.BlockSpec((None, 1, MOE_FC),
                         lambda i, s, te, tb, tn, nt: (te[i], 0, lin_off + f_idx(i, s, nt))),
            pl.BlockSpec((None, D_FF, MOE_NC), lambda i, s, te, tb, tn, nt: (te[i], 0, n_idx(i, s, nt))),
            pl.BlockSpec((None, 1, MOE_NC), lambda i, s, te, tb, tn, nt: (te[i], 0, n_idx(i, s, nt))),
        ],
        out_specs=pl.BlockSpec(memory_space=pl.ANY),
        scratch_shapes=[pltpu.VMEM((MOE_TM, D_MODEL), BF16),
                        pltpu.VMEM((MOE_TM, D_FF), BF16),
                        pltpu.VMEM((2, MOE_TM, MOE_NC), F32),
                        pltpu.SemaphoreType.DMA((1,)),
                        pltpu.SemaphoreType.DMA((2,))],
    )
    return pl.pallas_call(
        _moe_kernel,
        out_shape=jax.ShapeDtypeStruct((n_rows, D_MODEL), F32),
        grid_spec=grid_spec,
        compiler_params=_cparams(("arbitrary", "arbitrary")),
        name="moe_experts",
    )(tile_e, tile_b0, tile_nb, n_tiles, xs, w_gate_up, w_gate_up,
      b_gate_up.reshape(N_EXPERTS, 1, 2 * D_FF), b_gate_up.reshape(N_EXPERTS, 1, 2 * D_FF),
      w_down, b_down.reshape(N_EXPERTS, 1, D_MODEL))


COMBINE_TT = 128


def _combine_kernel(pos_ref, nxt_ref, x_ref, gate_ref, ys_hbm, o_ref, buf, sem):
    i = pl.program_id(0)
    nsteps = pl.num_programs(0)
    slot = lax.rem(i, 2)

    def issue(ids, s):
        def body(r, carry):
            for k in range(TOP_K):
                pltpu.make_async_copy(ys_hbm.at[pl.ds(ids[0, 0, r * TOP_K + k], 1), :],
                                      buf.at[s, k, pl.ds(r, 1), :], sem.at[s]).start()
            return carry
        lax.fori_loop(0, COMBINE_TT, body, 0, unroll=4)

    @pl.when(i == 0)
    def _():
        issue(pos_ref, 0)

    @pl.when(i + 1 < nsteps)
    def _():
        issue(nxt_ref, 1 - slot)

    for k in range(TOP_K):
        pltpu.make_async_copy(ys_hbm.at[pl.ds(0, COMBINE_TT), :], buf.at[slot, k], sem.at[slot]).wait()
    gate = gate_ref[...]
    acc = x_ref[...]
    moe = gate[:, 0:1] * buf[slot, 0]
    for k in range(1, TOP_K):
        moe = moe + gate[:, k:k + 1] * buf[slot, k]
    o_ref[...] = acc + moe


def _combine(x1, gates, ys, pos):
    t, d = x1.shape
    tt = COMBINE_TT
    nsteps = t // tt
    pos3 = pos.reshape(nsteps, 1, tt * TOP_K)
    return pl.pallas_call(
        _combine_kernel,
        out_shape=jax.ShapeDtypeStruct((t, d), F32),
        grid=(nsteps,),
        in_specs=[pl.BlockSpec((1, 1, tt * TOP_K), lambda i: (i, 0, 0), memory_space=pltpu.SMEM),
                  pl.BlockSpec((1, 1, tt * TOP_K), lambda i: (jnp.minimum(i + 1, nsteps - 1), 0, 0),
                               memory_space=pltpu.SMEM),
                  pl.BlockSpec((tt, d), lambda i: (i, 0)),
                  pl.BlockSpec((tt, LANES), lambda i: (i, 0)),
                  pl.BlockSpec(memory_space=pl.ANY)],
        out_specs=pl.BlockSpec((tt, d), lambda i: (i, 0)),
        scratch_shapes=[pltpu.VMEM((2, TOP_K, tt, d), F32),
                        pltpu.SemaphoreType.DMA((2,))],
        compiler_params=_cparams(("arbitrary",)),
        name="combine",
    )(pos3, pos3, x1, gates, ys)


def _routing(top_idx):
    t = top_idx.shape[0]
    n_assign = t * TOP_K
    flat_e = top_idx.reshape(-1)
    experts = jnp.arange(N_EXPERTS, dtype=jnp.int32)
    order = jnp.argsort(flat_e)
    sorted_pos = jnp.argsort(order).astype(jnp.int32)
    counts = jnp.sum((flat_e[:, None] == experts[None, :]).astype(jnp.int32), axis=0)
    starts = jnp.cumsum(counts) - counts
    padded = (counts + MOE_BLOCK - 1) // MOE_BLOCK * MOE_BLOCK
    pad_ends = jnp.cumsum(padded)
    pad_starts = pad_ends - padded
    shift = pad_starts - starts
    pos = sorted_pos + shift[flat_e]
    n_rows = n_assign + N_EXPERTS * MOE_BLOCK
    n_rows = (n_rows + GATHER_ROWS - 1) // GATHER_ROWS * GATHER_ROWS
    rows = jnp.arange(n_rows, dtype=jnp.int32)
    row_e = jnp.minimum(jnp.sum((rows[:, None] >= pad_ends[None, :]).astype(jnp.int32), axis=1),
                        N_EXPERTS - 1)
    place = rows - shift[row_e]
    valid = (rows - pad_starts[row_e]) < counts[row_e]
    row_tok = jnp.where(valid, order[jnp.clip(place, 0, n_assign - 1)].astype(jnp.int32) // TOP_K, 0)

    nblk = padded // MOE_BLOCK
    blk0 = pad_starts // MOE_BLOCK
    ntile = (nblk + MOE_NBT - 1) // MOE_NBT
    tile_end = jnp.cumsum(ntile)
    n_tiles = tile_end[-1]
    ids = jnp.arange(_moe_max_tiles(t), dtype=jnp.int32)
    ids_c = jnp.minimum(ids, n_tiles - 1)
    tile_e = jnp.minimum(jnp.searchsorted(tile_end, ids_c, side='right'), N_EXPERTS - 1).astype(jnp.int32)
    local = ids_c - (tile_end - ntile)[tile_e]
    tile_b0 = (blk0[tile_e] + local * MOE_NBT).astype(jnp.int32)
    tile_nb = jnp.where(ids < n_tiles, jnp.clip(nblk[tile_e] - local * MOE_NBT, 0, MOE_NBT), 0).astype(jnp.int32)
    return row_tok, pos, tile_e, tile_b0, tile_nb, n_tiles.reshape(1).astype(jnp.int32)


def _rope_tables(seq):
    inv_freq = 1.0 / (ROPE_THETA ** (jnp.arange(0, HEAD_DIM, 2, dtype=F32) / HEAD_DIM))
    ang = jnp.arange(seq, dtype=F32)[:, None] * inv_freq[None, :]
    cos = jnp.cos(ang)
    sin = jnp.sin(ang)
    reps = LANES // HEAD_DIM
    cos_t = jnp.tile(jnp.concatenate([cos, cos], axis=-1), (1, reps))
    sin_t = jnp.tile(jnp.concatenate([-sin, sin], axis=-1), (1, reps))
    return cos_t, sin_t


def kernel(x, norm1_g, w_in, b_in, q_norm_g, k_norm_g, attn_sinks, w_attn_o, conv_dw_w, conv_dw_b,
           conv_ln_g, conv_ln_b, w_conv_o, w_out, norm2_g, w_router, b_router, w_gate_up, b_gate_up,
           w_down, b_down):
    b, s, d = x.shape
    t = b * s
    depth = norm1_g.shape[0]
    xt = x.reshape(t, d)
    cos_t, sin_t = _rope_tables(s)
    reps = LANES // HEAD_DIM
    c0 = QKV_WIDTH
    c1 = c0 + CONV_CH
    c2 = c1 + CONV_CH
    for l in range(depth):
        h = _rmsnorm(xt, norm1_g[l], BF16)
        qkv = _mm_bias(h, w_in, b_in, l, 0, c0, F32, tm=1024, tn=512, name="inproj_qkv")
        z = _mm_glu(h, w_in, b_in, l, c0, CONV_CH, tm=1024, tn=256)
        gates = _mm_bias(h, w_in, b_in, l, c2, 2 * d, BF16, tm=1024, tn=512, sigmoid=True,
                         name="inproj_gates")
        attn = _attention(qkv, attn_sinks[l], cos_t, sin_t,
                          jnp.tile(q_norm_g[l], reps).reshape(1, LANES),
                          jnp.tile(k_norm_g[l], reps).reshape(1, LANES), b, s)
        conv = _conformer_conv(z, conv_dw_w[l], conv_dw_b[l], conv_ln_g[l], conv_ln_b[l], b, s)
        merged = _merge(attn, conv, w_attn_o, w_conv_o, l, gates, tm=1024, tn=512)
        x1 = _mm_residual(merged, w_out, l, xt, tm=1024, tn=512)
        h2, idx_pad, gate_pad = _router(x1, norm2_g[l], w_router[l], b_router[l])
        row_tok, pos, tile_e, tile_b0, tile_nb, n_tiles = _routing(idx_pad[:, :TOP_K])
        xs = _gather_rows(h2, row_tok)
        ys = _moe_experts(xs, _moe_max_blocks(t) * MOE_BLOCK, tile_e, tile_b0, tile_nb, n_tiles,
                          w_gate_up[l], b_gate_up[l], w_down[l], b_down[l])
        xt = _combine(x1, gate_pad, ys, pos)
    return xt.reshape(b, s, d)
```

```python
import functools

import jax
import jax.numpy as jnp
from jax import lax
from jax.experimental import pallas as pl
from jax.experimental.pallas import tpu as pltpu

D_MODEL = 4096
HEAD_DIM = 64
N_Q_HEADS = 32
N_KV_HEADS = 4
WINDOW = 128
ATTN_BLOCK = 128
ROPE_THETA = 10000.0
CONV_CH = 2048
CONV_WIDTH = 31
N_EXPERTS = 32
TOP_K = 4
D_FF = 2048
SWIGLU_ALPHA = 1.702
SWIGLU_LIMIT = 7.0
MOE_BLOCK = 128
NORM_EPS = 1e-5

Q_WIDTH = N_Q_HEADS * HEAD_DIM
KV_WIDTH = N_KV_HEADS * HEAD_DIM
QKV_WIDTH = Q_WIDTH + 2 * KV_WIDTH

LANES = 128
SUBLANES = 8
VMEM_LIMIT = 56 * 1024 * 1024

BF16 = jnp.bfloat16
F32 = jnp.float32


def _cparams(sem, vmem=VMEM_LIMIT):
    return pltpu.CompilerParams(dimension_semantics=sem, vmem_limit_bytes=vmem)


def _rmsnorm_kernel(x_ref, g_ref, o_ref):
    x = x_ref[...]
    ms = jnp.mean(x * x, axis=-1, keepdims=True)
    o_ref[...] = (x * lax.rsqrt(ms + NORM_EPS) * g_ref[...]).astype(o_ref.dtype)


def _rmsnorm(x, g, out_dtype, tm=512):
    t, d = x.shape
    return pl.pallas_call(
        _rmsnorm_kernel,
        out_shape=jax.ShapeDtypeStruct((t, d), out_dtype),
        grid=(t // tm,),
        in_specs=[pl.BlockSpec((tm, d), lambda i: (i, 0)),
                  pl.BlockSpec((1, d), lambda i: (0, 0))],
        out_specs=pl.BlockSpec((tm, d), lambda i: (i, 0)),
        compiler_params=_cparams(("parallel",)),
        name="rmsnorm",
    )(x, g.reshape(1, d))


def _wspec(k, tn, layer, off=0):
    return pl.BlockSpec((None, k, tn), lambda j, i: (layer, 0, j + off))


def _cast_on_first_row_block(pairs):
    @pl.when(pl.program_id(1) == 0)
    def _():
        for w_ref, wb_ref in pairs:
            wb_ref[...] = w_ref[...].astype(BF16)


def _mm_bias_kernel(a_ref, w_ref, b_ref, o_ref, wb, *, sigmoid):
    _cast_on_first_row_block([(w_ref, wb)])
    acc = jnp.dot(a_ref[...], wb[...], preferred_element_type=F32) + b_ref[...]
    if sigmoid:
        acc = jax.nn.sigmoid(acc)
    o_ref[...] = acc.astype(o_ref.dtype)


def _mm_bias(a, w, b, layer, col0, n, out_dtype, *, tm, tn, sigmoid=False, name):
    m, k = a.shape
    off = col0 // tn
    return pl.pallas_call(
        functools.partial(_mm_bias_kernel, sigmoid=sigmoid),
        out_shape=jax.ShapeDtypeStruct((m, n), out_dtype),
        grid=(n // tn, m // tm),
        in_specs=[pl.BlockSpec((tm, k), lambda j, i: (i, 0)),
                  _wspec(k, tn, layer, off),
                  _wspec(1, tn, layer, off)],
        out_specs=pl.BlockSpec((tm, tn), lambda j, i: (i, j)),
        scratch_shapes=[pltpu.VMEM((k, tn), BF16)],
        compiler_params=_cparams(("arbitrary", "arbitrary")),
        name=name,
    )(a, w, b.reshape(b.shape[0], 1, b.shape[1]))


def _mm_glu_kernel(a_ref, wa_ref, wg_ref, ba_ref, bg_ref, o_ref, wab, wgb):
    _cast_on_first_row_block([(wa_ref, wab), (wg_ref, wgb)])
    a = a_ref[...]
    u = jnp.dot(a, wab[...], preferred_element_type=F32) + ba_ref[...]
    g = jnp.dot(a, wgb[...], preferred_element_type=F32) + bg_ref[...]
    o_ref[...] = (u * jax.nn.sigmoid(g)).astype(o_ref.dtype)


def _mm_glu(a, w, b, layer, col0, n, *, tm, tn):
    m, k = a.shape
    off_u = col0 // tn
    off_g = (col0 + n) // tn
    b3 = b.reshape(b.shape[0], 1, b.shape[1])
    return pl.pallas_call(
        _mm_glu_kernel,
        out_shape=jax.ShapeDtypeStruct((m, n), F32),
        grid=(n // tn, m // tm),
        in_specs=[pl.BlockSpec((tm, k), lambda j, i: (i, 0)),
                  _wspec(k, tn, layer, off_u), _wspec(k, tn, layer, off_g),
                  _wspec(1, tn, layer, off_u), _wspec(1, tn, layer, off_g)],
        out_specs=pl.BlockSpec((tm, tn), lambda j, i: (i, j)),
        scratch_shapes=[pltpu.VMEM((k, tn), BF16), pltpu.VMEM((k, tn), BF16)],
        compiler_params=_cparams(("arbitrary", "arbitrary")),
        name="inproj_glu",
    )(a, w, w, b3, b3)


def _merge_kernel(a_ref, c_ref, wa_ref, wc_ref, ga_ref, gc_ref, o_ref, wab, wcb):
    _cast_on_first_row_block([(wa_ref, wab), (wc_ref, wcb)])
    pa = jnp.dot(a_ref[...], wab[...], preferred_element_type=F32)
    pc = jnp.dot(c_ref[...], wcb[...], preferred_element_type=F32)
    o_ref[...] = (ga_ref[...].astype(F32) * pa + gc_ref[...].astype(F32) * pc).astype(o_ref.dtype)


def _merge(attn, conv, wa, wc, layer, gates, *, tm, tn):
    m, k = attn.shape
    n = wa.shape[2]
    nj = n // tn
    return pl.pallas_call(
        _merge_kernel,
        out_shape=jax.ShapeDtypeStruct((m, n), BF16),
        grid=(nj, m // tm),
        in_specs=[pl.BlockSpec((tm, k), lambda j, i: (i, 0)),
                  pl.BlockSpec((tm, k), lambda j, i: (i, 0)),
                  _wspec(k, tn, layer), _wspec(k, tn, layer),
                  pl.BlockSpec((tm, tn), lambda j, i: (i, j)),
                  pl.BlockSpec((tm, tn), lambda j, i: (i, j + nj))],
        out_specs=pl.BlockSpec((tm, tn), lambda j, i: (i, j)),
        scratch_shapes=[pltpu.VMEM((k, tn), BF16), pltpu.VMEM((k, tn), BF16)],
        compiler_params=_cparams(("arbitrary", "arbitrary")),
        name="merge",
    )(attn, conv, wa, wc, gates, gates)


def _mm_residual_kernel(a_ref, w_ref, x_ref, o_ref, wb):
    _cast_on_first_row_block([(w_ref, wb)])
    o_ref[...] = x_ref[...] + jnp.dot(a_ref[...], wb[...], preferred_element_type=F32)


def _mm_residual(a, w, layer, x, *, tm, tn):
    m, k = a.shape
    n = w.shape[2]
    return pl.pallas_call(
        _mm_residual_kernel,
        out_shape=jax.ShapeDtypeStruct((m, n), F32),
        grid=(n // tn, m // tm),
        in_specs=[pl.BlockSpec((tm, k), lambda j, i: (i, 0)),
                  _wspec(k, tn, layer),
                  pl.BlockSpec((tm, tn), lambda j, i: (i, j))],
        out_specs=pl.BlockSpec((tm, tn), lambda j, i: (i, j)),
        scratch_shapes=[pltpu.VMEM((k, tn), BF16)],
        compiler_params=_cparams(("arbitrary", "arbitrary")),
        name="outproj",
    )(a, w, x)


ATTN_RC = 32


def _attn_kernel(sink_ref, q_ref, k_ref, v_ref, cos_ref, sin_ref, qg_ref, kg_ref, o_ref,
                 kwin, vwin, kb, vb, qs, bias_scr, s_scr, p_scr, inv_scr):
    n = pl.program_id(1)
    blk = ATTN_BLOCK

    @pl.when(n == 0)
    def _():
        kwin[0:blk, :] = jnp.zeros((blk, KV_WIDTH), F32)
        vwin[0:blk, :] = jnp.zeros((blk, KV_WIDTH), F32)

    cos = cos_ref[...]
    sin = sin_ref[...]
    lane = lax.broadcasted_iota(jnp.int32, (blk, LANES), 1)
    first_half = (lane % HEAD_DIM) < (HEAD_DIM // 2)
    seg_r = lax.broadcasted_iota(jnp.int32, (LANES, LANES), 0) // HEAD_DIM
    seg_c = lax.broadcasted_iota(jnp.int32, (LANES, LANES), 1) // HEAD_DIM
    seg = jnp.where(seg_r == seg_c, 1.0, 0.0).astype(BF16)

    def norm_rope(t, g):
        t2 = t * t
        hi = t2.astype(BF16)
        lo = (t2 - hi.astype(F32)).astype(BF16)
        ss = (jnp.dot(hi, seg, preferred_element_type=F32)
              + jnp.dot(lo, seg, preferred_element_type=F32))
        tn = t * lax.rsqrt(ss * (1.0 / HEAD_DIM) + NORM_EPS) * g
        rot = jnp.where(first_half, pltpu.roll(tn, LANES - HEAD_DIM // 2, 1),
                        pltpu.roll(tn, HEAD_DIM // 2, 1))
        return tn * cos + rot * sin

    kg = kg_ref[...]
    qg = qg_ref[...]
    for c in range(KV_WIDTH // LANES):
        cols = slice(c * LANES, (c + 1) * LANES)
        kwin[blk:2 * blk, cols] = norm_rope(k_ref[:, cols], kg)
    vwin[blk:2 * blk, :] = v_ref[...]

    lane2 = lax.broadcasted_iota(jnp.int32, (2 * blk, LANES), 1)
    lo_half = lane2 < HEAD_DIM
    for c in range(KV_WIDTH // LANES):
        cols = slice(c * LANES, (c + 1) * LANES)
        for win, dst in ((kwin, kb), (vwin, vb)):
            x = win[:, cols]
            xs = pltpu.roll(x, HEAD_DIM, 1)
            dst[2 * c, 0:2 * blk, :] = jnp.where(lo_half, x, 0.0).astype(BF16)
            dst[2 * c, 2 * blk:4 * blk, :] = jnp.where(lo_half, 0.0, xs).astype(BF16)
            dst[2 * c + 1, 0:2 * blk, :] = jnp.where(lo_half, xs, 0.0).astype(BF16)
            dst[2 * c + 1, 2 * blk:4 * blk, :] = jnp.where(lo_half, 0.0, x).astype(BF16)

    qi = lax.broadcasted_iota(jnp.int32, (blk, 4 * blk), 0)
    kj = lax.broadcasted_iota(jnp.int32, (blk, 4 * blk), 1) % (2 * blk)
    rel = qi + blk - kj
    first_key = jnp.where(n == 0, blk, 0)
    mask = (rel >= 0) & (rel < WINDOW) & (kj >= first_key)
    bias_scr[...] = jnp.where(mask, 0.0, -jnp.inf)

    pairs_per_kv = N_Q_HEADS // N_KV_HEADS // 2
    for p in range(N_Q_HEADS // 2):
        g, pp = divmod(p, pairs_per_kv)
        q2 = norm_rope(q_ref[:, p * LANES:(p + 1) * LANES], qg) * (HEAD_DIM ** -0.5)
        qs[g, pp * blk:(pp + 1) * blk, :] = q2.astype(BF16)

    lo_out = lax.broadcasted_iota(jnp.int32, (ATTN_RC, LANES), 1) < HEAD_DIM
    for g in range(N_KV_HEADS):
        s_scr[g] = lax.dot_general(qs[g], kb[g], (((1,), (1,)), ((), ())), preferred_element_type=F32)
        for r in range(pairs_per_kv * blk // ATTN_RC):
            rows = slice(r * ATTN_RC, (r + 1) * ATTN_RC)
            pp, q0 = divmod(r * ATTN_RC, blk)
            s = s_scr[g, rows, :] + bias_scr[q0:q0 + ATTN_RC, :]
            invs = []
            for h in range(2):
                keys = slice(h * 2 * blk, (h + 1) * 2 * blk)
                sh = s[:, keys]
                sink = sink_ref[2 * (g * pairs_per_kv + pp) + h]
                m = jnp.maximum(jnp.max(sh, axis=-1, keepdims=True), sink)
                e = jnp.exp(sh - m)
                den = jnp.sum(e, axis=-1, keepdims=True) + jnp.exp(sink - m)
                p_scr[g, rows, keys] = e.astype(BF16)
                invs.append(1.0 / den)
            inv_scr[g, rows, :] = jnp.where(lo_out, invs[0], invs[1])
        o = jnp.dot(p_scr[g], vb[g], preferred_element_type=F32) * inv_scr[g]
        for pp in range(pairs_per_kv):
            p = g * pairs_per_kv + pp
            o_ref[:, p * LANES:(p + 1) * LANES] = o[pp * blk:(pp + 1) * blk, :].astype(o_ref.dtype)

    kwin[0:blk, :] = kwin[blk:2 * blk, :]
    vwin[0:blk, :] = vwin[blk:2 * blk, :]


def _attention(qkv, sinks, cos_t, sin_t, qg, kg, batch, seq):
    t = qkv.shape[0]
    blk = ATTN_BLOCK
    nb = seq // blk
    kcol = Q_WIDTH // KV_WIDTH
    return pl.pallas_call(
        _attn_kernel,
        out_shape=jax.ShapeDtypeStruct((t, Q_WIDTH), BF16),
        grid=(batch, nb),
        in_specs=[pl.BlockSpec(memory_space=pltpu.SMEM),
                  pl.BlockSpec((blk, Q_WIDTH), lambda b, n: (b * nb + n, 0)),
                  pl.BlockSpec((blk, KV_WIDTH), lambda b, n: (b * nb + n, kcol)),
                  pl.BlockSpec((blk, KV_WIDTH), lambda b, n: (b * nb + n, kcol + 1)),
                  pl.BlockSpec((blk, LANES), lambda b, n: (n, 0)),
                  pl.BlockSpec((blk, LANES), lambda b, n: (n, 0)),
                  pl.BlockSpec((1, LANES), lambda b, n: (0, 0)),
                  pl.BlockSpec((1, LANES), lambda b, n: (0, 0))],
        out_specs=pl.BlockSpec((blk, Q_WIDTH), lambda b, n: (b * nb + n, 0)),
        scratch_shapes=[pltpu.VMEM((2 * blk, KV_WIDTH), F32),
                        pltpu.VMEM((2 * blk, KV_WIDTH), F32),
                        pltpu.VMEM((N_KV_HEADS, 4 * blk, LANES), BF16),
                        pltpu.VMEM((N_KV_HEADS, 4 * blk, LANES), BF16),
                        pltpu.VMEM((N_KV_HEADS, 4 * blk, LANES), BF16),
                        pltpu.VMEM((blk, 4 * blk), F32),
                        pltpu.VMEM((N_KV_HEADS, 4 * blk, 4 * blk), F32),
                        pltpu.VMEM((N_KV_HEADS, 4 * blk, 4 * blk), BF16),
                        pltpu.VMEM((N_KV_HEADS, 4 * blk, LANES), F32)],
        compiler_params=_cparams(("arbitrary", "arbitrary")),
        name="swa_attention",
    )(sinks, qkv, qkv, qkv, cos_t, sin_t, qg, kg)


CONV_TS = 256
CONV_HALO = 32
CONV_RB = 128


def _conv_kernel(z_ref, w_ref, b_ref, lg_ref, lb_ref, o_ref, zbuf, cbuf):
    n = pl.program_id(1)

    @pl.when(n == 0)
    def _():
        zbuf[0:CONV_HALO, :] = jnp.zeros((CONV_HALO, CONV_CH), F32)

    zbuf[CONV_HALO:CONV_HALO + CONV_TS, :] = z_ref[...]
    shift = CONV_HALO - (CONV_WIDTH - 1)

    def chunk(c, carry):
        c0 = pl.multiple_of(c * LANES, LANES)
        for rb in range(CONV_TS // CONV_RB):
            base = rb * CONV_RB
            acc = jnp.broadcast_to(b_ref[:, pl.ds(c0, LANES)], (CONV_RB, LANES))
            for res in range(SUBLANES):
                extra = SUBLANES if res else 0
                part = None
                for off in range(shift, shift + CONV_WIDTH):
                    if off % SUBLANES != res:
                        continue
                    r0 = base + off - res
                    term = (zbuf[r0:r0 + CONV_RB + extra, pl.ds(c0, LANES)]
                            * w_ref[off - shift:off - shift + 1, pl.ds(c0, LANES)])
                    part = term if part is None else part + term
                acc = acc + part[res:res + CONV_RB]
            cbuf[base:base + CONV_RB, pl.ds(c0, LANES)] = acc
        return carry

    lax.fori_loop(0, CONV_CH // LANES, chunk, 0)

    zbuf[0:CONV_HALO, :] = zbuf[CONV_TS:CONV_TS + CONV_HALO, :]

    y = cbuf[...]
    mu = jnp.mean(y, axis=-1, keepdims=True)
    yc = y - mu
    var = jnp.mean(yc * yc, axis=-1, keepdims=True)
    yn = yc * lax.rsqrt(var + NORM_EPS) * lg_ref[...] + lb_ref[...]
    o_ref[...] = (yn * jax.nn.sigmoid(yn)).astype(o_ref.dtype)


def _conformer_conv(z, dw_w, dw_b, ln_g, ln_b, batch, seq):
    t = z.shape[0]
    ns = seq // CONV_TS
    vec = lambda a: a.reshape(1, CONV_CH)
    return pl.pallas_call(
        _conv_kernel,
        out_shape=jax.ShapeDtypeStruct((t, CONV_CH), BF16),
        grid=(batch, ns),
        in_specs=[pl.BlockSpec((CONV_TS, CONV_CH), lambda b, n: (b * ns + n, 0)),
                  pl.BlockSpec((CONV_WIDTH, CONV_CH), lambda b, n: (0, 0)),
                  pl.BlockSpec((1, CONV_CH), lambda b, n: (0, 0)),
                  pl.BlockSpec((1, CONV_CH), lambda b, n: (0, 0)),
                  pl.BlockSpec((1, CONV_CH), lambda b, n: (0, 0))],
        out_specs=pl.BlockSpec((CONV_TS, CONV_CH), lambda b, n: (b * ns + n, 0)),
        scratch_shapes=[pltpu.VMEM((CONV_HALO + CONV_TS, CONV_CH), F32),
                        pltpu.VMEM((CONV_TS, CONV_CH), F32)],
        compiler_params=_cparams(("arbitrary", "arbitrary")),
        name="conformer_conv",
    )(z, dw_w, vec(dw_b), vec(ln_g), vec(ln_b))


ROUTER_TM = 256
HI16 = 0xFFFF0000


def _pack_halves(xb):
    n = xb.shape[1] // 2
    lo = lax.bitcast_convert_type(xb[:, :n].astype(F32), jnp.uint32)
    hi = lax.bitcast_convert_type(xb[:, n:].astype(F32), jnp.uint32)
    return (lo >> 16) | (hi & jnp.uint32(HI16))


def _unpack_halves(w):
    lo = lax.bitcast_convert_type(w << 16, F32).astype(BF16)
    hi = lax.bitcast_convert_type(w & jnp.uint32(HI16), F32).astype(BF16)
    return lo, hi


def _router_kernel(x_ref, g_ref, wr_ref, br_ref, h_ref, idx_ref, gate_ref):
    x = x_ref[...]
    ms = jnp.mean(x * x, axis=-1, keepdims=True)
    hb = (x * lax.rsqrt(ms + NORM_EPS) * g_ref[...]).astype(BF16)
    h_ref[...] = _pack_halves(hb)
    vals = jnp.dot(hb, wr_ref[...], preferred_element_type=F32) + br_ref[...]
    lane = lax.broadcasted_iota(jnp.int32, vals.shape, 1).astype(F32)
    tops, idxs = [], []
    for _ in range(TOP_K):
        m = jnp.max(vals, axis=-1, keepdims=True)
        idx = jnp.min(jnp.where(vals == m, lane, float(LANES)), axis=-1, keepdims=True)
        tops.append(m)
        idxs.append(idx)
        vals = jnp.where(lane == idx, -jnp.inf, vals)
    es = [jnp.exp(v - tops[0]) for v in tops]
    den = es[0] + es[1] + es[2] + es[3]
    idx_out = jnp.zeros(vals.shape, F32)
    gate_out = jnp.zeros(vals.shape, F32)
    for k in range(TOP_K):
        idx_out = jnp.where(lane == float(k), idxs[k], idx_out)
        gate_out = jnp.where(lane == float(k), es[k] / den, gate_out)
    idx_ref[...] = idx_out.astype(jnp.int32)
    gate_ref[...] = gate_out


def _router(x1, g, w_router, b_router):
    t, d = x1.shape
    tm = ROUTER_TM
    wr = jnp.zeros((d, LANES), BF16).at[:, :N_EXPERTS].set(w_router.astype(BF16))
    br = jnp.full((1, LANES), -1e30, F32).at[0, :N_EXPERTS].set(b_router)
    return pl.pallas_call(
        _router_kernel,
        out_shape=(jax.ShapeDtypeStruct((t, d // 2), jnp.uint32),
                   jax.ShapeDtypeStruct((t, LANES), jnp.int32),
                   jax.ShapeDtypeStruct((t, LANES), F32)),
        grid=(t // tm,),
        in_specs=[pl.BlockSpec((tm, d), lambda i: (i, 0)),
                  pl.BlockSpec((1, d), lambda i: (0, 0)),
                  pl.BlockSpec((d, LANES), lambda i: (0, 0)),
                  pl.BlockSpec((1, LANES), lambda i: (0, 0))],
        out_specs=(pl.BlockSpec((tm, d // 2), lambda i: (i, 0)),
                   pl.BlockSpec((tm, LANES), lambda i: (i, 0)),
                   pl.BlockSpec((tm, LANES), lambda i: (i, 0))),
        compiler_params=_cparams(("parallel",)),
        name="router",
    )(x1, g.reshape(1, d), wr, br)


GATHER_ROWS = 1024
GATHER_UNROLL = 8
GATHER_CHUNK = 128


def _gather_kernel(idx_ref, nxt_ref, src_hbm, o_ref, buf, sem):
    i = pl.program_id(0)
    nsteps = pl.num_programs(0)
    slot = lax.rem(i, 2)

    def issue(ids, s):
        def body(q, carry):
            for u in range(GATHER_UNROLL):
                r = q * GATHER_UNROLL + u
                pltpu.make_async_copy(src_hbm.at[pl.ds(ids[0, 0, r], 1), :],
                                      buf.at[s, pl.ds(r, 1), :], sem.at[s]).start(priority=u % 2)
            return carry
        lax.fori_loop(0, GATHER_ROWS // GATHER_UNROLL, body, 0)

    @pl.when(i == 0)
    def _():
        issue(idx_ref, 0)

    @pl.when(i + 1 < nsteps)
    def _():
        issue(nxt_ref, 1 - slot)

    pltpu.make_async_copy(src_hbm.at[pl.ds(0, GATHER_ROWS), :], buf.at[slot], sem.at[slot]).wait()
    half = buf.shape[2]
    for c in range(GATHER_ROWS // GATHER_CHUNK):
        rows = slice(c * GATHER_CHUNK, (c + 1) * GATHER_CHUNK)
        lo, hi = _unpack_halves(buf[slot, rows, :])
        o_ref[rows, 0:half] = lo
        o_ref[rows, half:2 * half] = hi


def _gather_rows(src, row_idx):
    n = row_idx.shape[0]
    d = src.shape[1]
    nsteps = n // GATHER_ROWS
    idx3 = row_idx.reshape(nsteps, 1, GATHER_ROWS)
    return pl.pallas_call(
        _gather_kernel,
        out_shape=jax.ShapeDtypeStruct((n, 2 * d), BF16),
        grid=(nsteps,),
        in_specs=[pl.BlockSpec((1, 1, GATHER_ROWS), lambda i: (i, 0, 0), memory_space=pltpu.SMEM),
                  pl.BlockSpec((1, 1, GATHER_ROWS), lambda i: (jnp.minimum(i + 1, nsteps - 1), 0, 0),
                               memory_space=pltpu.SMEM),
                  pl.BlockSpec(memory_space=pl.ANY)],
        out_specs=pl.BlockSpec((GATHER_ROWS, 2 * d), lambda i: (i, 0)),
        scratch_shapes=[pltpu.VMEM((2, GATHER_ROWS, d), src.dtype),
                        pltpu.SemaphoreType.DMA((2,))],
        compiler_params=_cparams(("arbitrary",)),
        name="dispatch_gather",
    )(idx3, idx3, src)


MOE_NBT = 10
MOE_TM = MOE_NBT * MOE_BLOCK
MOE_FAST = (7, 8, 9)
MOE_PIECE = 4
MOE_FC = 256
MOE_NC = 512
MOE_S1 = D_FF // MOE_FC
MOE_S2 = D_MODEL // MOE_NC


def _moe_max_blocks(t):
    return t * TOP_K // MOE_BLOCK + N_EXPERTS


def _moe_max_tiles(t):
    return (_moe_max_blocks(t) + N_EXPERTS * (MOE_NBT - 1)) // MOE_NBT + 1


def _moe_kernel(te_ref, tb_ref, tn_ref, nt_ref,
                xs_hbm, wg_ref, wl_ref, bg_ref, bl_ref, wd_ref, bd_ref, ys_hbm,
                xbuf, act, obuf, xsem, osem):
    i = pl.program_id(0)
    s = pl.program_id(1)
    ntiles = nt_ref[0]

    def xs_copy(tile, row0, rows):
        src0 = pl.multiple_of(tb_ref[tile] * MOE_BLOCK + row0, MOE_BLOCK)
        return pltpu.make_async_copy(xs_hbm.at[pl.ds(src0, rows), :],
                                     xbuf.at[pl.ds(row0, rows), :], xsem.at[0])

    def for_row_pieces(nb, emit):
        general = nb >= 0
        for f in MOE_FAST:
            general = general & (nb != f)

            @pl.when(nb == f)
            def _():
                emit(0, f * MOE_BLOCK)

        @pl.when(general)
        def _():
            nbig = nb // MOE_PIECE
            big = MOE_PIECE * MOE_BLOCK

            def big_piece(q, carry):
                emit(pl.multiple_of(q * big, big), big)
                return carry

            def small_piece(b, carry):
                emit(pl.multiple_of(b * MOE_BLOCK, MOE_BLOCK), MOE_BLOCK)
                return carry

            lax.fori_loop(0, nbig, big_piece, 0)
            lax.fori_loop(nbig * MOE_PIECE, nb, small_piece, 0)

    def out_copy(slot, row0, rows, n):
        dst0 = pl.multiple_of(tb_ref[i] * MOE_BLOCK + row0, MOE_BLOCK)
        col0 = pl.multiple_of(n * MOE_NC, MOE_NC)
        return pltpu.make_async_copy(obuf.at[slot, pl.ds(row0, rows), :],
                                     ys_hbm.at[pl.ds(dst0, rows), pl.ds(col0, MOE_NC)],
                                     osem.at[slot])

    @pl.when(i < ntiles)
    def _():
        nb = tn_ref[i]

        @pl.when((i == 0) & (s == 0))
        def _():
            for_row_pieces(nb, lambda row0, rows: xs_copy(i, row0, rows).start())

        @pl.when(s == 0)
        def _():
            for_row_pieces(nb, lambda row0, rows: xs_copy(i, row0, rows).wait())

        @pl.when(s < MOE_S1)
        def _():
            c0 = pl.multiple_of(s * MOE_FC, MOE_FC)

            def gate_up(row0, rows):
                x = xbuf[pl.ds(row0, rows), :]
                glu = jnp.dot(x, wg_ref[...].astype(BF16), preferred_element_type=F32) + bg_ref[...]
                lin = jnp.dot(x, wl_ref[...].astype(BF16), preferred_element_type=F32) + bl_ref[...]
                glu = jnp.minimum(glu, SWIGLU_LIMIT)
                lin = jnp.clip(lin, -SWIGLU_LIMIT, SWIGLU_LIMIT)
                a = glu * jax.nn.sigmoid(SWIGLU_ALPHA * glu) * (lin + 1.0)
                act[pl.ds(row0, rows), pl.ds(c0, MOE_FC)] = a.astype(BF16)

            for_row_pieces(nb, gate_up)

        @pl.when(s >= MOE_S1)
        def _():
            n = s - MOE_S1
            slot = lax.rem(n, 2)

            @pl.when((s == MOE_S1) & (i + 1 < ntiles))
            def _():
                for_row_pieces(tn_ref[i + 1], lambda row0, rows: xs_copy(i + 1, row0, rows).start())

            def down(row0, rows):
                y = jnp.dot(act[pl.ds(row0, rows), :], wd_ref[...].astype(BF16),
                            preferred_element_type=F32) + bd_ref[...]
                obuf[slot, pl.ds(row0, rows), :] = y
                out_copy(slot, row0, rows, n).start()

            for_row_pieces(nb, down)

            @pl.when(n > 0)
            def _():
                for_row_pieces(nb, lambda row0, rows: out_copy(1 - slot, row0, rows, n - 1).wait())

            @pl.when(n == MOE_S2 - 1)
            def _():
                for_row_pieces(nb, lambda row0, rows: out_copy(slot, row0, rows, n).wait())

    @pl.when(i == ntiles)
    def _():
        @pl.when(s == 0)
        def _():
            obuf[0, 0:MOE_BLOCK, :] = jnp.zeros((MOE_BLOCK, MOE_NC), F32)

        first_tail = tb_ref[ntiles - 1] + tn_ref[ntiles - 1]
        per_step = -(-N_EXPERTS // (MOE_S1 + MOE_S2))
        for u in range(per_step):
            blk = first_tail + s * per_step + u

            @pl.when(blk < ys_hbm.shape[0] // MOE_BLOCK)
            def _():
                row0 = pl.multiple_of(blk * MOE_BLOCK, MOE_BLOCK)
                fills = [pltpu.make_async_copy(
                    obuf.at[0, pl.ds(0, MOE_BLOCK), :],
                    ys_hbm.at[pl.ds(row0, MOE_BLOCK), pl.ds(c * MOE_NC, MOE_NC)], osem.at[0])
                    for c in range(MOE_S2)]
                for f in fills:
                    f.start()
                for f in fills:
                    f.wait()


def _moe_experts(xs, n_rows, tile_e, tile_b0, tile_nb, n_tiles, w_gate_up, b_gate_up, w_down, b_down):

    def live(i, nt):
        return i < nt[0]

    def f_idx(i, s, nt):
        return jnp.where(live(i, nt), jnp.minimum(s, MOE_S1 - 1), MOE_S1 - 1)

    def n_idx(i, s, nt):
        return jnp.where(live(i, nt), jnp.maximum(s - MOE_S1, 0), MOE_S2 - 1)

    lin_off = D_FF // MOE_FC
    grid_spec = pltpu.PrefetchScalarGridSpec(
        num_scalar_prefetch=4,
        grid=(tile_e.shape[0], MOE_S1 + MOE_S2),
        in_specs=[
            pl.BlockSpec(memory_space=pl.ANY),
            pl.BlockSpec((None, D_MODEL, MOE_FC), lambda i, s, te, tb, tn, nt: (te[i], 0, f_idx(i, s, nt))),
            pl.BlockSpec((None, D_MODEL, MOE_FC),
                         lambda i, s, te, tb, tn, nt: (te[i], 0, lin_off + f_idx(i, s, nt))),
            pl.BlockSpec((None, 1, MOE_FC), lambda i, s, te, tb, tn, nt: (te[i], 0, f_idx(i, s, nt))),
            pl.BlockSpec((None, 1, MOE_FC),
                         lambda i, s, te, tb, tn, nt: (te[i], 0, lin_off + f_idx(i, s, nt))),
            pl.BlockSpec((None, D_FF, MOE_NC), lambda i, s, te, tb, tn, nt: (te[i], 0, n_idx(i, s, nt))),
            pl.BlockSpec((None, 1, MOE_NC), lambda i, s, te, tb, tn, nt: (te[i], 0, n_idx(i, s, nt))),
        ],
        out_specs=pl.BlockSpec(memory_space=pl.ANY),
        scratch_shapes=[pltpu.VMEM((MOE_TM, D_MODEL), BF16),
                        pltpu.VMEM((MOE_TM, D_FF), BF16),
                        pltpu.VMEM((2, MOE_TM, MOE_NC), F32),
                        pltpu.SemaphoreType.DMA((1,)),
                        pltpu.SemaphoreType.DMA((2,))],
    )
    return pl.pallas_call(
        _moe_kernel,
        out_shape=jax.ShapeDtypeStruct((n_rows, D_MODEL), F32),
        grid_spec=grid_spec,
        compiler_params=_cparams(("arbitrary", "arbitrary")),
        name="moe_experts",
    )(tile_e, tile_b0, tile_nb, n_tiles, xs, w_gate_up, w_gate_up,
      b_gate_up.reshape(N_EXPERTS, 1, 2 * D_FF), b_gate_up.reshape(N_EXPERTS, 1, 2 * D_FF),
      w_down, b_down.reshape(N_EXPERTS, 1, D_MODEL))


COMBINE_TT = 128


def _combine_kernel(pos_ref, nxt_ref, x_ref, gate_ref, ys_hbm, o_ref, buf, sem):
    i = pl.program_id(0)
    nsteps = pl.num_programs(0)
    slot = lax.rem(i, 2)

    def issue(ids, s):
        def body(r, carry):
            for k in range(TOP_K):
                pltpu.make_async_copy(ys_hbm.at[pl.ds(ids[0, 0, r * TOP_K + k], 1), :],
                                      buf.at[s, k, pl.ds(r, 1), :], sem.at[s]).start(priority=k % 2)
            return carry
        lax.fori_loop(0, COMBINE_TT, body, 0, unroll=4)

    @pl.when(i == 0)
    def _():
        issue(pos_ref, 0)

    @pl.when(i + 1 < nsteps)
    def _():
        issue(nxt_ref, 1 - slot)

    for k in range(TOP_K):
        pltpu.make_async_copy(ys_hbm.at[pl.ds(0, COMBINE_TT), :], buf.at[slot, k], sem.at[slot]).wait()
    gate = gate_ref[...]
    acc = x_ref[...]
    moe = gate[:, 0:1] * buf[slot, 0]
    for k in range(1, TOP_K):
        moe = moe + gate[:, k:k + 1] * buf[slot, k]
    o_ref[...] = acc + moe


def _combine(x1, gates, ys, pos):
    t, d = x1.shape
    tt = COMBINE_TT
    nsteps = t // tt
    pos3 = pos.reshape(nsteps, 1, tt * TOP_K)
    return pl.pallas_call(
        _combine_kernel,
        out_shape=jax.ShapeDtypeStruct((t, d), F32),
        grid=(nsteps,),
        in_specs=[pl.BlockSpec((1, 1, tt * TOP_K), lambda i: (i, 0, 0), memory_space=pltpu.SMEM),
                  pl.BlockSpec((1, 1, tt * TOP_K), lambda i: (jnp.minimum(i + 1, nsteps - 1), 0, 0),
                               memory_space=pltpu.SMEM),
                  pl.BlockSpec((tt, d), lambda i: (i, 0)),
                  pl.BlockSpec((tt, LANES), lambda i: (i, 0)),
                  pl.BlockSpec(memory_space=pl.ANY)],
        out_specs=pl.BlockSpec((tt, d), lambda i: (i, 0)),
        scratch_shapes=[pltpu.VMEM((2, TOP_K, tt, d), F32),
                        pltpu.SemaphoreType.DMA((2,))],
        compiler_params=_cparams(("arbitrary",)),
        name="combine",
    )(pos3, pos3, x1, gates, ys)


def _routing(top_idx):
    t = top_idx.shape[0]
    n_assign = t * TOP_K
    flat_e = top_idx.reshape(-1)
    experts = jnp.arange(N_EXPERTS, dtype=jnp.int32)
    order = jnp.argsort(flat_e)
    sorted_pos = jnp.argsort(order).astype(jnp.int32)
    counts = jnp.sum((flat_e[:, None] == experts[None, :]).astype(jnp.int32), axis=0)
    starts = jnp.cumsum(counts) - counts
    padded = (counts + MOE_BLOCK - 1) // MOE_BLOCK * MOE_BLOCK
    pad_ends = jnp.cumsum(padded)
    pad_starts = pad_ends - padded
    shift = pad_starts - starts
    pos = sorted_pos + shift[flat_e]
    n_rows = n_assign + N_EXPERTS * MOE_BLOCK
    n_rows = (n_rows + GATHER_ROWS - 1) // GATHER_ROWS * GATHER_ROWS
    blocks = jnp.arange(n_rows // MOE_BLOCK, dtype=jnp.int32)
    blk_e = jnp.minimum(jnp.sum((blocks[:, None] * MOE_BLOCK >= pad_ends[None, :]).astype(jnp.int32), axis=1),
                        N_EXPERTS - 1)
    rows = jnp.arange(n_rows, dtype=jnp.int32).reshape(-1, MOE_BLOCK)
    place = rows - shift[blk_e][:, None]
    valid = (rows - pad_starts[blk_e][:, None]) < counts[blk_e][:, None]
    row_tok = jnp.where(valid, order[jnp.clip(place, 0, n_assign - 1)].astype(jnp.int32) // TOP_K, 0)
    row_tok = row_tok.reshape(-1)

    nblk = padded // MOE_BLOCK
    blk0 = pad_starts // MOE_BLOCK
    ntile = (nblk + MOE_NBT - 1) // MOE_NBT
    tile_end = jnp.cumsum(ntile)
    n_tiles = tile_end[-1]
    ids = jnp.arange(_moe_max_tiles(t), dtype=jnp.int32)
    ids_c = jnp.minimum(ids, n_tiles - 1)
    tile_e = jnp.minimum(jnp.searchsorted(tile_end, ids_c, side='right'), N_EXPERTS - 1).astype(jnp.int32)
    local = ids_c - (tile_end - ntile)[tile_e]
    tile_b0 = (blk0[tile_e] + local * MOE_NBT).astype(jnp.int32)
    tile_nb = jnp.where(ids < n_tiles, jnp.clip(nblk[tile_e] - local * MOE_NBT, 0, MOE_NBT), 0).astype(jnp.int32)
    return row_tok, pos, tile_e, tile_b0, tile_nb, n_tiles.reshape(1).astype(jnp.int32)


def _rope_tables(seq):
    inv_freq = 1.0 / (ROPE_THETA ** (jnp.arange(0, HEAD_DIM, 2, dtype=F32) / HEAD_DIM))
    ang = jnp.arange(seq, dtype=F32)[:, None] * inv_freq[None, :]
    cos = jnp.cos(ang)
    sin = jnp.sin(ang)
    reps = LANES // HEAD_DIM
    cos_t = jnp.tile(jnp.concatenate([cos, cos], axis=-1), (1, reps))
    sin_t = jnp.tile(jnp.concatenate([-sin, sin], axis=-1), (1, reps))
    return cos_t, sin_t


def kernel(x, norm1_g, w_in, b_in, q_norm_g, k_norm_g, attn_sinks, w_attn_o, conv_dw_w, conv_dw_b,
           conv_ln_g, conv_ln_b, w_conv_o, w_out, norm2_g, w_router, b_router, w_gate_up, b_gate_up,
           w_down, b_down):
    b, s, d = x.shape
    t = b * s
    depth = norm1_g.shape[0]
    xt = x.reshape(t, d)
    cos_t, sin_t = _rope_tables(s)
    reps = LANES // HEAD_DIM
    c0 = QKV_WIDTH
    c1 = c0 + CONV_CH
    c2 = c1 + CONV_CH
    for l in range(depth):
        h = _rmsnorm(xt, norm1_g[l], BF16)
        qkv = _mm_bias(h, w_in, b_in, l, 0, c0, F32, tm=1024, tn=512, name="inproj_qkv")
        z = _mm_glu(h, w_in, b_in, l, c0, CONV_CH, tm=1024, tn=256)
        gates = _mm_bias(h, w_in, b_in, l, c2, 2 * d, BF16, tm=1024, tn=512, sigmoid=True,
                         name="inproj_gates")
        attn = _attention(qkv, attn_sinks[l], cos_t, sin_t,
                          jnp.tile(q_norm_g[l], reps).reshape(1, LANES),
                          jnp.tile(k_norm_g[l], reps).reshape(1, LANES), b, s)
        conv = _conformer_conv(z, conv_dw_w[l], conv_dw_b[l], conv_ln_g[l], conv_ln_b[l], b, s)
        merged = _merge(attn, conv, w_attn_o, w_conv_o, l, gates, tm=1024, tn=512)
        x1 = _mm_residual(merged, w_out, l, xt, tm=1024, tn=512)
        h2, idx_pad, gate_pad = _router(x1, norm2_g[l], w_router[l], b_router[l])
        row_tok, pos, tile_e, tile_b0, tile_nb, n_tiles = _routing(idx_pad[:, :TOP_K])
        xs = _gather_rows(h2, row_tok)
        ys = _moe_experts(xs, _moe_max_blocks(t) * MOE_BLOCK, tile_e, tile_b0, tile_nb, n_tiles,
                          w_gate_up[l], b_gate_up[l], w_down[l], b_down[l])
        xt = _combine(x1, gate_pad, ys, pos)
    return xt.reshape(b, s, d)
```

```python
import functools

import jax
import jax.numpy as jnp
from jax import lax
from jax.experimental import pallas as pl
from jax.experimental.pallas import tpu as pltpu

D_MODEL = 4096
HEAD_DIM = 64
N_Q_HEADS = 32
N_KV_HEADS = 4
WINDOW = 128
ATTN_BLOCK = 128
ROPE_THETA = 10000.0
CONV_CH = 2048
CONV_WIDTH = 31
N_EXPERTS = 32
TOP_K = 4
D_FF = 2048
SWIGLU_ALPHA = 1.702
SWIGLU_LIMIT = 7.0
MOE_BLOCK = 128
NORM_EPS = 1e-5

Q_WIDTH = N_Q_HEADS * HEAD_DIM
KV_WIDTH = N_KV_HEADS * HEAD_DIM
QKV_WIDTH = Q_WIDTH + 2 * KV_WIDTH

LANES = 128
SUBLANES = 8
VMEM_LIMIT = 56 * 1024 * 1024

BF16 = jnp.bfloat16
F32 = jnp.float32


def _cparams(sem, vmem=VMEM_LIMIT):
    return pltpu.CompilerParams(dimension_semantics=sem, vmem_limit_bytes=vmem)


def _rmsnorm_kernel(x_ref, g_ref, o_ref):
    x = x_ref[...]
    ms = jnp.mean(x * x, axis=-1, keepdims=True)
    o_ref[...] = (x * lax.rsqrt(ms + NORM_EPS) * g_ref[...]).astype(o_ref.dtype)


def _rmsnorm(x, g, out_dtype, tm=512):
    t, d = x.shape
    return pl.pallas_call(
        _rmsnorm_kernel,
        out_shape=jax.ShapeDtypeStruct((t, d), out_dtype),
        grid=(t // tm,),
        in_specs=[pl.BlockSpec((tm, d), lambda i: (i, 0)),
                  pl.BlockSpec((1, d), lambda i: (0, 0))],
        out_specs=pl.BlockSpec((tm, d), lambda i: (i, 0)),
        compiler_params=_cparams(("parallel",)),
        name="rmsnorm",
    )(x, g.reshape(1, d))


def _wspec(k, tn, layer, off=0):
    return pl.BlockSpec((None, k, tn), lambda j, i: (layer, 0, j + off))


def _cast_on_first_row_block(pairs):
    @pl.when(pl.program_id(1) == 0)
    def _():
        for w_ref, wb_ref in pairs:
            wb_ref[...] = w_ref[...].astype(BF16)


def _mm_bias_kernel(a_ref, w_ref, b_ref, o_ref, wb, *, sigmoid):
    _cast_on_first_row_block([(w_ref, wb)])
    acc = jnp.dot(a_ref[...], wb[...], preferred_element_type=F32) + b_ref[...]
    if sigmoid:
        acc = jax.nn.sigmoid(acc)
    o_ref[...] = acc.astype(o_ref.dtype)


def _mm_bias(a, w, b, layer, col0, n, out_dtype, *, tm, tn, sigmoid=False, name):
    m, k = a.shape
    off = col0 // tn
    return pl.pallas_call(
        functools.partial(_mm_bias_kernel, sigmoid=sigmoid),
        out_shape=jax.ShapeDtypeStruct((m, n), out_dtype),
        grid=(n // tn, m // tm),
        in_specs=[pl.BlockSpec((tm, k), lambda j, i: (i, 0)),
                  _wspec(k, tn, layer, off),
                  _wspec(1, tn, layer, off)],
        out_specs=pl.BlockSpec((tm, tn), lambda j, i: (i, j)),
        scratch_shapes=[pltpu.VMEM((k, tn), BF16)],
        compiler_params=_cparams(("arbitrary", "arbitrary")),
        name=name,
    )(a, w, b.reshape(b.shape[0], 1, b.shape[1]))


def _mm_glu_kernel(a_ref, wa_ref, wg_ref, ba_ref, bg_ref, o_ref, wab, wgb):
    _cast_on_first_row_block([(wa_ref, wab), (wg_ref, wgb)])
    a = a_ref[...]
    u = jnp.dot(a, wab[...], preferred_element_type=F32) + ba_ref[...]
    g = jnp.dot(a, wgb[...], preferred_element_type=F32) + bg_ref[...]
    o_ref[...] = (u * jax.nn.sigmoid(g)).astype(o_ref.dtype)


def _mm_glu(a, w, b, layer, col0, n, *, tm, tn):
    m, k = a.shape
    off_u = col0 // tn
    off_g = (col0 + n) // tn
    b3 = b.reshape(b.shape[0], 1, b.shape[1])
    return pl.pallas_call(
        _mm_glu_kernel,
        out_shape=jax.ShapeDtypeStruct((m, n), F32),
        grid=(n // tn, m // tm),
        in_specs=[pl.BlockSpec((tm, k), lambda j, i: (i, 0)),
                  _wspec(k, tn, layer, off_u), _wspec(k, tn, layer, off_g),
                  _wspec(1, tn, layer, off_u), _wspec(1, tn, layer, off_g)],
        out_specs=pl.BlockSpec((tm, tn), lambda j, i: (i, j)),
        scratch_shapes=[pltpu.VMEM((k, tn), BF16), pltpu.VMEM((k, tn), BF16)],
        compiler_params=_cparams(("arbitrary", "arbitrary")),
        name="inproj_glu",
    )(a, w, w, b3, b3)


def _merge_kernel(a_ref, c_ref, wa_ref, wc_ref, ga_ref, gc_ref, o_ref, wab, wcb):
    _cast_on_first_row_block([(wa_ref, wab), (wc_ref, wcb)])
    pa = jnp.dot(a_ref[...], wab[...], preferred_element_type=F32)
    pc = jnp.dot(c_ref[...], wcb[...], preferred_element_type=F32)
    o_ref[...] = (ga_ref[...].astype(F32) * pa + gc_ref[...].astype(F32) * pc).astype(o_ref.dtype)


def _merge(attn, conv, wa, wc, layer, gates, *, tm, tn):
    m, k = attn.shape
    n = wa.shape[2]
    nj = n // tn
    return pl.pallas_call(
        _merge_kernel,
        out_shape=jax.ShapeDtypeStruct((m, n), BF16),
        grid=(nj, m // tm),
        in_specs=[pl.BlockSpec((tm, k), lambda j, i: (i, 0)),
                  pl.BlockSpec((tm, k), lambda j, i: (i, 0)),
                  _wspec(k, tn, layer), _wspec(k, tn, layer),
                  pl.BlockSpec((tm, tn), lambda j, i: (i, j)),
                  pl.BlockSpec((tm, tn), lambda j, i: (i, j + nj))],
        out_specs=pl.BlockSpec((tm, tn), lambda j, i: (i, j)),
        scratch_shapes=[pltpu.VMEM((k, tn), BF16), pltpu.VMEM((k, tn), BF16)],
        compiler_params=_cparams(("arbitrary", "arbitrary")),
        name="merge",
    )(attn, conv, wa, wc, gates, gates)


def _mm_residual_kernel(a_ref, w_ref, x_ref, o_ref, wb):
    _cast_on_first_row_block([(w_ref, wb)])
    o_ref[...] = x_ref[...] + jnp.dot(a_ref[...], wb[...], preferred_element_type=F32)


def _mm_residual(a, w, layer, x, *, tm, tn):
    m, k = a.shape
    n = w.shape[2]
    return pl.pallas_call(
        _mm_residual_kernel,
        out_shape=jax.ShapeDtypeStruct((m, n), F32),
        grid=(n // tn, m // tm),
        in_specs=[pl.BlockSpec((tm, k), lambda j, i: (i, 0)),
                  _wspec(k, tn, layer),
                  pl.BlockSpec((tm, tn), lambda j, i: (i, j))],
        out_specs=pl.BlockSpec((tm, tn), lambda j, i: (i, j)),
        scratch_shapes=[pltpu.VMEM((k, tn), BF16)],
        compiler_params=_cparams(("arbitrary", "arbitrary")),
        name="outproj",
    )(a, w, x)


ATTN_RC = 32


def _attn_kernel(sink_ref, q_ref, k_ref, v_ref, cos_ref, sin_ref, qg_ref, kg_ref, o_ref,
                 kwin, vwin, kb, vb, qs, bias_scr, s_scr, p_scr, inv_scr):
    n = pl.program_id(1)
    blk = ATTN_BLOCK

    @pl.when(n == 0)
    def _():
        kwin[0:blk, :] = jnp.zeros((blk, KV_WIDTH), F32)
        vwin[0:blk, :] = jnp.zeros((blk, KV_WIDTH), F32)

    cos = cos_ref[...]
    sin = sin_ref[...]
    lane = lax.broadcasted_iota(jnp.int32, (blk, LANES), 1)
    first_half = (lane % HEAD_DIM) < (HEAD_DIM // 2)
    seg_r = lax.broadcasted_iota(jnp.int32, (LANES, LANES), 0) // HEAD_DIM
    seg_c = lax.broadcasted_iota(jnp.int32, (LANES, LANES), 1) // HEAD_DIM
    seg = jnp.where(seg_r == seg_c, 1.0, 0.0).astype(BF16)

    def norm_rope(t, g):
        t2 = t * t
        hi = t2.astype(BF16)
        lo = (t2 - hi.astype(F32)).astype(BF16)
        ss = (jnp.dot(hi, seg, preferred_element_type=F32)
              + jnp.dot(lo, seg, preferred_element_type=F32))
        tn = t * lax.rsqrt(ss * (1.0 / HEAD_DIM) + NORM_EPS) * g
        rot = jnp.where(first_half, pltpu.roll(tn, LANES - HEAD_DIM // 2, 1),
                        pltpu.roll(tn, HEAD_DIM // 2, 1))
        return tn * cos + rot * sin

    kg = kg_ref[...]
    qg = qg_ref[...]
    for c in range(KV_WIDTH // LANES):
        cols = slice(c * LANES, (c + 1) * LANES)
        kwin[blk:2 * blk, cols] = norm_rope(k_ref[:, cols], kg)
    vwin[blk:2 * blk, :] = v_ref[...]

    lane2 = lax.broadcasted_iota(jnp.int32, (2 * blk, LANES), 1)
    lo_half = lane2 < HEAD_DIM
    for c in range(KV_WIDTH // LANES):
        cols = slice(c * LANES, (c + 1) * LANES)
        for win, dst in ((kwin, kb), (vwin, vb)):
            x = win[:, cols]
            xs = pltpu.roll(x, HEAD_DIM, 1)
            dst[2 * c, 0:2 * blk, :] = jnp.where(lo_half, x, 0.0).astype(BF16)
            dst[2 * c, 2 * blk:4 * blk, :] = jnp.where(lo_half, 0.0, xs).astype(BF16)
            dst[2 * c + 1, 0:2 * blk, :] = jnp.where(lo_half, xs, 0.0).astype(BF16)
            dst[2 * c + 1, 2 * blk:4 * blk, :] = jnp.where(lo_half, 0.0, x).astype(BF16)

    qi = lax.broadcasted_iota(jnp.int32, (blk, 4 * blk), 0)
    kj = lax.broadcasted_iota(jnp.int32, (blk, 4 * blk), 1) % (2 * blk)
    rel = qi + blk - kj
    first_key = jnp.where(n == 0, blk, 0)
    mask = (rel >= 0) & (rel < WINDOW) & (kj >= first_key)
    bias_scr[...] = jnp.where(mask, 0.0, -jnp.inf)

    pairs_per_kv = N_Q_HEADS // N_KV_HEADS // 2
    for p in range(N_Q_HEADS // 2):
        g, pp = divmod(p, pairs_per_kv)
        q2 = norm_rope(q_ref[:, p * LANES:(p + 1) * LANES], qg) * (HEAD_DIM ** -0.5)
        qs[g, pp * blk:(pp + 1) * blk, :] = q2.astype(BF16)

    lo_out = lax.broadcasted_iota(jnp.int32, (ATTN_RC, LANES), 1) < HEAD_DIM
    for g in range(N_KV_HEADS):
        s_scr[g] = lax.dot_general(qs[g], kb[g], (((1,), (1,)), ((), ())), preferred_element_type=F32)
        for r in range(pairs_per_kv * blk // ATTN_RC):
            rows = slice(r * ATTN_RC, (r + 1) * ATTN_RC)
            pp, q0 = divmod(r * ATTN_RC, blk)
            s = s_scr[g, rows, :] + bias_scr[q0:q0 + ATTN_RC, :]
            invs = []
            for h in range(2):
                keys = slice(h * 2 * blk, (h + 1) * 2 * blk)
                sh = s[:, keys]
                sink = sink_ref[2 * (g * pairs_per_kv + pp) + h]
                m = jnp.maximum(jnp.max(sh, axis=-1, keepdims=True), sink)
                e = jnp.exp(sh - m)
                den = jnp.sum(e, axis=-1, keepdims=True) + jnp.exp(sink - m)
                p_scr[g, rows, keys] = e.astype(BF16)
                invs.append(1.0 / den)
            inv_scr[g, rows, :] = jnp.where(lo_out, invs[0], invs[1])
        o = jnp.dot(p_scr[g], vb[g], preferred_element_type=F32) * inv_scr[g]
        for pp in range(pairs_per_kv):
            p = g * pairs_per_kv + pp
            o_ref[:, p * LANES:(p + 1) * LANES] = o[pp * blk:(pp + 1) * blk, :].astype(o_ref.dtype)

    kwin[0:blk, :] = kwin[blk:2 * blk, :]
    vwin[0:blk, :] = vwin[blk:2 * blk, :]


def _attention(qkv, sinks, cos_t, sin_t, qg, kg, batch, seq):
    t = qkv.shape[0]
    blk = ATTN_BLOCK
    nb = seq // blk
    kcol = Q_WIDTH // KV_WIDTH
    return pl.pallas_call(
        _attn_kernel,
        out_shape=jax.ShapeDtypeStruct((t, Q_WIDTH), BF16),
        grid=(batch, nb),
        in_specs=[pl.BlockSpec(memory_space=pltpu.SMEM),
                  pl.BlockSpec((blk, Q_WIDTH), lambda b, n: (b * nb + n, 0)),
                  pl.BlockSpec((blk, KV_WIDTH), lambda b, n: (b * nb + n, kcol)),
                  pl.BlockSpec((blk, KV_WIDTH), lambda b, n: (b * nb + n, kcol + 1)),
                  pl.BlockSpec((blk, LANES), lambda b, n: (n, 0)),
                  pl.BlockSpec((blk, LANES), lambda b, n: (n, 0)),
                  pl.BlockSpec((1, LANES), lambda b, n: (0, 0)),
                  pl.BlockSpec((1, LANES), lambda b, n: (0, 0))],
        out_specs=pl.BlockSpec((blk, Q_WIDTH), lambda b, n: (b * nb + n, 0)),
        scratch_shapes=[pltpu.VMEM((2 * blk, KV_WIDTH), F32),
                        pltpu.VMEM((2 * blk, KV_WIDTH), F32),
                        pltpu.VMEM((N_KV_HEADS, 4 * blk, LANES), BF16),
                        pltpu.VMEM((N_KV_HEADS, 4 * blk, LANES), BF16),
                        pltpu.VMEM((N_KV_HEADS, 4 * blk, LANES), BF16),
                        pltpu.VMEM((blk, 4 * blk), F32),
                        pltpu.VMEM((N_KV_HEADS, 4 * blk, 4 * blk), F32),
                        pltpu.VMEM((N_KV_HEADS, 4 * blk, 4 * blk), BF16),
                        pltpu.VMEM((N_KV_HEADS, 4 * blk, LANES), F32)],
        compiler_params=_cparams(("arbitrary", "arbitrary")),
        name="swa_attention",
    )(sinks, qkv, qkv, qkv, cos_t, sin_t, qg, kg)


CONV_TS = 256
CONV_HALO = 32
CONV_RB = 128


def _conv_kernel(z_ref, w_ref, b_ref, lg_ref, lb_ref, o_ref, zbuf, cbuf):
    n = pl.program_id(1)

    @pl.when(n == 0)
    def _():
        zbuf[0:CONV_HALO, :] = jnp.zeros((CONV_HALO, CONV_CH), F32)

    zbuf[CONV_HALO:CONV_HALO + CONV_TS, :] = z_ref[...]
    shift = CONV_HALO - (CONV_WIDTH - 1)

    def chunk(c, carry):
        c0 = pl.multiple_of(c * LANES, LANES)
        for rb in range(CONV_TS // CONV_RB):
            base = rb * CONV_RB
            acc = jnp.broadcast_to(b_ref[:, pl.ds(c0, LANES)], (CONV_RB, LANES))
            for res in range(SUBLANES):
                extra = SUBLANES if res else 0
                part = None
                for off in range(shift, shift + CONV_WIDTH):
                    if off % SUBLANES != res:
                        continue
                    r0 = base + off - res
                    term = (zbuf[r0:r0 + CONV_RB + extra, pl.ds(c0, LANES)]
                            * w_ref[off - shift:off - shift + 1, pl.ds(c0, LANES)])
                    part = term if part is None else part + term
                acc = acc + part[res:res + CONV_RB]
            cbuf[base:base + CONV_RB, pl.ds(c0, LANES)] = acc
        return carry

    lax.fori_loop(0, CONV_CH // LANES, chunk, 0)

    zbuf[0:CONV_HALO, :] = zbuf[CONV_TS:CONV_TS + CONV_HALO, :]

    y = cbuf[...]
    mu = jnp.mean(y, axis=-1, keepdims=True)
    yc = y - mu
    var = jnp.mean(yc * yc, axis=-1, keepdims=True)
    yn = yc * lax.rsqrt(var + NORM_EPS) * lg_ref[...] + lb_ref[...]
    o_ref[...] = (yn * jax.nn.sigmoid(yn)).astype(o_ref.dtype)


def _conformer_conv(z, dw_w, dw_b, ln_g, ln_b, batch, seq):
    t = z.shape[0]
    ns = seq // CONV_TS
    vec = lambda a: a.reshape(1, CONV_CH)
    return pl.pallas_call(
        _conv_kernel,
        out_shape=jax.ShapeDtypeStruct((t, CONV_CH), BF16),
        grid=(batch, ns),
        in_specs=[pl.BlockSpec((CONV_TS, CONV_CH), lambda b, n: (b * ns + n, 0)),
                  pl.BlockSpec((CONV_WIDTH, CONV_CH), lambda b, n: (0, 0)),
                  pl.BlockSpec((1, CONV_CH), lambda b, n: (0, 0)),
                  pl.BlockSpec((1, CONV_CH), lambda b, n: (0, 0)),
                  pl.BlockSpec((1, CONV_CH), lambda b, n: (0, 0))],
        out_specs=pl.BlockSpec((CONV_TS, CONV_CH), lambda b, n: (b * ns + n, 0)),
        scratch_shapes=[pltpu.VMEM((CONV_HALO + CONV_TS, CONV_CH), F32),
                        pltpu.VMEM((CONV_TS, CONV_CH), F32)],
        compiler_params=_cparams(("arbitrary", "arbitrary")),
        name="conformer_conv",
    )(z, dw_w, vec(dw_b), vec(ln_g), vec(ln_b))


ROUTER_TM = 256
HI16 = 0xFFFF0000


def _pack_halves(xb):
    n = xb.shape[1] // 2
    lo = lax.bitcast_convert_type(xb[:, :n].astype(F32), jnp.uint32)
    hi = lax.bitcast_convert_type(xb[:, n:].astype(F32), jnp.uint32)
    return (lo >> 16) | (hi & jnp.uint32(HI16))


def _unpack_halves(w):
    lo = lax.bitcast_convert_type(w << 16, F32).astype(BF16)
    hi = lax.bitcast_convert_type(w & jnp.uint32(HI16), F32).astype(BF16)
    return lo, hi


def _router_kernel(x_ref, g_ref, wr_ref, br_ref, h_ref, idx_ref, gate_ref):
    x = x_ref[...]
    ms = jnp.mean(x * x, axis=-1, keepdims=True)
    hb = (x * lax.rsqrt(ms + NORM_EPS) * g_ref[...]).astype(BF16)
    h_ref[...] = _pack_halves(hb)
    vals = jnp.dot(hb, wr_ref[...], preferred_element_type=F32) + br_ref[...]
    lane = lax.broadcasted_iota(jnp.int32, vals.shape, 1).astype(F32)
    tops, idxs = [], []
    for _ in range(TOP_K):
        m = jnp.max(vals, axis=-1, keepdims=True)
        idx = jnp.min(jnp.where(vals == m, lane, float(LANES)), axis=-1, keepdims=True)
        tops.append(m)
        idxs.append(idx)
        vals = jnp.where(lane == idx, -jnp.inf, vals)
    es = [jnp.exp(v - tops[0]) for v in tops]
    den = es[0] + es[1] + es[2] + es[3]
    idx_out = jnp.zeros(vals.shape, F32)
    gate_out = jnp.zeros(vals.shape, F32)
    for k in range(TOP_K):
        idx_out = jnp.where(lane == float(k), idxs[k], idx_out)
        gate_out = jnp.where(lane == float(k), es[k] / den, gate_out)
    idx_ref[...] = idx_out.astype(jnp.int32)
    gate_ref[...] = gate_out


def _router(x1, g, w_router, b_router):
    t, d = x1.shape
    tm = ROUTER_TM
    wr = jnp.zeros((d, LANES), BF16).at[:, :N_EXPERTS].set(w_router.astype(BF16))
    br = jnp.full((1, LANES), -1e30, F32).at[0, :N_EXPERTS].set(b_router)
    return pl.pallas_call(
        _router_kernel,
        out_shape=(jax.ShapeDtypeStruct((t, d // 2), jnp.uint32),
                   jax.ShapeDtypeStruct((t, LANES), jnp.int32),
                   jax.ShapeDtypeStruct((t, LANES), F32)),
        grid=(t // tm,),
        in_specs=[pl.BlockSpec((tm, d), lambda i: (i, 0)),
                  pl.BlockSpec((1, d), lambda i: (0, 0)),
                  pl.BlockSpec((d, LANES), lambda i: (0, 0)),
                  pl.BlockSpec((1, LANES), lambda i: (0, 0))],
        out_specs=(pl.BlockSpec((tm, d // 2), lambda i: (i, 0)),
                   pl.BlockSpec((tm, LANES), lambda i: (i, 0)),
                   pl.BlockSpec((tm, LANES), lambda i: (i, 0))),
        compiler_params=_cparams(("parallel",)),
        name="router",
    )(x1, g.reshape(1, d), wr, br)


GATHER_ROWS = 1024
GATHER_UNROLL = 8
GATHER_CHUNK = 128


def _gather_kernel(idx_ref, nxt_ref, src_hbm, o_ref, buf, sem):
    i = pl.program_id(0)
    nsteps = pl.num_programs(0)
    slot = lax.rem(i, 2)

    def issue(ids, s):
        def body(q, carry):
            for u in range(GATHER_UNROLL):
                r = q * GATHER_UNROLL + u
                pltpu.make_async_copy(src_hbm.at[pl.ds(ids[0, 0, r], 1), :],
                                      buf.at[s, pl.ds(r, 1), :], sem.at[s]).start(priority=u % 2)
            return carry
        lax.fori_loop(0, GATHER_ROWS // GATHER_UNROLL, body, 0)

    @pl.when(i == 0)
    def _():
        issue(idx_ref, 0)

    @pl.when(i + 1 < nsteps)
    def _():
        issue(nxt_ref, 1 - slot)

    pltpu.make_async_copy(src_hbm.at[pl.ds(0, GATHER_ROWS), :], buf.at[slot], sem.at[slot]).wait()
    half = buf.shape[2]
    for c in range(GATHER_ROWS // GATHER_CHUNK):
        rows = slice(c * GATHER_CHUNK, (c + 1) * GATHER_CHUNK)
        lo, hi = _unpack_halves(buf[slot, rows, :])
        o_ref[rows, 0:half] = lo
        o_ref[rows, half:2 * half] = hi


def _gather_rows(src, row_idx):
    n = row_idx.shape[0]
    d = src.shape[1]
    nsteps = n // GATHER_ROWS
    idx3 = row_idx.reshape(nsteps, 1, GATHER_ROWS)
    return pl.pallas_call(
        _gather_kernel,
        out_shape=jax.ShapeDtypeStruct((n, 2 * d), BF16),
        grid=(nsteps,),
        in_specs=[pl.BlockSpec((1, 1, GATHER_ROWS), lambda i: (i, 0, 0), memory_space=pltpu.SMEM),
                  pl.BlockSpec((1, 1, GATHER_ROWS), lambda i: (jnp.minimum(i + 1, nsteps - 1), 0, 0),
                               memory_space=pltpu.SMEM),
                  pl.BlockSpec(memory_space=pl.ANY)],
        out_specs=pl.BlockSpec((GATHER_ROWS, 2 * d), lambda i: (i, 0)),
        scratch_shapes=[pltpu.VMEM((2, GATHER_ROWS, d), src.dtype),
                        pltpu.SemaphoreType.DMA((2,))],
        compiler_params=_cparams(("arbitrary",)),
        name="dispatch_gather",
    )(idx3, idx3, src)


MOE_NBT = 10
MOE_TM = MOE_NBT * MOE_BLOCK
MOE_FAST = (7, 8, 9)
MOE_PIECE = 4
MOE_FC = 256
MOE_NC = 4 * MOE_FC
MOE_S1 = D_FF // MOE_FC
MOE_S2 = D_MODEL // MOE_NC
MOE_STEPS = MOE_S1 + MOE_S2
assert MOE_STEPS % 2 == 0 and D_MODEL == 2 * D_FF


def _moe_max_blocks(t):
    return t * TOP_K // MOE_BLOCK + N_EXPERTS


def _moe_max_tiles(t):
    return (_moe_max_blocks(t) + N_EXPERTS * (MOE_NBT - 1)) // MOE_NBT + 1


def _moe_kernel(te_ref, tb_ref, tn_ref, nt_ref,
                xs_hbm, wgu_hbm, bgu_hbm, wd_hbm, bd_hbm, ys_hbm,
                xbuf, act, obuf, ring, bgu, bdn, xsem, osem, wsem, bsem):
    i = pl.program_id(0)
    ntiles = nt_ref[0]
    half = MOE_NC // 2

    def weight_copies(tile, s, slot):
        e = te_ref[tile]
        c1 = pl.multiple_of(jnp.minimum(s, MOE_S1 - 1) * MOE_FC, MOE_FC)
        c2 = pl.multiple_of(jnp.maximum(s - MOE_S1, 0) * MOE_NC, MOE_NC)
        gate_up = [pltpu.make_async_copy(wgu_hbm.at[e, :, pl.ds(g * D_FF + c1, MOE_FC)],
                                         ring.at[slot, :, pl.ds(g * MOE_FC, MOE_FC)], wsem.at[slot])
                   for g in range(2)]
        down = [pltpu.make_async_copy(wd_hbm.at[e, :, pl.ds(c2 + h * half, half)],
                                      ring.at[slot, pl.ds(h * D_FF, D_FF), :], wsem.at[slot])
                for h in range(2)]
        return gate_up, down

    def weights(tile, s, slot, op):
        s = jnp.asarray(s, jnp.int32)
        gate_up, down = weight_copies(tile, s, slot)

        @pl.when(s < MOE_S1)
        def _():
            for c in gate_up:
                op(c)

        @pl.when(s >= MOE_S1)
        def _():
            for c in down:
                op(c)

    def bias_copies(tile):
        e = te_ref[tile]
        ts = lax.rem(tile, 2)
        return [pltpu.make_async_copy(bgu_hbm.at[e], bgu.at[ts], bsem.at[ts]),
                pltpu.make_async_copy(bd_hbm.at[e], bdn.at[ts], bsem.at[ts])]

    start = lambda c: c.start()
    wait = lambda c: c.wait()

    def xs_copy(tile, row0, rows):
        src0 = pl.multiple_of(tb_ref[tile] * MOE_BLOCK + row0, MOE_BLOCK)
        return pltpu.make_async_copy(xs_hbm.at[pl.ds(src0, rows), :],
                                     xbuf.at[pl.ds(row0, rows), :], xsem.at[0])

    def for_row_pieces(nb, emit):
        general = nb >= 0
        for f in MOE_FAST:
            general = general & (nb != f)

            @pl.when(nb == f)
            def _():
                emit(0, f * MOE_BLOCK)

        @pl.when(general)
        def _():
            nbig = nb // MOE_PIECE
            big = MOE_PIECE * MOE_BLOCK

            def big_piece(q, carry):
                emit(pl.multiple_of(q * big, big), big)
                return carry

            def small_piece(b, carry):
                emit(pl.multiple_of(b * MOE_BLOCK, MOE_BLOCK), MOE_BLOCK)
                return carry

            lax.fori_loop(0, nbig, big_piece, 0)
            lax.fori_loop(nbig * MOE_PIECE, nb, small_piece, 0)

    def out_copy(slot, row0, rows, n):
        dst0 = pl.multiple_of(tb_ref[i] * MOE_BLOCK + row0, MOE_BLOCK)
        col0 = pl.multiple_of(n * MOE_NC, MOE_NC)
        return pltpu.make_async_copy(obuf.at[slot, pl.ds(row0, rows), :],
                                     ys_hbm.at[pl.ds(dst0, rows), pl.ds(col0, MOE_NC)],
                                     osem.at[slot])

    @pl.when(i < ntiles)
    def _():
        nb = tn_ref[i]
        tslot = lax.rem(i, 2)

        @pl.when(i == 0)
        def _():
            for_row_pieces(nb, lambda row0, rows: xs_copy(i, row0, rows).start())
            weights(i, 0, 0, start)
            for c in bias_copies(i):
                c.start()

        for_row_pieces(nb, lambda row0, rows: xs_copy(i, row0, rows).wait())
        for c in bias_copies(i):
            c.wait()

        def step(s, carry):
            slot = lax.rem(s, 2)
            weights(i, s, slot, wait)

            @pl.when(s + 1 < MOE_STEPS)
            def _():
                weights(i, s + 1, 1 - slot, start)

            @pl.when((s + 1 == MOE_STEPS) & (i + 1 < ntiles))
            def _():
                weights(i + 1, 0, 1 - slot, start)
                for c in bias_copies(i + 1):
                    c.start()

            @pl.when(s < MOE_S1)
            def _():
                c0 = pl.multiple_of(s * MOE_FC, MOE_FC)

                def gate_up(row0, rows):
                    x = xbuf[pl.ds(row0, rows), :]
                    wg = ring[slot, :, 0:MOE_FC].astype(BF16)
                    wl = ring[slot, :, MOE_FC:2 * MOE_FC].astype(BF16)
                    glu = jnp.dot(x, wg, preferred_element_type=F32) + bgu[tslot, :, pl.ds(c0, MOE_FC)]
                    lin = jnp.dot(x, wl, preferred_element_type=F32) + bgu[tslot, :, pl.ds(D_FF + c0, MOE_FC)]
                    glu = jnp.minimum(glu, SWIGLU_LIMIT)
                    lin = jnp.clip(lin, -SWIGLU_LIMIT, SWIGLU_LIMIT)
                    a = glu * jax.nn.sigmoid(SWIGLU_ALPHA * glu) * (lin + 1.0)
                    act[pl.ds(row0, rows), pl.ds(c0, MOE_FC)] = a.astype(BF16)

                for_row_pieces(nb, gate_up)

            @pl.when(s >= MOE_S1)
            def _():
                n = s - MOE_S1
                oslot = lax.rem(n, 2)

                @pl.when((s == MOE_S1) & (i + 1 < ntiles))
                def _():
                    for_row_pieces(tn_ref[i + 1], lambda row0, rows: xs_copy(i + 1, row0, rows).start())

                def down(row0, rows):
                    a = act[pl.ds(row0, rows), :]
                    for h in range(2):
                        wd = ring[slot, h * D_FF:(h + 1) * D_FF, :].astype(BF16)
                        bd = bdn[tslot, :, pl.ds(pl.multiple_of(n * MOE_NC + h * half, half), half)]
                        obuf[oslot, pl.ds(row0, rows), h * half:(h + 1) * half] = (
                            jnp.dot(a, wd, preferred_element_type=F32) + bd)
                    out_copy(oslot, row0, rows, n).start()

                for_row_pieces(nb, down)

                @pl.when(n > 0)
                def _():
                    for_row_pieces(nb, lambda row0, rows: out_copy(1 - oslot, row0, rows, n - 1).wait())

                @pl.when(n == MOE_S2 - 1)
                def _():
                    for_row_pieces(nb, lambda row0, rows: out_copy(oslot, row0, rows, n).wait())

            return carry

        lax.fori_loop(0, MOE_STEPS, step, 0)

    @pl.when(i == ntiles)
    def _():
        obuf[0, 0:MOE_BLOCK, :] = jnp.zeros((MOE_BLOCK, MOE_NC), F32)

        def fill(blk, carry):
            row0 = pl.multiple_of(blk * MOE_BLOCK, MOE_BLOCK)
            fills = [pltpu.make_async_copy(
                obuf.at[0, pl.ds(0, MOE_BLOCK), :],
                ys_hbm.at[pl.ds(row0, MOE_BLOCK), pl.ds(c * MOE_NC, MOE_NC)], osem.at[0])
                for c in range(MOE_S2)]
            for f in fills:
                f.start()
            for f in fills:
                f.wait()
            return carry

        lax.fori_loop(tb_ref[ntiles - 1] + tn_ref[ntiles - 1], ys_hbm.shape[0] // MOE_BLOCK, fill, 0)


def _moe_experts(xs, n_rows, tile_e, tile_b0, tile_nb, n_tiles, w_gate_up, b_gate_up, w_down, b_down):
    grid_spec = pltpu.PrefetchScalarGridSpec(
        num_scalar_prefetch=4,
        grid=(tile_e.shape[0],),
        in_specs=[pl.BlockSpec(memory_space=pl.ANY)] * 5,
        out_specs=pl.BlockSpec(memory_space=pl.ANY),
        scratch_shapes=[pltpu.VMEM((MOE_TM, D_MODEL), BF16),
                        pltpu.VMEM((MOE_TM, D_FF), BF16),
                        pltpu.VMEM((2, MOE_TM, MOE_NC), F32),
                        pltpu.VMEM((2, D_MODEL, 2 * MOE_FC), F32),
                        pltpu.VMEM((2, 1, 2 * D_FF), F32),
                        pltpu.VMEM((2, 1, D_MODEL), F32),
                        pltpu.SemaphoreType.DMA((1,)),
                        pltpu.SemaphoreType.DMA((2,)),
                        pltpu.SemaphoreType.DMA((2,)),
                        pltpu.SemaphoreType.DMA((2,))],
    )
    return pl.pallas_call(
        _moe_kernel,
        out_shape=jax.ShapeDtypeStruct((n_rows, D_MODEL), F32),
        grid_spec=grid_spec,
        compiler_params=_cparams(("arbitrary",)),
        name="moe_experts",
    )(tile_e, tile_b0, tile_nb, n_tiles, xs, w_gate_up, b_gate_up.reshape(N_EXPERTS, 1, 2 * D_FF),
      w_down, b_down.reshape(N_EXPERTS, 1, D_MODEL))


COMBINE_TT = 128


def _combine_kernel(pos_ref, nxt_ref, x_ref, gate_ref, ys_hbm, o_ref, buf, sem):
    i = pl.program_id(0)
    nsteps = pl.num_programs(0)
    slot = lax.rem(i, 2)

    def issue(ids, s):
        def body(r, carry):
            for k in range(TOP_K):
                pltpu.make_async_copy(ys_hbm.at[pl.ds(ids[0, 0, r * TOP_K + k], 1), :],
                                      buf.at[s, k, pl.ds(r, 1), :], sem.at[s]).start(priority=k % 2)
            return carry
        lax.fori_loop(0, COMBINE_TT, body, 0, unroll=4)

    @pl.when(i == 0)
    def _():
        issue(pos_ref, 0)

    @pl.when(i + 1 < nsteps)
    def _():
        issue(nxt_ref, 1 - slot)

    for k in range(TOP_K):
        pltpu.make_async_copy(ys_hbm.at[pl.ds(0, COMBINE_TT), :], buf.at[slot, k], sem.at[slot]).wait()
    gate = gate_ref[...]
    acc = x_ref[...]
    moe = gate[:, 0:1] * buf[slot, 0]
    for k in range(1, TOP_K):
        moe = moe + gate[:, k:k + 1] * buf[slot, k]
    o_ref[...] = acc + moe


def _combine(x1, gates, ys, pos):
    t, d = x1.shape
    tt = COMBINE_TT
    nsteps = t // tt
    pos3 = pos.reshape(nsteps, 1, tt * TOP_K)
    return pl.pallas_call(
        _combine_kernel,
        out_shape=jax.ShapeDtypeStruct((t, d), F32),
        grid=(nsteps,),
        in_specs=[pl.BlockSpec((1, 1, tt * TOP_K), lambda i: (i, 0, 0), memory_space=pltpu.SMEM),
                  pl.BlockSpec((1, 1, tt * TOP_K), lambda i: (jnp.minimum(i + 1, nsteps - 1), 0, 0),
                               memory_space=pltpu.SMEM),
                  pl.BlockSpec((tt, d), lambda i: (i, 0)),
                  pl.BlockSpec((tt, LANES), lambda i: (i, 0)),
                  pl.BlockSpec(memory_space=pl.ANY)],
        out_specs=pl.BlockSpec((tt, d), lambda i: (i, 0)),
        scratch_shapes=[pltpu.VMEM((2, TOP_K, tt, d), F32),
                        pltpu.SemaphoreType.DMA((2,))],
        compiler_params=_cparams(("arbitrary",)),
        name="combine",
    )(pos3, pos3, x1, gates, ys)


def _routing(top_idx):
    t = top_idx.shape[0]
    n_assign = t * TOP_K
    flat_e = top_idx.reshape(-1)
    experts = jnp.arange(N_EXPERTS, dtype=jnp.int32)
    order = jnp.argsort(flat_e)
    sorted_pos = jnp.argsort(order).astype(jnp.int32)
    counts = jnp.sum((flat_e[:, None] == experts[None, :]).astype(jnp.int32), axis=0)
    starts = jnp.cumsum(counts) - counts
    padded = (counts + MOE_BLOCK - 1) // MOE_BLOCK * MOE_BLOCK
    pad_ends = jnp.cumsum(padded)
    pad_starts = pad_ends - padded
    shift = pad_starts - starts
    pos = sorted_pos + shift[flat_e]
    n_rows = n_assign + N_EXPERTS * MOE_BLOCK
    n_rows = (n_rows + GATHER_ROWS - 1) // GATHER_ROWS * GATHER_ROWS
    blocks = jnp.arange(n_rows // MOE_BLOCK, dtype=jnp.int32)
    blk_e = jnp.minimum(jnp.sum((blocks[:, None] * MOE_BLOCK >= pad_ends[None, :]).astype(jnp.int32), axis=1),
                        N_EXPERTS - 1)
    rows = jnp.arange(n_rows, dtype=jnp.int32).reshape(-1, MOE_BLOCK)
    place = rows - shift[blk_e][:, None]
    valid = (rows - pad_starts[blk_e][:, None]) < counts[blk_e][:, None]
    row_tok = jnp.where(valid, order[jnp.clip(place, 0, n_assign - 1)].astype(jnp.int32) // TOP_K, 0)
    row_tok = row_tok.reshape(-1)

    nblk = padded // MOE_BLOCK
    blk0 = pad_starts // MOE_BLOCK
    ntile = (nblk + MOE_NBT - 1) // MOE_NBT
    tile_end = jnp.cumsum(ntile)
    n_tiles = tile_end[-1]
    ids = jnp.arange(_moe_max_tiles(t), dtype=jnp.int32)
    ids_c = jnp.minimum(ids, n_tiles - 1)
    tile_e = jnp.minimum(jnp.searchsorted(tile_end, ids_c, side='right'), N_EXPERTS - 1).astype(jnp.int32)
    local = ids_c - (tile_end - ntile)[tile_e]
    tile_b0 = (blk0[tile_e] + local * MOE_NBT).astype(jnp.int32)
    tile_nb = jnp.where(ids < n_tiles, jnp.clip(nblk[tile_e] - local * MOE_NBT, 0, MOE_NBT), 0).astype(jnp.int32)
    return row_tok, pos, tile_e, tile_b0, tile_nb, n_tiles.reshape(1).astype(jnp.int32)


def _rope_tables(seq):
    inv_freq = 1.0 / (ROPE_THETA ** (jnp.arange(0, HEAD_DIM, 2, dtype=F32) / HEAD_DIM))
    ang = jnp.arange(seq, dtype=F32)[:, None] * inv_freq[None, :]
    cos = jnp.cos(ang)
    sin = jnp.sin(ang)
    reps = LANES // HEAD_DIM
    cos_t = jnp.tile(jnp.concatenate([cos, cos], axis=-1), (1, reps))
    sin_t = jnp.tile(jnp.concatenate([-sin, sin], axis=-1), (1, reps))
    return cos_t, sin_t


def kernel(x, norm1_g, w_in, b_in, q_norm_g, k_norm_g, attn_sinks, w_attn_o, conv_dw_w, conv_dw_b,
           conv_ln_g, conv_ln_b, w_conv_o, w_out, norm2_g, w_router, b_router, w_gate_up, b_gate_up,
           w_down, b_down):
    b, s, d = x.shape
    t = b * s
    depth = norm1_g.shape[0]
    xt = x.reshape(t, d)
    cos_t, sin_t = _rope_tables(s)
    reps = LANES // HEAD_DIM
    c0 = QKV_WIDTH
    c1 = c0 + CONV_CH
    c2 = c1 + CONV_CH
    for l in range(depth):
        h = _rmsnorm(xt, norm1_g[l], BF16)
        qkv = _mm_bias(h, w_in, b_in, l, 0, c0, F32, tm=1024, tn=512, name="inproj_qkv")
        z = _mm_glu(h, w_in, b_in, l, c0, CONV_CH, tm=1024, tn=256)
        gates = _mm_bias(h, w_in, b_in, l, c2, 2 * d, BF16, tm=1024, tn=512, sigmoid=True,
                         name="inproj_gates")
        attn = _attention(qkv, attn_sinks[l], cos_t, sin_t,
                          jnp.tile(q_norm_g[l], reps).reshape(1, LANES),
                          jnp.tile(k_norm_g[l], reps).reshape(1, LANES), b, s)
        conv = _conformer_conv(z, conv_dw_w[l], conv_dw_b[l], conv_ln_g[l], conv_ln_b[l], b, s)
        merged = _merge(attn, conv, w_attn_o, w_conv_o, l, gates, tm=1024, tn=512)
        x1 = _mm_residual(merged, w_out, l, xt, tm=1024, tn=512)
        h2, idx_pad, gate_pad = _router(x1, norm2_g[l], w_router[l], b_router[l])
        row_tok, pos, tile_e, tile_b0, tile_nb, n_tiles = _routing(idx_pad[:, :TOP_K])
        xs = _gather_rows(h2, row_tok)
        ys = _moe_experts(xs, _moe_max_blocks(t) * MOE_BLOCK, tile_e, tile_b0, tile_nb, n_tiles,
                          w_gate_up[l], b_gate_up[l], w_down[l], b_down[l])
        xt = _combine(x1, gate_pad, ys, pos)
    return xt.reshape(b, s, d)
```

```python
import functools

import jax
import jax.numpy as jnp
import numpy as np
from jax import lax
from jax.experimental import pallas as pl
from jax.experimental.pallas import tpu as pltpu

D_MODEL = 4096
HEAD_DIM = 64
N_Q_HEADS = 32
N_KV_HEADS = 4
WINDOW = 128
ATTN_BLOCK = 128
ROPE_THETA = 10000.0
CONV_CH = 2048
CONV_WIDTH = 31
N_EXPERTS = 32
TOP_K = 4
D_FF = 2048
SWIGLU_ALPHA = 1.702
SWIGLU_LIMIT = 7.0
MOE_BLOCK = 128
NORM_EPS = 1e-5

Q_WIDTH = N_Q_HEADS * HEAD_DIM
KV_WIDTH = N_KV_HEADS * HEAD_DIM
QKV_WIDTH = Q_WIDTH + 2 * KV_WIDTH

LANES = 128
SUBLANES = 8
VMEM_LIMIT = 56 * 1024 * 1024

BF16 = jnp.bfloat16
F32 = jnp.float32


def _cparams(sem, vmem=VMEM_LIMIT):
    return pltpu.CompilerParams(dimension_semantics=sem, vmem_limit_bytes=vmem)


def _rmsnorm_kernel(x_ref, g_ref, o_ref):
    x = x_ref[...]
    ms = jnp.mean(x * x, axis=-1, keepdims=True)
    o_ref[...] = (x * lax.rsqrt(ms + NORM_EPS) * g_ref[...]).astype(o_ref.dtype)


def _rmsnorm(x, g, out_dtype, tm=512):
    t, d = x.shape
    return pl.pallas_call(
        _rmsnorm_kernel,
        out_shape=jax.ShapeDtypeStruct((t, d), out_dtype),
        grid=(t // tm,),
        in_specs=[pl.BlockSpec((tm, d), lambda i: (i, 0)),
                  pl.BlockSpec((1, d), lambda i: (0, 0))],
        out_specs=pl.BlockSpec((tm, d), lambda i: (i, 0)),
        compiler_params=_cparams(("parallel",)),
        name="rmsnorm",
    )(x, g.reshape(1, d))


def _wspec(k, tn, layer, off=0):
    return pl.BlockSpec((None, k, tn), lambda j, i: (layer, 0, j + off))


def _cast_on_first_row_block(pairs):
    @pl.when(pl.program_id(1) == 0)
    def _():
        for w_ref, wb_ref in pairs:
            wb_ref[...] = w_ref[...].astype(BF16)


def _stage_weights(w_hbm, layer, cols, tn, stage, wbs, sem):
    j = pl.program_id(0)
    nj = pl.num_programs(0)

    def copies(jj):
        return [pltpu.make_async_copy(w_hbm.at[layer, :, pl.ds(pl.multiple_of(c + jj * tn, LANES), tn)],
                                      stage.at[q], sem.at[q]) for q, c in enumerate(cols)]

    @pl.when(pl.program_id(1) == 0)
    def _():
        @pl.when(j == 0)
        def _():
            for c in copies(j):
                c.start()

        for q, c in enumerate(copies(j)):
            c.wait()
            wbs[q][...] = stage[q].astype(BF16)

        @pl.when(j + 1 < nj)
        def _():
            for c in copies(j + 1):
                c.start()


def _mm_bias_kernel(a_ref, w_hbm, b_ref, o_ref, stage, wb, sem, *, layer, col0, tn, sigmoid):
    _stage_weights(w_hbm, layer, (col0,), tn, stage, (wb,), sem)
    acc = jnp.dot(a_ref[...], wb[...], preferred_element_type=F32) + b_ref[...]
    if sigmoid:
        acc = jax.nn.sigmoid(acc)
    o_ref[...] = acc.astype(o_ref.dtype)


def _mm_bias(a, w, b, layer, col0, n, out_dtype, *, tm, tn, sigmoid=False, name):
    m, k = a.shape
    return pl.pallas_call(
        functools.partial(_mm_bias_kernel, layer=layer, col0=col0, tn=tn, sigmoid=sigmoid),
        out_shape=jax.ShapeDtypeStruct((m, n), out_dtype),
        grid=(n // tn, m // tm),
        in_specs=[pl.BlockSpec((tm, k), lambda j, i: (i, 0)),
                  pl.BlockSpec(memory_space=pl.ANY),
                  pl.BlockSpec((1, tn), lambda j, i: (0, j))],
        out_specs=pl.BlockSpec((tm, tn), lambda j, i: (i, j)),
        scratch_shapes=[pltpu.VMEM((1, k, tn), F32), pltpu.VMEM((k, tn), BF16),
                        pltpu.SemaphoreType.DMA((1,))],
        compiler_params=_cparams(("arbitrary", "arbitrary")),
        name=name,
    )(a, w, b[layer, col0:col0 + n].reshape(1, n))


def _mm_glu_kernel(a_ref, w_hbm, ba_ref, bg_ref, o_ref, stage, wab, wgb, sem, *, layer, col0, n, tn):
    _stage_weights(w_hbm, layer, (col0, col0 + n), tn, stage, (wab, wgb), sem)
    a = a_ref[...]
    u = jnp.dot(a, wab[...], preferred_element_type=F32) + ba_ref[...]
    g = jnp.dot(a, wgb[...], preferred_element_type=F32) + bg_ref[...]
    o_ref[...] = (u * jax.nn.sigmoid(g)).astype(o_ref.dtype)


def _mm_glu(a, w, b, layer, col0, n, *, tm, tn):
    m, k = a.shape
    bu = b[layer, col0:col0 + n].reshape(1, n)
    bg = b[layer, col0 + n:col0 + 2 * n].reshape(1, n)
    return pl.pallas_call(
        functools.partial(_mm_glu_kernel, layer=layer, col0=col0, n=n, tn=tn),
        out_shape=jax.ShapeDtypeStruct((m, n), F32),
        grid=(n // tn, m // tm),
        in_specs=[pl.BlockSpec((tm, k), lambda j, i: (i, 0)),
                  pl.BlockSpec(memory_space=pl.ANY),
                  pl.BlockSpec((1, tn), lambda j, i: (0, j)), pl.BlockSpec((1, tn), lambda j, i: (0, j))],
        out_specs=pl.BlockSpec((tm, tn), lambda j, i: (i, j)),
        scratch_shapes=[pltpu.VMEM((2, k, tn), F32), pltpu.VMEM((k, tn), BF16), pltpu.VMEM((k, tn), BF16),
                        pltpu.SemaphoreType.DMA((2,))],
        compiler_params=_cparams(("arbitrary", "arbitrary")),
        name="inproj_glu",
    )(a, w, bu, bg)


def _merge_kernel(a_ref, c_ref, wa_ref, wc_ref, ga_ref, gc_ref, o_ref, wab, wcb):
    _cast_on_first_row_block([(wa_ref, wab), (wc_ref, wcb)])
    pa = jnp.dot(a_ref[...], wab[...], preferred_element_type=F32)
    pc = jnp.dot(c_ref[...], wcb[...], preferred_element_type=F32)
    o_ref[...] = (ga_ref[...].astype(F32) * pa + gc_ref[...].astype(F32) * pc).astype(o_ref.dtype)


def _merge(attn, conv, wa, wc, layer, gates, *, tm, tn):
    m, k = attn.shape
    n = wa.shape[2]
    nj = n // tn
    return pl.pallas_call(
        _merge_kernel,
        out_shape=jax.ShapeDtypeStruct((m, n), BF16),
        grid=(nj, m // tm),
        in_specs=[pl.BlockSpec((tm, k), lambda j, i: (i, 0)),
                  pl.BlockSpec((tm, k), lambda j, i: (i, 0)),
                  _wspec(k, tn, layer), _wspec(k, tn, layer),
                  pl.BlockSpec((tm, tn), lambda j, i: (i, j)),
                  pl.BlockSpec((tm, tn), lambda j, i: (i, j + nj))],
        out_specs=pl.BlockSpec((tm, tn), lambda j, i: (i, j)),
        scratch_shapes=[pltpu.VMEM((k, tn), BF16), pltpu.VMEM((k, tn), BF16)],
        compiler_params=_cparams(("arbitrary", "arbitrary")),
        name="merge",
    )(attn, conv, wa, wc, gates, gates)


def _mm_residual_kernel(a_ref, w_ref, x_ref, o_ref, wb):
    _cast_on_first_row_block([(w_ref, wb)])
    o_ref[...] = x_ref[...] + jnp.dot(a_ref[...], wb[...], preferred_element_type=F32)


def _mm_residual(a, w, layer, x, *, tm, tn):
    m, k = a.shape
    n = w.shape[2]
    return pl.pallas_call(
        _mm_residual_kernel,
        out_shape=jax.ShapeDtypeStruct((m, n), F32),
        grid=(n // tn, m // tm),
        in_specs=[pl.BlockSpec((tm, k), lambda j, i: (i, 0)),
                  _wspec(k, tn, layer),
                  pl.BlockSpec((tm, tn), lambda j, i: (i, j))],
        out_specs=pl.BlockSpec((tm, tn), lambda j, i: (i, j)),
        scratch_shapes=[pltpu.VMEM((k, tn), BF16)],
        compiler_params=_cparams(("arbitrary", "arbitrary")),
        name="outproj",
    )(a, w, x)


ATTN_RC = 32


def _attn_kernel(sink_ref, q_ref, k_ref, v_ref, cos_ref, sin_ref, qg_ref, kg_ref, o_ref,
                 kwin, vwin, kb, vb, qs, bias_scr, s_scr, p_scr, inv_scr):
    n = pl.program_id(1)
    blk = ATTN_BLOCK

    @pl.when(n == 0)
    def _():
        kwin[0:blk, :] = jnp.zeros((blk, KV_WIDTH), F32)
        vwin[0:blk, :] = jnp.zeros((blk, KV_WIDTH), F32)

    cos = cos_ref[...]
    sin = sin_ref[...]
    lane = lax.broadcasted_iota(jnp.int32, (blk, LANES), 1)
    first_half = (lane % HEAD_DIM) < (HEAD_DIM // 2)
    seg_r = lax.broadcasted_iota(jnp.int32, (LANES, LANES), 0) // HEAD_DIM
    seg_c = lax.broadcasted_iota(jnp.int32, (LANES, LANES), 1) // HEAD_DIM
    seg = jnp.where(seg_r == seg_c, 1.0, 0.0).astype(BF16)

    def norm_rope(t, g):
        t2 = t * t
        hi = t2.astype(BF16)
        lo = (t2 - hi.astype(F32)).astype(BF16)
        ss = (jnp.dot(hi, seg, preferred_element_type=F32)
              + jnp.dot(lo, seg, preferred_element_type=F32))
        tn = t * lax.rsqrt(ss * (1.0 / HEAD_DIM) + NORM_EPS) * g
        rot = jnp.where(first_half, pltpu.roll(tn, LANES - HEAD_DIM // 2, 1),
                        pltpu.roll(tn, HEAD_DIM // 2, 1))
        return tn * cos + rot * sin

    kg = kg_ref[...]
    qg = qg_ref[...]
    for c in range(KV_WIDTH // LANES):
        cols = slice(c * LANES, (c + 1) * LANES)
        kwin[blk:2 * blk, cols] = norm_rope(k_ref[:, cols], kg)
    vwin[blk:2 * blk, :] = v_ref[...]

    lane2 = lax.broadcasted_iota(jnp.int32, (2 * blk, LANES), 1)
    lo_half = lane2 < HEAD_DIM
    for c in range(KV_WIDTH // LANES):
        cols = slice(c * LANES, (c + 1) * LANES)
        for win, dst in ((kwin, kb), (vwin, vb)):
            x = win[:, cols]
            xs = pltpu.roll(x, HEAD_DIM, 1)
            dst[2 * c, 0:2 * blk, :] = jnp.where(lo_half, x, 0.0).astype(BF16)
            dst[2 * c, 2 * blk:4 * blk, :] = jnp.where(lo_half, 0.0, xs).astype(BF16)
            dst[2 * c + 1, 0:2 * blk, :] = jnp.where(lo_half, xs, 0.0).astype(BF16)
            dst[2 * c + 1, 2 * blk:4 * blk, :] = jnp.where(lo_half, 0.0, x).astype(BF16)

    qi = lax.broadcasted_iota(jnp.int32, (blk, 4 * blk), 0)
    kj = lax.broadcasted_iota(jnp.int32, (blk, 4 * blk), 1) % (2 * blk)
    rel = qi + blk - kj
    first_key = jnp.where(n == 0, blk, 0)
    mask = (rel >= 0) & (rel < WINDOW) & (kj >= first_key)
    bias_scr[...] = jnp.where(mask, 0.0, -jnp.inf)

    pairs_per_kv = N_Q_HEADS // N_KV_HEADS // 2
    for p in range(N_Q_HEADS // 2):
        g, pp = divmod(p, pairs_per_kv)
        q2 = norm_rope(q_ref[:, p * LANES:(p + 1) * LANES], qg) * (HEAD_DIM ** -0.5)
        qs[g, pp * blk:(pp + 1) * blk, :] = q2.astype(BF16)

    lo_out = lax.broadcasted_iota(jnp.int32, (ATTN_RC, LANES), 1) < HEAD_DIM
    for g in range(N_KV_HEADS):
        s_scr[g] = lax.dot_general(qs[g], kb[g], (((1,), (1,)), ((), ())), preferred_element_type=F32)
        for r in range(pairs_per_kv * blk // ATTN_RC):
            rows = slice(r * ATTN_RC, (r + 1) * ATTN_RC)
            pp, q0 = divmod(r * ATTN_RC, blk)
            s = s_scr[g, rows, :] + bias_scr[q0:q0 + ATTN_RC, :]
            invs = []
            for h in range(2):
                keys = slice(h * 2 * blk, (h + 1) * 2 * blk)
                sh = s[:, keys]
                sink = sink_ref[2 * (g * pairs_per_kv + pp) + h]
                m = jnp.maximum(jnp.max(sh, axis=-1, keepdims=True), sink)
                e = jnp.exp(sh - m)
                den = jnp.sum(e, axis=-1, keepdims=True) + jnp.exp(sink - m)
                p_scr[g, rows, keys] = e.astype(BF16)
                invs.append(1.0 / den)
            inv_scr[g, rows, :] = jnp.where(lo_out, invs[0], invs[1])
        o = jnp.dot(p_scr[g], vb[g], preferred_element_type=F32) * inv_scr[g]
        for pp in range(pairs_per_kv):
            p = g * pairs_per_kv + pp
            o_ref[:, p * LANES:(p + 1) * LANES] = o[pp * blk:(pp + 1) * blk, :].astype(o_ref.dtype)

    kwin[0:blk, :] = kwin[blk:2 * blk, :]
    vwin[0:blk, :] = vwin[blk:2 * blk, :]


def _attention(qkv, sinks, cos_t, sin_t, qg, kg, batch, seq):
    t = qkv.shape[0]
    blk = ATTN_BLOCK
    nb = seq // blk
    kcol = Q_WIDTH // KV_WIDTH
    return pl.pallas_call(
        _attn_kernel,
        out_shape=jax.ShapeDtypeStruct((t, Q_WIDTH), BF16),
        grid=(batch, nb),
        in_specs=[pl.BlockSpec(memory_space=pltpu.SMEM),
                  pl.BlockSpec((blk, Q_WIDTH), lambda b, n: (b * nb + n, 0)),
                  pl.BlockSpec((blk, KV_WIDTH), lambda b, n: (b * nb + n, kcol)),
                  pl.BlockSpec((blk, KV_WIDTH), lambda b, n: (b * nb + n, kcol + 1)),
                  pl.BlockSpec((blk, LANES), lambda b, n: (n, 0)),
                  pl.BlockSpec((blk, LANES), lambda b, n: (n, 0)),
                  pl.BlockSpec((1, LANES), lambda b, n: (0, 0)),
                  pl.BlockSpec((1, LANES), lambda b, n: (0, 0))],
        out_specs=pl.BlockSpec((blk, Q_WIDTH), lambda b, n: (b * nb + n, 0)),
        scratch_shapes=[pltpu.VMEM((2 * blk, KV_WIDTH), F32),
                        pltpu.VMEM((2 * blk, KV_WIDTH), F32),
                        pltpu.VMEM((N_KV_HEADS, 4 * blk, LANES), BF16),
                        pltpu.VMEM((N_KV_HEADS, 4 * blk, LANES), BF16),
                        pltpu.VMEM((N_KV_HEADS, 4 * blk, LANES), BF16),
                        pltpu.VMEM((blk, 4 * blk), F32),
                        pltpu.VMEM((N_KV_HEADS, 4 * blk, 4 * blk), F32),
                        pltpu.VMEM((N_KV_HEADS, 4 * blk, 4 * blk), BF16),
                        pltpu.VMEM((N_KV_HEADS, 4 * blk, LANES), F32)],
        compiler_params=_cparams(("arbitrary", "arbitrary")),
        name="swa_attention",
    )(sinks, qkv, qkv, qkv, cos_t, sin_t, qg, kg)


CONV_TS = 256
CONV_HALO = 32
CONV_RB = 128


def _conv_kernel(z_ref, w_ref, b_ref, lg_ref, lb_ref, o_ref, zbuf, cbuf):
    n = pl.program_id(1)

    @pl.when(n == 0)
    def _():
        zbuf[0:CONV_HALO, :] = jnp.zeros((CONV_HALO, CONV_CH), F32)

    zbuf[CONV_HALO:CONV_HALO + CONV_TS, :] = z_ref[...]
    shift = CONV_HALO - (CONV_WIDTH - 1)

    def chunk(c, carry):
        c0 = pl.multiple_of(c * LANES, LANES)
        for rb in range(CONV_TS // CONV_RB):
            base = rb * CONV_RB
            acc = jnp.broadcast_to(b_ref[:, pl.ds(c0, LANES)], (CONV_RB, LANES))
            for res in range(SUBLANES):
                extra = SUBLANES if res else 0
                part = None
                for off in range(shift, shift + CONV_WIDTH):
                    if off % SUBLANES != res:
                        continue
                    r0 = base + off - res
                    term = (zbuf[r0:r0 + CONV_RB + extra, pl.ds(c0, LANES)]
                            * w_ref[off - shift:off - shift + 1, pl.ds(c0, LANES)])
                    part = term if part is None else part + term
                acc = acc + part[res:res + CONV_RB]
            cbuf[base:base + CONV_RB, pl.ds(c0, LANES)] = acc
        return carry

    lax.fori_loop(0, CONV_CH // LANES, chunk, 0)

    zbuf[0:CONV_HALO, :] = zbuf[CONV_TS:CONV_TS + CONV_HALO, :]

    y = cbuf[...]
    mu = jnp.mean(y, axis=-1, keepdims=True)
    yc = y - mu
    var = jnp.mean(yc * yc, axis=-1, keepdims=True)
    yn = yc * lax.rsqrt(var + NORM_EPS) * lg_ref[...] + lb_ref[...]
    o_ref[...] = (yn * jax.nn.sigmoid(yn)).astype(o_ref.dtype)


def _conformer_conv(z, dw_w, dw_b, ln_g, ln_b, batch, seq):
    t = z.shape[0]
    ns = seq // CONV_TS
    vec = lambda a: a.reshape(1, CONV_CH)
    return pl.pallas_call(
        _conv_kernel,
        out_shape=jax.ShapeDtypeStruct((t, CONV_CH), BF16),
        grid=(batch, ns),
        in_specs=[pl.BlockSpec((CONV_TS, CONV_CH), lambda b, n: (b * ns + n, 0)),
                  pl.BlockSpec((CONV_WIDTH, CONV_CH), lambda b, n: (0, 0)),
                  pl.BlockSpec((1, CONV_CH), lambda b, n: (0, 0)),
                  pl.BlockSpec((1, CONV_CH), lambda b, n: (0, 0)),
                  pl.BlockSpec((1, CONV_CH), lambda b, n: (0, 0))],
        out_specs=pl.BlockSpec((CONV_TS, CONV_CH), lambda b, n: (b * ns + n, 0)),
        scratch_shapes=[pltpu.VMEM((CONV_HALO + CONV_TS, CONV_CH), F32),
                        pltpu.VMEM((CONV_TS, CONV_CH), F32)],
        compiler_params=_cparams(("arbitrary", "arbitrary")),
        name="conformer_conv",
    )(z, dw_w, vec(dw_b), vec(ln_g), vec(ln_b))


ROUTER_TM = 256
HI16 = 0xFFFF0000


def _pack_halves(xb):
    n = xb.shape[1] // 2
    lo = lax.bitcast_convert_type(xb[:, :n].astype(F32), jnp.uint32)
    hi = lax.bitcast_convert_type(xb[:, n:].astype(F32), jnp.uint32)
    return (lo >> 16) | (hi & jnp.uint32(HI16))


def _unpack_halves(w):
    lo = lax.bitcast_convert_type(w << 16, F32).astype(BF16)
    hi = lax.bitcast_convert_type(w & jnp.uint32(HI16), F32).astype(BF16)
    return lo, hi


def _router_kernel(x_ref, g_ref, wr_ref, br_ref, h_ref, idx_ref, gate_ref):
    x = x_ref[...]
    ms = jnp.mean(x * x, axis=-1, keepdims=True)
    hb = (x * lax.rsqrt(ms + NORM_EPS) * g_ref[...]).astype(BF16)
    h_ref[...] = _pack_halves(hb)
    vals = jnp.dot(hb, wr_ref[...], preferred_element_type=F32) + br_ref[...]
    lane = lax.broadcasted_iota(jnp.int32, vals.shape, 1).astype(F32)
    tops, idxs = [], []
    for _ in range(TOP_K):
        m = jnp.max(vals, axis=-1, keepdims=True)
        idx = jnp.min(jnp.where(vals == m, lane, float(LANES)), axis=-1, keepdims=True)
        tops.append(m)
        idxs.append(idx)
        vals = jnp.where(lane == idx, -jnp.inf, vals)
    es = [jnp.exp(v - tops[0]) for v in tops]
    den = es[0] + es[1] + es[2] + es[3]
    idx_out = jnp.zeros(vals.shape, F32)
    gate_out = jnp.zeros(vals.shape, F32)
    for k in range(TOP_K):
        idx_out = jnp.where(lane == float(k), idxs[k], idx_out)
        gate_out = jnp.where(lane == float(k), es[k] / den, gate_out)
    idx_ref[...] = idx_out.astype(jnp.int32)
    gate_ref[...] = gate_out


def _router(x1, g, w_router, b_router):
    t, d = x1.shape
    tm = ROUTER_TM
    wr = jnp.zeros((d, LANES), BF16).at[:, :N_EXPERTS].set(w_router.astype(BF16))
    br = jnp.full((1, LANES), -1e30, F32).at[0, :N_EXPERTS].set(b_router)
    return pl.pallas_call(
        _router_kernel,
        out_shape=(jax.ShapeDtypeStruct((t, d // 2), jnp.uint32),
                   jax.ShapeDtypeStruct((t, LANES), jnp.int32),
                   jax.ShapeDtypeStruct((t, LANES), F32)),
        grid=(t // tm,),
        in_specs=[pl.BlockSpec((tm, d), lambda i: (i, 0)),
                  pl.BlockSpec((1, d), lambda i: (0, 0)),
                  pl.BlockSpec((d, LANES), lambda i: (0, 0)),
                  pl.BlockSpec((1, LANES), lambda i: (0, 0))],
        out_specs=(pl.BlockSpec((tm, d // 2), lambda i: (i, 0)),
                   pl.BlockSpec((tm, LANES), lambda i: (i, 0)),
                   pl.BlockSpec((tm, LANES), lambda i: (i, 0))),
        compiler_params=_cparams(("parallel",)),
        name="router",
    )(x1, g.reshape(1, d), wr, br)


GATHER_ROWS = 1024
GATHER_UNROLL = 8
GATHER_CHUNK = 128


def _gather_kernel(idx_ref, nxt_ref, src_hbm, o_ref, buf, sem):
    i = pl.program_id(0)
    nsteps = pl.num_programs(0)
    slot = lax.rem(i, 2)

    def issue(ids, s):
        def body(q, carry):
            for u in range(GATHER_UNROLL):
                r = q * GATHER_UNROLL + u
                pltpu.make_async_copy(src_hbm.at[pl.ds(ids[0, 0, r], 1), :],
                                      buf.at[s, pl.ds(r, 1), :], sem.at[s]).start(priority=u % 2)
            return carry
        lax.fori_loop(0, GATHER_ROWS // GATHER_UNROLL, body, 0)

    @pl.when(i == 0)
    def _():
        issue(idx_ref, 0)

    @pl.when(i + 1 < nsteps)
    def _():
        issue(nxt_ref, 1 - slot)

    pltpu.make_async_copy(src_hbm.at[pl.ds(0, GATHER_ROWS), :], buf.at[slot], sem.at[slot]).wait()
    half = buf.shape[2]
    for c in range(GATHER_ROWS // GATHER_CHUNK):
        rows = slice(c * GATHER_CHUNK, (c + 1) * GATHER_CHUNK)
        lo, hi = _unpack_halves(buf[slot, rows, :])
        o_ref[rows, 0:half] = lo
        o_ref[rows, half:2 * half] = hi


def _gather_rows(src, row_idx):
    n = row_idx.shape[0]
    d = src.shape[1]
    nsteps = n // GATHER_ROWS
    idx3 = row_idx.reshape(nsteps, 1, GATHER_ROWS)
    return pl.pallas_call(
        _gather_kernel,
        out_shape=jax.ShapeDtypeStruct((n, 2 * d), BF16),
        grid=(nsteps,),
        in_specs=[pl.BlockSpec((1, 1, GATHER_ROWS), lambda i: (i, 0, 0), memory_space=pltpu.SMEM),
                  pl.BlockSpec((1, 1, GATHER_ROWS), lambda i: (jnp.minimum(i + 1, nsteps - 1), 0, 0),
                               memory_space=pltpu.SMEM),
                  pl.BlockSpec(memory_space=pl.ANY)],
        out_specs=pl.BlockSpec((GATHER_ROWS, 2 * d), lambda i: (i, 0)),
        scratch_shapes=[pltpu.VMEM((2, GATHER_ROWS, d), src.dtype),
                        pltpu.SemaphoreType.DMA((2,))],
        compiler_params=_cparams(("arbitrary",)),
        name="dispatch_gather",
    )(idx3, idx3, src)


MOE_NBT = 10
MOE_TM = MOE_NBT * MOE_BLOCK
MOE_FAST = (8, 9)
MOE_PIECE = 2
MOE_FC = 256
MOE_NC = 4 * MOE_FC
MOE_S1 = D_FF // MOE_FC
MOE_S2 = D_MODEL // MOE_NC
MOE_STEPS = MOE_S1 + MOE_S2
assert MOE_STEPS % 2 == 0 and D_MODEL == 2 * D_FF


def _moe_max_blocks(t):
    return t * TOP_K // MOE_BLOCK + N_EXPERTS


def _moe_max_tiles(t):
    return (_moe_max_blocks(t) + N_EXPERTS * (MOE_NBT - 1)) // MOE_NBT + 1


def _moe_kernel(te_ref, tb_ref, tn_ref, nt_ref,
                xs_hbm, wgu_hbm, bgu_hbm, wd_hbm, bd_hbm, ys_hbm,
                xbuf, act, obuf, ring, bgu, bdn, xsem, osem, wsem, bsem):
    i = pl.program_id(0)
    ntiles = nt_ref[0]
    half = MOE_NC // 2

    def weight_copies(tile, s, slot):
        e = te_ref[tile]
        c1 = pl.multiple_of(jnp.minimum(s, MOE_S1 - 1) * MOE_FC, MOE_FC)
        c2 = pl.multiple_of(jnp.maximum(s - MOE_S1, 0) * MOE_NC, MOE_NC)
        gate_up = [pltpu.make_async_copy(wgu_hbm.at[e, :, pl.ds(g * D_FF + c1, MOE_FC)],
                                         ring.at[slot, :, pl.ds(g * MOE_FC, MOE_FC)], wsem.at[slot])
                   for g in range(2)]
        down = [pltpu.make_async_copy(wd_hbm.at[e, :, pl.ds(c2 + h * half, half)],
                                      ring.at[slot, pl.ds(h * D_FF, D_FF), :], wsem.at[slot])
                for h in range(2)]
        return gate_up, down

    def weights(tile, s, slot, op):
        s = jnp.asarray(s, jnp.int32)
        gate_up, down = weight_copies(tile, s, slot)

        @pl.when(s < MOE_S1)
        def _():
            for c in gate_up:
                op(c)

        @pl.when(s >= MOE_S1)
        def _():
            for c in down:
                op(c)

    def bias_copies(tile):
        e = te_ref[tile]
        ts = lax.rem(tile, 2)
        return [pltpu.make_async_copy(bgu_hbm.at[e], bgu.at[ts], bsem.at[ts]),
                pltpu.make_async_copy(bd_hbm.at[e], bdn.at[ts], bsem.at[ts])]

    start = lambda c: c.start()
    wait = lambda c: c.wait()

    def xs_copy(tile, row0, rows):
        src0 = pl.multiple_of(tb_ref[tile] * MOE_BLOCK + row0, MOE_BLOCK)
        return pltpu.make_async_copy(xs_hbm.at[pl.ds(src0, rows), :],
                                     xbuf.at[pl.ds(row0, rows), :], xsem.at[0])

    def for_row_pieces(nb, emit):
        general = nb >= 0
        for f in MOE_FAST:
            general = general & (nb != f)

            @pl.when(nb == f)
            def _():
                emit(0, f * MOE_BLOCK)

        @pl.when(general)
        def _():
            nbig = nb // MOE_PIECE
            big = MOE_PIECE * MOE_BLOCK

            def big_piece(q, carry):
                emit(pl.multiple_of(q * big, big), big)
                return carry

            def small_piece(b, carry):
                emit(pl.multiple_of(b * MOE_BLOCK, MOE_BLOCK), MOE_BLOCK)
                return carry

            lax.fori_loop(0, nbig, big_piece, 0)
            lax.fori_loop(nbig * MOE_PIECE, nb, small_piece, 0)

    def out_copy(slot, row0, rows, n):
        dst0 = pl.multiple_of(tb_ref[i] * MOE_BLOCK + row0, MOE_BLOCK)
        col0 = pl.multiple_of(n * MOE_NC, MOE_NC)
        return pltpu.make_async_copy(obuf.at[slot, pl.ds(row0, rows), :],
                                     ys_hbm.at[pl.ds(dst0, rows), pl.ds(col0, MOE_NC)],
                                     osem.at[slot])

    @pl.when(i < ntiles)
    def _():
        nb = tn_ref[i]
        tslot = lax.rem(i, 2)

        @pl.when(i == 0)
        def _():
            for_row_pieces(nb, lambda row0, rows: xs_copy(i, row0, rows).start())
            weights(i, 0, 0, start)
            for c in bias_copies(i):
                c.start()

        for_row_pieces(nb, lambda row0, rows: xs_copy(i, row0, rows).wait())
        for c in bias_copies(i):
            c.wait()

        def step(s, carry):
            slot = lax.rem(s, 2)
            weights(i, s, slot, wait)

            @pl.when(s + 1 < MOE_STEPS)
            def _():
                weights(i, s + 1, 1 - slot, start)

            @pl.when((s + 1 == MOE_STEPS) & (i + 1 < ntiles))
            def _():
                weights(i + 1, 0, 1 - slot, start)
                for c in bias_copies(i + 1):
                    c.start()

            @pl.when(s < MOE_S1)
            def _():
                c0 = pl.multiple_of(s * MOE_FC, MOE_FC)

                def gate_up(row0, rows):
                    x = xbuf[pl.ds(row0, rows), :]
                    wg = ring[slot, :, 0:MOE_FC].astype(BF16)
                    wl = ring[slot, :, MOE_FC:2 * MOE_FC].astype(BF16)
                    glu = jnp.dot(x, wg, preferred_element_type=F32) + bgu[tslot, :, pl.ds(c0, MOE_FC)]
                    lin = jnp.dot(x, wl, preferred_element_type=F32) + bgu[tslot, :, pl.ds(D_FF + c0, MOE_FC)]
                    glu = jnp.minimum(glu, SWIGLU_LIMIT)
                    lin = jnp.clip(lin, -SWIGLU_LIMIT, SWIGLU_LIMIT)
                    a = glu * jax.nn.sigmoid(SWIGLU_ALPHA * glu) * (lin + 1.0)
                    act[pl.ds(row0, rows), pl.ds(c0, MOE_FC)] = a.astype(BF16)

                for_row_pieces(nb, gate_up)

            @pl.when(s >= MOE_S1)
            def _():
                n = s - MOE_S1
                oslot = lax.rem(n, 2)

                @pl.when((s == MOE_S1) & (i + 1 < ntiles))
                def _():
                    for_row_pieces(tn_ref[i + 1], lambda row0, rows: xs_copy(i + 1, row0, rows).start())

                def down(row0, rows):
                    a = act[pl.ds(row0, rows), :]
                    for h in range(2):
                        wd = ring[slot, h * D_FF:(h + 1) * D_FF, :].astype(BF16)
                        bd = bdn[tslot, :, pl.ds(pl.multiple_of(n * MOE_NC + h * half, half), half)]
                        obuf[oslot, pl.ds(row0, rows), h * half:(h + 1) * half] = (
                            jnp.dot(a, wd, preferred_element_type=F32) + bd)
                    out_copy(oslot, row0, rows, n).start()

                for_row_pieces(nb, down)

                @pl.when(n > 0)
                def _():
                    for_row_pieces(nb, lambda row0, rows: out_copy(1 - oslot, row0, rows, n - 1).wait())

                @pl.when(n == MOE_S2 - 1)
                def _():
                    for_row_pieces(nb, lambda row0, rows: out_copy(oslot, row0, rows, n).wait())

            return carry

        lax.fori_loop(0, MOE_STEPS, step, 0)

    @pl.when(i == ntiles)
    def _():
        obuf[0, 0:MOE_BLOCK, :] = jnp.zeros((MOE_BLOCK, MOE_NC), F32)

        def fill(blk, carry):
            row0 = pl.multiple_of(blk * MOE_BLOCK, MOE_BLOCK)
            fills = [pltpu.make_async_copy(
                obuf.at[0, pl.ds(0, MOE_BLOCK), :],
                ys_hbm.at[pl.ds(row0, MOE_BLOCK), pl.ds(c * MOE_NC, MOE_NC)], osem.at[0])
                for c in range(MOE_S2)]
            for f in fills:
                f.start()
            for f in fills:
                f.wait()
            return carry

        lax.fori_loop(tb_ref[ntiles - 1] + tn_ref[ntiles - 1], ys_hbm.shape[0] // MOE_BLOCK, fill, 0)


def _moe_experts(xs, n_rows, tile_e, tile_b0, tile_nb, n_tiles, w_gate_up, b_gate_up, w_down, b_down):
    grid_spec = pltpu.PrefetchScalarGridSpec(
        num_scalar_prefetch=4,
        grid=(tile_e.shape[0],),
        in_specs=[pl.BlockSpec(memory_space=pl.ANY)] * 5,
        out_specs=pl.BlockSpec(memory_space=pl.ANY),
        scratch_shapes=[pltpu.VMEM((MOE_TM, D_MODEL), BF16),
                        pltpu.VMEM((MOE_TM, D_FF), BF16),
                        pltpu.VMEM((2, MOE_TM, MOE_NC), F32),
                        pltpu.VMEM((2, D_MODEL, 2 * MOE_FC), F32),
                        pltpu.VMEM((2, 1, 2 * D_FF), F32),
                        pltpu.VMEM((2, 1, D_MODEL), F32),
                        pltpu.SemaphoreType.DMA((1,)),
                        pltpu.SemaphoreType.DMA((2,)),
                        pltpu.SemaphoreType.DMA((2,)),
                        pltpu.SemaphoreType.DMA((2,))],
    )
    return pl.pallas_call(
        _moe_kernel,
        out_shape=jax.ShapeDtypeStruct((n_rows, D_MODEL), F32),
        grid_spec=grid_spec,
        compiler_params=_cparams(("arbitrary",)),
        name="moe_experts",
    )(tile_e, tile_b0, tile_nb, n_tiles, xs, w_gate_up, b_gate_up.reshape(N_EXPERTS, 1, 2 * D_FF),
      w_down, b_down.reshape(N_EXPERTS, 1, D_MODEL))


COMBINE_TT = 128


def _combine_kernel(pos_ref, nxt_ref, x_ref, gate_ref, ys_hbm, o_ref, buf, sem):
    i = pl.program_id(0)
    nsteps = pl.num_programs(0)
    slot = lax.rem(i, 2)

    def issue(ids, s):
        def body(r, carry):
            for k in range(TOP_K):
                pltpu.make_async_copy(ys_hbm.at[pl.ds(ids[0, 0, r * TOP_K + k], 1), :],
                                      buf.at[s, k, pl.ds(r, 1), :], sem.at[s]).start(priority=k % 2)
            return carry
        lax.fori_loop(0, COMBINE_TT, body, 0, unroll=4)

    @pl.when(i == 0)
    def _():
        issue(pos_ref, 0)

    @pl.when(i + 1 < nsteps)
    def _():
        issue(nxt_ref, 1 - slot)

    for k in range(TOP_K):
        pltpu.make_async_copy(ys_hbm.at[pl.ds(0, COMBINE_TT), :], buf.at[slot, k], sem.at[slot]).wait()
    gate = gate_ref[...]
    acc = x_ref[...]
    moe = gate[:, 0:1] * buf[slot, 0]
    for k in range(1, TOP_K):
        moe = moe + gate[:, k:k + 1] * buf[slot, k]
    o_ref[...] = acc + moe


def _combine(x1, gates, ys, pos):
    t, d = x1.shape
    tt = COMBINE_TT
    nsteps = t // tt
    pos3 = pos.reshape(nsteps, 1, tt * TOP_K)
    return pl.pallas_call(
        _combine_kernel,
        out_shape=jax.ShapeDtypeStruct((t, d), F32),
        grid=(nsteps,),
        in_specs=[pl.BlockSpec((1, 1, tt * TOP_K), lambda i: (i, 0, 0), memory_space=pltpu.SMEM),
                  pl.BlockSpec((1, 1, tt * TOP_K), lambda i: (jnp.minimum(i + 1, nsteps - 1), 0, 0),
                               memory_space=pltpu.SMEM),
                  pl.BlockSpec((tt, d), lambda i: (i, 0)),
                  pl.BlockSpec((tt, LANES), lambda i: (i, 0)),
                  pl.BlockSpec(memory_space=pl.ANY)],
        out_specs=pl.BlockSpec((tt, d), lambda i: (i, 0)),
        scratch_shapes=[pltpu.VMEM((2, TOP_K, tt, d), F32),
                        pltpu.SemaphoreType.DMA((2,))],
        compiler_params=_cparams(("arbitrary",)),
        name="combine",
    )(pos3, pos3, x1, gates, ys)


def _routing(top_idx):
    t = top_idx.shape[0]
    n_assign = t * TOP_K
    flat_e = top_idx.reshape(-1)
    experts = jnp.arange(N_EXPERTS, dtype=jnp.int32)
    order = jnp.argsort(flat_e)
    sorted_pos = jnp.argsort(order).astype(jnp.int32)
    counts = jnp.sum((flat_e[:, None] == experts[None, :]).astype(jnp.int32), axis=0)
    starts = jnp.cumsum(counts) - counts
    padded = (counts + MOE_BLOCK - 1) // MOE_BLOCK * MOE_BLOCK
    pad_ends = jnp.cumsum(padded)
    pad_starts = pad_ends - padded
    shift = pad_starts - starts
    pos = sorted_pos + shift[flat_e]
    n_rows = n_assign + N_EXPERTS * MOE_BLOCK
    n_rows = (n_rows + GATHER_ROWS - 1) // GATHER_ROWS * GATHER_ROWS
    blocks = jnp.arange(n_rows // MOE_BLOCK, dtype=jnp.int32)
    blk_e = jnp.minimum(jnp.sum((blocks[:, None] * MOE_BLOCK >= pad_ends[None, :]).astype(jnp.int32), axis=1),
                        N_EXPERTS - 1)
    rows = jnp.arange(n_rows, dtype=jnp.int32).reshape(-1, MOE_BLOCK)
    place = rows - shift[blk_e][:, None]
    valid = (rows - pad_starts[blk_e][:, None]) < counts[blk_e][:, None]
    row_tok = jnp.where(valid, order[jnp.clip(place, 0, n_assign - 1)].astype(jnp.int32) // TOP_K, 0)
    row_tok = row_tok.reshape(-1)

    nblk = padded // MOE_BLOCK
    blk0 = pad_starts // MOE_BLOCK
    ntile = (nblk + MOE_NBT - 1) // MOE_NBT
    tile_end = jnp.cumsum(ntile)
    n_tiles = tile_end[-1]
    ids = jnp.arange(_moe_max_tiles(t), dtype=jnp.int32)
    ids_c = jnp.minimum(ids, n_tiles - 1)
    tile_e = jnp.minimum(jnp.searchsorted(tile_end, ids_c, side='right'), N_EXPERTS - 1).astype(jnp.int32)
    local = ids_c - (tile_end - ntile)[tile_e]
    tile_b0 = (blk0[tile_e] + local * MOE_NBT).astype(jnp.int32)
    tile_nb = jnp.where(ids < n_tiles, jnp.clip(nblk[tile_e] - local * MOE_NBT, 0, MOE_NBT), 0).astype(jnp.int32)
    return row_tok, pos, tile_e, tile_b0, tile_nb, n_tiles.reshape(1).astype(jnp.int32)


def _rope_tables(seq):
    f32 = np.float32
    inv_freq = f32(1.0) / (f32(ROPE_THETA) ** (np.arange(0, HEAD_DIM, 2, dtype=f32) / f32(HEAD_DIM)))
    ang = np.arange(seq, dtype=f32)[:, None] * inv_freq[None, :]
    cos = np.cos(ang).astype(f32)
    sin = np.sin(ang).astype(f32)
    reps = LANES // HEAD_DIM
    cos_t = np.tile(np.concatenate([cos, cos], axis=-1), (1, reps))
    sin_t = np.tile(np.concatenate([-sin, sin], axis=-1), (1, reps))
    return jnp.asarray(cos_t), jnp.asarray(sin_t)


def kernel(x, norm1_g, w_in, b_in, q_norm_g, k_norm_g, attn_sinks, w_attn_o, conv_dw_w, conv_dw_b,
           conv_ln_g, conv_ln_b, w_conv_o, w_out, norm2_g, w_router, b_router, w_gate_up, b_gate_up,
           w_down, b_down):
    b, s, d = x.shape
    t = b * s
    depth = norm1_g.shape[0]
    xt = x.reshape(t, d)
    cos_t, sin_t = _rope_tables(s)
    reps = LANES // HEAD_DIM
    c0 = QKV_WIDTH
    c1 = c0 + CONV_CH
    c2 = c1 + CONV_CH
    for l in range(depth):
        h = _rmsnorm(xt, norm1_g[l], BF16)
        qkv = _mm_bias(h, w_in, b_in, l, 0, c0, F32, tm=1024, tn=512, name="inproj_qkv")
        z = _mm_glu(h, w_in, b_in, l, c0, CONV_CH, tm=1024, tn=256)
        gates = _mm_bias(h, w_in, b_in, l, c2, 2 * d, BF16, tm=1024, tn=1024, sigmoid=True,
                         name="inproj_gates")
        attn = _attention(qkv, attn_sinks[l], cos_t, sin_t,
                          jnp.tile(q_norm_g[l], reps).reshape(1, LANES),
                          jnp.tile(k_norm_g[l], reps).reshape(1, LANES), b, s)
        conv = _conformer_conv(z, conv_dw_w[l], conv_dw_b[l], conv_ln_g[l], conv_ln_b[l], b, s)
        merged = _merge(attn, conv, w_attn_o, w_conv_o, l, gates, tm=1024, tn=512)
        x1 = _mm_residual(merged, w_out, l, xt, tm=1024, tn=512)
        h2, idx_pad, gate_pad = _router(x1, norm2_g[l], w_router[l], b_router[l])
        row_tok, pos, tile_e, tile_b0, tile_nb, n_tiles = _routing(idx_pad[:, :TOP_K])
        xs = _gather_rows(h2, row_tok)
        ys = _moe_experts(xs, _moe_max_blocks(t) * MOE_BLOCK, tile_e, tile_b0, tile_nb, n_tiles,
                          w_gate_up[l], b_gate_up[l], w_down[l], b_down[l])
        xt = _combine(x1, gate_pad, ys, pos)
    return xt.reshape(b, s, d)
```

```python
import functools

import jax
import jax.numpy as jnp
import numpy as np
from jax import lax
from jax.experimental import pallas as pl
from jax.experimental.pallas import tpu as pltpu

D_MODEL = 4096
HEAD_DIM = 64
N_Q_HEADS = 32
N_KV_HEADS = 4
WINDOW = 128
ATTN_BLOCK = 128
ROPE_THETA = 10000.0
CONV_CH = 2048
CONV_WIDTH = 31
N_EXPERTS = 32
TOP_K = 4
D_FF = 2048
SWIGLU_ALPHA = 1.702
SWIGLU_LIMIT = 7.0
MOE_BLOCK = 128
NORM_EPS = 1e-5

Q_WIDTH = N_Q_HEADS * HEAD_DIM
KV_WIDTH = N_KV_HEADS * HEAD_DIM
QKV_WIDTH = Q_WIDTH + 2 * KV_WIDTH

LANES = 128
SUBLANES = 8
VMEM_LIMIT = 56 * 1024 * 1024

BF16 = jnp.bfloat16
F32 = jnp.float32


def _cparams(sem, vmem=VMEM_LIMIT):
    return pltpu.CompilerParams(dimension_semantics=sem, vmem_limit_bytes=vmem)


def _rmsnorm_kernel(x_ref, g_ref, o_ref):
    x = x_ref[...]
    ms = jnp.mean(x * x, axis=-1, keepdims=True)
    o_ref[...] = (x * lax.rsqrt(ms + NORM_EPS) * g_ref[...]).astype(o_ref.dtype)


def _rmsnorm(x, g, out_dtype, tm=512):
    t, d = x.shape
    return pl.pallas_call(
        _rmsnorm_kernel,
        out_shape=jax.ShapeDtypeStruct((t, d), out_dtype),
        grid=(t // tm,),
        in_specs=[pl.BlockSpec((tm, d), lambda i: (i, 0)),
                  pl.BlockSpec((1, d), lambda i: (0, 0))],
        out_specs=pl.BlockSpec((tm, d), lambda i: (i, 0)),
        compiler_params=_cparams(("parallel",)),
        name="rmsnorm",
    )(x, g.reshape(1, d))


def _wspec(k, tn, layer, off=0):
    return pl.BlockSpec((None, k, tn), lambda j, i: (layer, 0, j + off))


def _cast_on_first_row_block(pairs):
    @pl.when(pl.program_id(1) == 0)
    def _():
        for w_ref, wb_ref in pairs:
            wb_ref[...] = w_ref[...].astype(BF16)


def _stage_weights(w_hbm, layer, cols, tn, stage, wbs, sem):
    j = pl.program_id(0)
    nj = pl.num_programs(0)

    def copies(jj):
        return [pltpu.make_async_copy(w_hbm.at[layer, :, pl.ds(pl.multiple_of(c + jj * tn, LANES), tn)],
                                      stage.at[q], sem.at[q]) for q, c in enumerate(cols)]

    @pl.when(pl.program_id(1) == 0)
    def _():
        @pl.when(j == 0)
        def _():
            for c in copies(j):
                c.start()

        for q, c in enumerate(copies(j)):
            c.wait()
            wbs[q][...] = stage[q].astype(BF16)

        @pl.when(j + 1 < nj)
        def _():
            for c in copies(j + 1):
                c.start()


def _mm_bias_kernel(a_ref, w_hbm, b_ref, o_ref, stage, wb, sem, *, layer, col0, tn, sigmoid):
    _stage_weights(w_hbm, layer, (col0,), tn, stage, (wb,), sem)
    acc = jnp.dot(a_ref[...], wb[...], preferred_element_type=F32) + b_ref[...]
    if sigmoid:
        acc = jax.nn.sigmoid(acc)
    o_ref[...] = acc.astype(o_ref.dtype)


def _mm_bias(a, w, b, layer, col0, n, out_dtype, *, tm, tn, sigmoid=False, name):
    m, k = a.shape
    return pl.pallas_call(
        functools.partial(_mm_bias_kernel, layer=layer, col0=col0, tn=tn, sigmoid=sigmoid),
        out_shape=jax.ShapeDtypeStruct((m, n), out_dtype),
        grid=(n // tn, m // tm),
        in_specs=[pl.BlockSpec((tm, k), lambda j, i: (i, 0)),
                  pl.BlockSpec(memory_space=pl.ANY),
                  pl.BlockSpec((1, tn), lambda j, i: (0, j))],
        out_specs=pl.BlockSpec((tm, tn), lambda j, i: (i, j)),
        scratch_shapes=[pltpu.VMEM((1, k, tn), F32), pltpu.VMEM((k, tn), BF16),
                        pltpu.SemaphoreType.DMA((1,))],
        compiler_params=_cparams(("arbitrary", "arbitrary")),
        name=name,
    )(a, w, b[layer, col0:col0 + n].reshape(1, n))


def _mm_glu_kernel(a_ref, w_hbm, ba_ref, bg_ref, o_ref, stage, wab, wgb, sem, *, layer, col0, n, tn):
    _stage_weights(w_hbm, layer, (col0, col0 + n), tn, stage, (wab, wgb), sem)
    a = a_ref[...]
    u = jnp.dot(a, wab[...], preferred_element_type=F32) + ba_ref[...]
    g = jnp.dot(a, wgb[...], preferred_element_type=F32) + bg_ref[...]
    o_ref[...] = (u * jax.nn.sigmoid(g)).astype(o_ref.dtype)


def _mm_glu(a, w, b, layer, col0, n, *, tm, tn):
    m, k = a.shape
    bu = b[layer, col0:col0 + n].reshape(1, n)
    bg = b[layer, col0 + n:col0 + 2 * n].reshape(1, n)
    return pl.pallas_call(
        functools.partial(_mm_glu_kernel, layer=layer, col0=col0, n=n, tn=tn),
        out_shape=jax.ShapeDtypeStruct((m, n), F32),
        grid=(n // tn, m // tm),
        in_specs=[pl.BlockSpec((tm, k), lambda j, i: (i, 0)),
                  pl.BlockSpec(memory_space=pl.ANY),
                  pl.BlockSpec((1, tn), lambda j, i: (0, j)), pl.BlockSpec((1, tn), lambda j, i: (0, j))],
        out_specs=pl.BlockSpec((tm, tn), lambda j, i: (i, j)),
        scratch_shapes=[pltpu.VMEM((2, k, tn), F32), pltpu.VMEM((k, tn), BF16), pltpu.VMEM((k, tn), BF16),
                        pltpu.SemaphoreType.DMA((2,))],
        compiler_params=_cparams(("arbitrary", "arbitrary")),
        name="inproj_glu",
    )(a, w, bu, bg)


def _merge_kernel(a_ref, c_ref, wa_ref, wc_ref, ga_ref, gc_ref, o_ref, wab, wcb):
    _cast_on_first_row_block([(wa_ref, wab), (wc_ref, wcb)])
    pa = jnp.dot(a_ref[...], wab[...], preferred_element_type=F32)
    pc = jnp.dot(c_ref[...], wcb[...], preferred_element_type=F32)
    o_ref[...] = (ga_ref[...].astype(F32) * pa + gc_ref[...].astype(F32) * pc).astype(o_ref.dtype)


def _merge(attn, conv, wa, wc, layer, gates, *, tm, tn):
    m, k = attn.shape
    n = wa.shape[2]
    nj = n // tn
    return pl.pallas_call(
        _merge_kernel,
        out_shape=jax.ShapeDtypeStruct((m, n), BF16),
        grid=(nj, m // tm),
        in_specs=[pl.BlockSpec((tm, k), lambda j, i: (i, 0)),
                  pl.BlockSpec((tm, k), lambda j, i: (i, 0)),
                  _wspec(k, tn, layer), _wspec(k, tn, layer),
                  pl.BlockSpec((tm, tn), lambda j, i: (i, j)),
                  pl.BlockSpec((tm, tn), lambda j, i: (i, j + nj))],
        out_specs=pl.BlockSpec((tm, tn), lambda j, i: (i, j)),
        scratch_shapes=[pltpu.VMEM((k, tn), BF16), pltpu.VMEM((k, tn), BF16)],
        compiler_params=_cparams(("arbitrary", "arbitrary")),
        name="merge",
    )(attn, conv, wa, wc, gates, gates)


def _mm_residual_kernel(a_ref, w_ref, x_ref, o_ref, wb):
    _cast_on_first_row_block([(w_ref, wb)])
    o_ref[...] = x_ref[...] + jnp.dot(a_ref[...], wb[...], preferred_element_type=F32)


def _mm_residual(a, w, layer, x, *, tm, tn):
    m, k = a.shape
    n = w.shape[2]
    return pl.pallas_call(
        _mm_residual_kernel,
        out_shape=jax.ShapeDtypeStruct((m, n), F32),
        grid=(n // tn, m // tm),
        in_specs=[pl.BlockSpec((tm, k), lambda j, i: (i, 0)),
                  _wspec(k, tn, layer),
                  pl.BlockSpec((tm, tn), lambda j, i: (i, j))],
        out_specs=pl.BlockSpec((tm, tn), lambda j, i: (i, j)),
        scratch_shapes=[pltpu.VMEM((k, tn), BF16)],
        compiler_params=_cparams(("arbitrary", "arbitrary")),
        name="outproj",
    )(a, w, x)


ATTN_RC = 32


def _attn_kernel(sink_ref, q_ref, k_ref, v_ref, cos_ref, sin_ref, qg_ref, kg_ref, o_ref,
                 kwin, vwin, kb, vb, qs, bias_scr, s_scr, p_scr, inv_scr):
    n = pl.program_id(1)
    blk = ATTN_BLOCK

    @pl.when(n == 0)
    def _():
        kwin[0:blk, :] = jnp.zeros((blk, KV_WIDTH), F32)
        vwin[0:blk, :] = jnp.zeros((blk, KV_WIDTH), F32)

    cos = cos_ref[...]
    sin = sin_ref[...]
    lane = lax.broadcasted_iota(jnp.int32, (blk, LANES), 1)
    first_half = (lane % HEAD_DIM) < (HEAD_DIM // 2)
    seg_r = lax.broadcasted_iota(jnp.int32, (LANES, LANES), 0) // HEAD_DIM
    seg_c = lax.broadcasted_iota(jnp.int32, (LANES, LANES), 1) // HEAD_DIM
    seg = jnp.where(seg_r == seg_c, 1.0, 0.0).astype(BF16)

    def norm_rope(t, g):
        t2 = t * t
        hi = t2.astype(BF16)
        lo = (t2 - hi.astype(F32)).astype(BF16)
        ss = (jnp.dot(hi, seg, preferred_element_type=F32)
              + jnp.dot(lo, seg, preferred_element_type=F32))
        tn = t * lax.rsqrt(ss * (1.0 / HEAD_DIM) + NORM_EPS) * g
        rot = jnp.where(first_half, pltpu.roll(tn, LANES - HEAD_DIM // 2, 1),
                        pltpu.roll(tn, HEAD_DIM // 2, 1))
        return tn * cos + rot * sin

    kg = kg_ref[...]
    qg = qg_ref[...]
    for c in range(KV_WIDTH // LANES):
        cols = slice(c * LANES, (c + 1) * LANES)
        kwin[blk:2 * blk, cols] = norm_rope(k_ref[:, cols], kg)
    vwin[blk:2 * blk, :] = v_ref[...]

    lane2 = lax.broadcasted_iota(jnp.int32, (2 * blk, LANES), 1)
    lo_half = lane2 < HEAD_DIM
    for c in range(KV_WIDTH // LANES):
        cols = slice(c * LANES, (c + 1) * LANES)
        for win, dst in ((kwin, kb), (vwin, vb)):
            x = win[:, cols]
            xs = pltpu.roll(x, HEAD_DIM, 1)
            dst[2 * c, 0:2 * blk, :] = jnp.where(lo_half, x, 0.0).astype(BF16)
            dst[2 * c, 2 * blk:4 * blk, :] = jnp.where(lo_half, 0.0, xs).astype(BF16)
            dst[2 * c + 1, 0:2 * blk, :] = jnp.where(lo_half, xs, 0.0).astype(BF16)
            dst[2 * c + 1, 2 * blk:4 * blk, :] = jnp.where(lo_half, 0.0, x).astype(BF16)

    qi = lax.broadcasted_iota(jnp.int32, (blk, 4 * blk), 0)
    kj = lax.broadcasted_iota(jnp.int32, (blk, 4 * blk), 1) % (2 * blk)
    rel = qi + blk - kj
    first_key = jnp.where(n == 0, blk, 0)
    mask = (rel >= 0) & (rel < WINDOW) & (kj >= first_key)
    bias_scr[...] = jnp.where(mask, 0.0, -jnp.inf)

    pairs_per_kv = N_Q_HEADS // N_KV_HEADS // 2
    for p in range(N_Q_HEADS // 2):
        g, pp = divmod(p, pairs_per_kv)
        q2 = norm_rope(q_ref[:, p * LANES:(p + 1) * LANES], qg) * (HEAD_DIM ** -0.5)
        qs[g, pp * blk:(pp + 1) * blk, :] = q2.astype(BF16)

    lo_out = lax.broadcasted_iota(jnp.int32, (ATTN_RC, LANES), 1) < HEAD_DIM
    for g in range(N_KV_HEADS):
        s_scr[g] = lax.dot_general(qs[g], kb[g], (((1,), (1,)), ((), ())), preferred_element_type=F32)
        for r in range(pairs_per_kv * blk // ATTN_RC):
            rows = slice(r * ATTN_RC, (r + 1) * ATTN_RC)
            pp, q0 = divmod(r * ATTN_RC, blk)
            s = s_scr[g, rows, :] + bias_scr[q0:q0 + ATTN_RC, :]
            invs = []
            for h in range(2):
                keys = slice(h * 2 * blk, (h + 1) * 2 * blk)
                sh = s[:, keys]
                sink = sink_ref[2 * (g * pairs_per_kv + pp) + h]
                m = jnp.maximum(jnp.max(sh, axis=-1, keepdims=True), sink)
                e = jnp.exp(sh - m)
                den = jnp.sum(e, axis=-1, keepdims=True) + jnp.exp(sink - m)
                p_scr[g, rows, keys] = e.astype(BF16)
                invs.append(1.0 / den)
            inv_scr[g, rows, :] = jnp.where(lo_out, invs[0], invs[1])
        o = jnp.dot(p_scr[g], vb[g], preferred_element_type=F32) * inv_scr[g]
        for pp in range(pairs_per_kv):
            p = g * pairs_per_kv + pp
            o_ref[:, p * LANES:(p + 1) * LANES] = o[pp * blk:(pp + 1) * blk, :].astype(o_ref.dtype)

    kwin[0:blk, :] = kwin[blk:2 * blk, :]
    vwin[0:blk, :] = vwin[blk:2 * blk, :]


def _attention(qkv, sinks, cos_t, sin_t, qg, kg, batch, seq):
    t = qkv.shape[0]
    blk = ATTN_BLOCK
    nb = seq // blk
    kcol = Q_WIDTH // KV_WIDTH
    return pl.pallas_call(
        _attn_kernel,
        out_shape=jax.ShapeDtypeStruct((t, Q_WIDTH), BF16),
        grid=(batch, nb),
        in_specs=[pl.BlockSpec(memory_space=pltpu.SMEM),
                  pl.BlockSpec((blk, Q_WIDTH), lambda b, n: (b * nb + n, 0)),
                  pl.BlockSpec((blk, KV_WIDTH), lambda b, n: (b * nb + n, kcol)),
                  pl.BlockSpec((blk, KV_WIDTH), lambda b, n: (b * nb + n, kcol + 1)),
                  pl.BlockSpec((blk, LANES), lambda b, n: (n, 0)),
                  pl.BlockSpec((blk, LANES), lambda b, n: (n, 0)),
                  pl.BlockSpec((1, LANES), lambda b, n: (0, 0)),
                  pl.BlockSpec((1, LANES), lambda b, n: (0, 0))],
        out_specs=pl.BlockSpec((blk, Q_WIDTH), lambda b, n: (b * nb + n, 0)),
        scratch_shapes=[pltpu.VMEM((2 * blk, KV_WIDTH), F32),
                        pltpu.VMEM((2 * blk, KV_WIDTH), F32),
                        pltpu.VMEM((N_KV_HEADS, 4 * blk, LANES), BF16),
                        pltpu.VMEM((N_KV_HEADS, 4 * blk, LANES), BF16),
                        pltpu.VMEM((N_KV_HEADS, 4 * blk, LANES), BF16),
                        pltpu.VMEM((blk, 4 * blk), F32),
                        pltpu.VMEM((N_KV_HEADS, 4 * blk, 4 * blk), F32),
                        pltpu.VMEM((N_KV_HEADS, 4 * blk, 4 * blk), BF16),
                        pltpu.VMEM((N_KV_HEADS, 4 * blk, LANES), F32)],
        compiler_params=_cparams(("arbitrary", "arbitrary")),
        name="swa_attention",
    )(sinks, qkv, qkv, qkv, cos_t, sin_t, qg, kg)


CONV_TS = 256
CONV_HALO = 32
CONV_RB = 128


def _conv_kernel(z_ref, w_ref, b_ref, lg_ref, lb_ref, o_ref, zbuf, cbuf):
    n = pl.program_id(1)

    @pl.when(n == 0)
    def _():
        zbuf[0:CONV_HALO, :] = jnp.zeros((CONV_HALO, CONV_CH), F32)

    zbuf[CONV_HALO:CONV_HALO + CONV_TS, :] = z_ref[...]
    shift = CONV_HALO - (CONV_WIDTH - 1)

    def chunk(c, carry):
        c0 = pl.multiple_of(c * LANES, LANES)
        for rb in range(CONV_TS // CONV_RB):
            base = rb * CONV_RB
            acc = jnp.broadcast_to(b_ref[:, pl.ds(c0, LANES)], (CONV_RB, LANES))
            for res in range(SUBLANES):
                extra = SUBLANES if res else 0
                part = None
                for off in range(shift, shift + CONV_WIDTH):
                    if off % SUBLANES != res:
                        continue
                    r0 = base + off - res
                    term = (zbuf[r0:r0 + CONV_RB + extra, pl.ds(c0, LANES)]
                            * w_ref[off - shift:off - shift + 1, pl.ds(c0, LANES)])
                    part = term if part is None else part + term
                acc = acc + part[res:res + CONV_RB]
            cbuf[base:base + CONV_RB, pl.ds(c0, LANES)] = acc
        return carry

    lax.fori_loop(0, CONV_CH // LANES, chunk, 0)

    zbuf[0:CONV_HALO, :] = zbuf[CONV_TS:CONV_TS + CONV_HALO, :]

    y = cbuf[...]
    mu = jnp.mean(y, axis=-1, keepdims=True)
    yc = y - mu
    var = jnp.mean(yc * yc, axis=-1, keepdims=True)
    yn = yc * lax.rsqrt(var + NORM_EPS) * lg_ref[...] + lb_ref[...]
    o_ref[...] = (yn * jax.nn.sigmoid(yn)).astype(o_ref.dtype)


def _conformer_conv(z, dw_w, dw_b, ln_g, ln_b, batch, seq):
    t = z.shape[0]
    ns = seq // CONV_TS
    vec = lambda a: a.reshape(1, CONV_CH)
    return pl.pallas_call(
        _conv_kernel,
        out_shape=jax.ShapeDtypeStruct((t, CONV_CH), BF16),
        grid=(batch, ns),
        in_specs=[pl.BlockSpec((CONV_TS, CONV_CH), lambda b, n: (b * ns + n, 0)),
                  pl.BlockSpec((CONV_WIDTH, CONV_CH), lambda b, n: (0, 0)),
                  pl.BlockSpec((1, CONV_CH), lambda b, n: (0, 0)),
                  pl.BlockSpec((1, CONV_CH), lambda b, n: (0, 0)),
                  pl.BlockSpec((1, CONV_CH), lambda b, n: (0, 0))],
        out_specs=pl.BlockSpec((CONV_TS, CONV_CH), lambda b, n: (b * ns + n, 0)),
        scratch_shapes=[pltpu.VMEM((CONV_HALO + CONV_TS, CONV_CH), F32),
                        pltpu.VMEM((CONV_TS, CONV_CH), F32)],
        compiler_params=_cparams(("arbitrary", "arbitrary")),
        name="conformer_conv",
    )(z, dw_w, vec(dw_b), vec(ln_g), vec(ln_b))


ROUTER_TM = 256
HI16 = 0xFFFF0000


def _pack_halves(xb):
    n = xb.shape[1] // 2
    lo = lax.bitcast_convert_type(xb[:, :n].astype(F32), jnp.uint32)
    hi = lax.bitcast_convert_type(xb[:, n:].astype(F32), jnp.uint32)
    return (lo >> 16) | (hi & jnp.uint32(HI16))


def _unpack_halves(w):
    lo = lax.bitcast_convert_type(w << 16, F32).astype(BF16)
    hi = lax.bitcast_convert_type(w & jnp.uint32(HI16), F32).astype(BF16)
    return lo, hi


def _router_kernel(x_ref, g_ref, wr_ref, br_ref, h_ref, idx_ref, gate_ref):
    x = x_ref[...]
    ms = jnp.mean(x * x, axis=-1, keepdims=True)
    hb = (x * lax.rsqrt(ms + NORM_EPS) * g_ref[...]).astype(BF16)
    h_ref[...] = _pack_halves(hb)
    vals = jnp.dot(hb, wr_ref[...], preferred_element_type=F32) + br_ref[...]
    lane = lax.broadcasted_iota(jnp.int32, vals.shape, 1).astype(F32)
    tops, idxs = [], []
    for _ in range(TOP_K):
        m = jnp.max(vals, axis=-1, keepdims=True)
        idx = jnp.min(jnp.where(vals == m, lane, float(LANES)), axis=-1, keepdims=True)
        tops.append(m)
        idxs.append(idx)
        vals = jnp.where(lane == idx, -jnp.inf, vals)
    es = [jnp.exp(v - tops[0]) for v in tops]
    den = es[0] + es[1] + es[2] + es[3]
    idx_out = jnp.zeros(vals.shape, F32)
    gate_out = jnp.zeros(vals.shape, F32)
    for k in range(TOP_K):
        idx_out = jnp.where(lane == float(k), idxs[k], idx_out)
        gate_out = jnp.where(lane == float(k), es[k] / den, gate_out)
    idx_ref[...] = idx_out.astype(jnp.int32)
    gate_ref[...] = gate_out


def _router(x1, g, w_router, b_router):
    t, d = x1.shape
    tm = ROUTER_TM
    wr = jnp.zeros((d, LANES), BF16).at[:, :N_EXPERTS].set(w_router.astype(BF16))
    br = jnp.full((1, LANES), -1e30, F32).at[0, :N_EXPERTS].set(b_router)
    return pl.pallas_call(
        _router_kernel,
        out_shape=(jax.ShapeDtypeStruct((t, d // 2), jnp.uint32),
                   jax.ShapeDtypeStruct((t, LANES), jnp.int32),
                   jax.ShapeDtypeStruct((t, LANES), F32)),
        grid=(t // tm,),
        in_specs=[pl.BlockSpec((tm, d), lambda i: (i, 0)),
                  pl.BlockSpec((1, d), lambda i: (0, 0)),
                  pl.BlockSpec((d, LANES), lambda i: (0, 0)),
                  pl.BlockSpec((1, LANES), lambda i: (0, 0))],
        out_specs=(pl.BlockSpec((tm, d // 2), lambda i: (i, 0)),
                   pl.BlockSpec((tm, LANES), lambda i: (i, 0)),
                   pl.BlockSpec((tm, LANES), lambda i: (i, 0))),
        compiler_params=_cparams(("parallel",)),
        name="router",
    )(x1, g.reshape(1, d), wr, br)


GATHER_ROWS = 1024
GATHER_UNROLL = 8
GATHER_CHUNK = 128


def _gather_kernel(idx_ref, nxt_ref, src_hbm, o_ref, buf, sem):
    i = pl.program_id(0)
    nsteps = pl.num_programs(0)
    slot = lax.rem(i, 2)

    def issue(ids, s):
        def body(q, carry):
            for u in range(GATHER_UNROLL):
                r = q * GATHER_UNROLL + u
                pltpu.make_async_copy(src_hbm.at[pl.ds(ids[0, 0, r], 1), :],
                                      buf.at[s, pl.ds(r, 1), :], sem.at[s]).start(priority=u % 2)
            return carry
        lax.fori_loop(0, GATHER_ROWS // GATHER_UNROLL, body, 0)

    @pl.when(i == 0)
    def _():
        issue(idx_ref, 0)

    @pl.when(i + 1 < nsteps)
    def _():
        issue(nxt_ref, 1 - slot)

    pltpu.make_async_copy(src_hbm.at[pl.ds(0, GATHER_ROWS), :], buf.at[slot], sem.at[slot]).wait()
    half = buf.shape[2]
    for c in range(GATHER_ROWS // GATHER_CHUNK):
        rows = slice(c * GATHER_CHUNK, (c + 1) * GATHER_CHUNK)
        lo, hi = _unpack_halves(buf[slot, rows, :])
        o_ref[rows, 0:half] = lo
        o_ref[rows, half:2 * half] = hi


def _gather_rows(src, row_idx):
    n = row_idx.shape[0]
    d = src.shape[1]
    nsteps = n // GATHER_ROWS
    idx3 = row_idx.reshape(nsteps, 1, GATHER_ROWS)
    return pl.pallas_call(
        _gather_kernel,
        out_shape=jax.ShapeDtypeStruct((n, 2 * d), BF16),
        grid=(nsteps,),
        in_specs=[pl.BlockSpec((1, 1, GATHER_ROWS), lambda i: (i, 0, 0), memory_space=pltpu.SMEM),
                  pl.BlockSpec((1, 1, GATHER_ROWS), lambda i: (jnp.minimum(i + 1, nsteps - 1), 0, 0),
                               memory_space=pltpu.SMEM),
                  pl.BlockSpec(memory_space=pl.ANY)],
        out_specs=pl.BlockSpec((GATHER_ROWS, 2 * d), lambda i: (i, 0)),
        scratch_shapes=[pltpu.VMEM((2, GATHER_ROWS, d), src.dtype),
                        pltpu.SemaphoreType.DMA((2,))],
        compiler_params=_cparams(("arbitrary",)),
        name="dispatch_gather",
    )(idx3, idx3, src)


MOE_NBT = 10
MOE_TM = MOE_NBT * MOE_BLOCK
MOE_FAST = (7, 8, 9)
MOE_PIECE = 4
MOE_FC = 256
MOE_NC = 4 * MOE_FC
MOE_S1 = D_FF // MOE_FC
MOE_S2 = D_MODEL // MOE_NC
MOE_STEPS = MOE_S1 + MOE_S2
MOE_RING = 3
MOE_AHEAD = MOE_RING - 1
assert MOE_STEPS % MOE_RING == 0 and D_MODEL == 2 * D_FF


def _moe_max_blocks(t):
    return t * TOP_K // MOE_BLOCK + N_EXPERTS


def _moe_max_tiles(t):
    return (_moe_max_blocks(t) + N_EXPERTS * (MOE_NBT - 1)) // MOE_NBT + 1


def _moe_kernel(te_ref, tb_ref, tn_ref, nt_ref,
                xs_hbm, wgu_hbm, bgu_hbm, wd_hbm, bd_hbm, ys_hbm,
                xbuf, act, obuf, ring, bgu, bdn, xsem, osem, wsem, bsem):
    i = pl.program_id(0)
    ntiles = nt_ref[0]
    half = MOE_NC // 2

    def weight_copies(tile, s, slot):
        e = te_ref[tile]
        c1 = pl.multiple_of(jnp.minimum(s, MOE_S1 - 1) * MOE_FC, MOE_FC)
        c2 = pl.multiple_of(jnp.maximum(s - MOE_S1, 0) * MOE_NC, MOE_NC)
        gate_up = [pltpu.make_async_copy(wgu_hbm.at[e, :, pl.ds(g * D_FF + c1, MOE_FC)],
                                         ring.at[slot, :, pl.ds(g * MOE_FC, MOE_FC)], wsem.at[slot])
                   for g in range(2)]
        down = [pltpu.make_async_copy(wd_hbm.at[e, :, pl.ds(c2 + h * half, half)],
                                      ring.at[slot, pl.ds(h * D_FF, D_FF), :], wsem.at[slot])
                for h in range(2)]
        return gate_up, down

    def weights(tile, s, slot, op):
        s = jnp.asarray(s, jnp.int32)
        gate_up, down = weight_copies(tile, s, slot)

        @pl.when(s < MOE_S1)
        def _():
            for c in gate_up:
                op(c)

        @pl.when(s >= MOE_S1)
        def _():
            for c in down:
                op(c)

    def bias_copies(tile):
        e = te_ref[tile]
        ts = lax.rem(tile, 2)
        return [pltpu.make_async_copy(bgu_hbm.at[e], bgu.at[ts], bsem.at[ts]),
                pltpu.make_async_copy(bd_hbm.at[e], bdn.at[ts], bsem.at[ts])]

    start = lambda c: c.start()
    wait = lambda c: c.wait()

    def xs_copy(tile, row0, rows):
        src0 = pl.multiple_of(tb_ref[tile] * MOE_BLOCK + row0, MOE_BLOCK)
        return pltpu.make_async_copy(xs_hbm.at[pl.ds(src0, rows), :],
                                     xbuf.at[pl.ds(row0, rows), :], xsem.at[0])

    def for_row_pieces(nb, emit):
        general = nb >= 0
        for f in MOE_FAST:
            general = general & (nb != f)

            @pl.when(nb == f)
            def _():
                emit(0, f * MOE_BLOCK)

        @pl.when(general)
        def _():
            nbig = nb // MOE_PIECE
            big = MOE_PIECE * MOE_BLOCK

            def big_piece(q, carry):
                emit(pl.multiple_of(q * big, big), big)
                return carry

            def small_piece(b, carry):
                emit(pl.multiple_of(b * MOE_BLOCK, MOE_BLOCK), MOE_BLOCK)
                return carry

            lax.fori_loop(0, nbig, big_piece, 0)
            lax.fori_loop(nbig * MOE_PIECE, nb, small_piece, 0)

    def out_copy(slot, row0, rows, n):
        dst0 = pl.multiple_of(tb_ref[i] * MOE_BLOCK + row0, MOE_BLOCK)
        col0 = pl.multiple_of(n * MOE_NC, MOE_NC)
        return pltpu.make_async_copy(obuf.at[slot, pl.ds(row0, rows), :],
                                     ys_hbm.at[pl.ds(dst0, rows), pl.ds(col0, MOE_NC)],
                                     osem.at[slot])

    @pl.when(i < ntiles)
    def _():
        nb = tn_ref[i]
        tslot = lax.rem(i, 2)

        @pl.when(i == 0)
        def _():
            for_row_pieces(nb, lambda row0, rows: xs_copy(i, row0, rows).start())
            for c in bias_copies(i):
                c.start()
            for s0 in range(MOE_AHEAD):
                weights(i, s0, s0, start)

        for_row_pieces(nb, lambda row0, rows: xs_copy(i, row0, rows).wait())
        for c in bias_copies(i):
            c.wait()

        def step(s, carry):
            slot = lax.rem(s, MOE_RING)
            weights(i, s, slot, wait)

            ahead = s + MOE_AHEAD
            aslot = lax.rem(ahead, MOE_RING)

            @pl.when(ahead < MOE_STEPS)
            def _():
                weights(i, ahead, aslot, start)

            @pl.when((ahead >= MOE_STEPS) & (i + 1 < ntiles))
            def _():
                weights(i + 1, ahead - MOE_STEPS, aslot, start)

                @pl.when(ahead == MOE_STEPS)
                def _():
                    for c in bias_copies(i + 1):
                        c.start()

            @pl.when(s < MOE_S1)
            def _():
                c0 = pl.multiple_of(s * MOE_FC, MOE_FC)

                def gate_up(row0, rows):
                    x = xbuf[pl.ds(row0, rows), :]
                    wg = ring[slot, :, 0:MOE_FC].astype(BF16)
                    wl = ring[slot, :, MOE_FC:2 * MOE_FC].astype(BF16)
                    glu = jnp.dot(x, wg, preferred_element_type=F32) + bgu[tslot, :, pl.ds(c0, MOE_FC)]
                    lin = jnp.dot(x, wl, preferred_element_type=F32) + bgu[tslot, :, pl.ds(D_FF + c0, MOE_FC)]
                    glu = jnp.minimum(glu, SWIGLU_LIMIT)
                    lin = jnp.clip(lin, -SWIGLU_LIMIT, SWIGLU_LIMIT)
                    a = glu * jax.nn.sigmoid(SWIGLU_ALPHA * glu) * (lin + 1.0)
                    act[pl.ds(row0, rows), pl.ds(c0, MOE_FC)] = a.astype(BF16)

                for_row_pieces(nb, gate_up)

            @pl.when(s >= MOE_S1)
            def _():
                n = s - MOE_S1
                oslot = lax.rem(n, 2)

                @pl.when((s == MOE_S1) & (i + 1 < ntiles))
                def _():
                    for_row_pieces(tn_ref[i + 1], lambda row0, rows: xs_copy(i + 1, row0, rows).start())

                def down(row0, rows):
                    a = act[pl.ds(row0, rows), :]
                    for h in range(2):
                        wd = ring[slot, h * D_FF:(h + 1) * D_FF, :].astype(BF16)
                        bd = bdn[tslot, :, pl.ds(pl.multiple_of(n * MOE_NC + h * half, half), half)]
                        obuf[oslot, pl.ds(row0, rows), h * half:(h + 1) * half] = (
                            jnp.dot(a, wd, preferred_element_type=F32) + bd)
                    out_copy(oslot, row0, rows, n).start()

                for_row_pieces(nb, down)

                @pl.when(n > 0)
                def _():
                    for_row_pieces(nb, lambda row0, rows: out_copy(1 - oslot, row0, rows, n - 1).wait())

                @pl.when(n == MOE_S2 - 1)
                def _():
                    for_row_pieces(nb, lambda row0, rows: out_copy(oslot, row0, rows, n).wait())

            return carry

        lax.fori_loop(0, MOE_STEPS, step, 0)

    @pl.when(i == ntiles)
    def _():
        obuf[0, 0:MOE_BLOCK, :] = jnp.zeros((MOE_BLOCK, MOE_NC), F32)

        def fill(blk, carry):
            row0 = pl.multiple_of(blk * MOE_BLOCK, MOE_BLOCK)
            fills = [pltpu.make_async_copy(
                obuf.at[0, pl.ds(0, MOE_BLOCK), :],
                ys_hbm.at[pl.ds(row0, MOE_BLOCK), pl.ds(c * MOE_NC, MOE_NC)], osem.at[0])
                for c in range(MOE_S2)]
            for f in fills:
                f.start()
            for f in fills:
                f.wait()
            return carry

        lax.fori_loop(tb_ref[ntiles - 1] + tn_ref[ntiles - 1], ys_hbm.shape[0] // MOE_BLOCK, fill, 0)


def _moe_experts(xs, n_rows, tile_e, tile_b0, tile_nb, n_tiles, w_gate_up, b_gate_up, w_down, b_down):
    grid_spec = pltpu.PrefetchScalarGridSpec(
        num_scalar_prefetch=4,
        grid=(tile_e.shape[0],),
        in_specs=[pl.BlockSpec(memory_space=pl.ANY)] * 5,
        out_specs=pl.BlockSpec(memory_space=pl.ANY),
        scratch_shapes=[pltpu.VMEM((MOE_TM, D_MODEL), BF16),
                        pltpu.VMEM((MOE_TM, D_FF), BF16),
                        pltpu.VMEM((2, MOE_TM, MOE_NC), F32),
                        pltpu.VMEM((MOE_RING, D_MODEL, 2 * MOE_FC), F32),
                        pltpu.VMEM((2, 1, 2 * D_FF), F32),
                        pltpu.VMEM((2, 1, D_MODEL), F32),
                        pltpu.SemaphoreType.DMA((1,)),
                        pltpu.SemaphoreType.DMA((2,)),
                        pltpu.SemaphoreType.DMA((MOE_RING,)),
                        pltpu.SemaphoreType.DMA((2,))],
    )
    return pl.pallas_call(
        _moe_kernel,
        out_shape=jax.ShapeDtypeStruct((n_rows, D_MODEL), F32),
        grid_spec=grid_spec,
        compiler_params=_cparams(("arbitrary",)),
        name="moe_experts",
    )(tile_e, tile_b0, tile_nb, n_tiles, xs, w_gate_up, b_gate_up.reshape(N_EXPERTS, 1, 2 * D_FF),
      w_down, b_down.reshape(N_EXPERTS, 1, D_MODEL))


COMBINE_TT = 128


def _combine_kernel(pos_ref, nxt_ref, x_ref, gate_ref, ys_hbm, o_ref, buf, sem):
    i = pl.program_id(0)
    nsteps = pl.num_programs(0)
    slot = lax.rem(i, 2)

    def issue(ids, s):
        def body(r, carry):
            for k in range(TOP_K):
                pltpu.make_async_copy(ys_hbm.at[pl.ds(ids[0, 0, r * TOP_K + k], 1), :],
                                      buf.at[s, k, pl.ds(r, 1), :], sem.at[s]).start(priority=k % 2)
            return carry
        lax.fori_loop(0, COMBINE_TT, body, 0, unroll=4)

    @pl.when(i == 0)
    def _():
        issue(pos_ref, 0)

    @pl.when(i + 1 < nsteps)
    def _():
        issue(nxt_ref, 1 - slot)

    for k in range(TOP_K):
        pltpu.make_async_copy(ys_hbm.at[pl.ds(0, COMBINE_TT), :], buf.at[slot, k], sem.at[slot]).wait()
    gate = gate_ref[...]
    acc = x_ref[...]
    moe = gate[:, 0:1] * buf[slot, 0]
    for k in range(1, TOP_K):
        moe = moe + gate[:, k:k + 1] * buf[slot, k]
    o_ref[...] = acc + moe


def _combine(x1, gates, ys, pos):
    t, d = x1.shape
    tt = COMBINE_TT
    nsteps = t // tt
    pos3 = pos.reshape(nsteps, 1, tt * TOP_K)
    return pl.pallas_call(
        _combine_kernel,
        out_shape=jax.ShapeDtypeStruct((t, d), F32),
        grid=(nsteps,),
        in_specs=[pl.BlockSpec((1, 1, tt * TOP_K), lambda i: (i, 0, 0), memory_space=pltpu.SMEM),
                  pl.BlockSpec((1, 1, tt * TOP_K), lambda i: (jnp.minimum(i + 1, nsteps - 1), 0, 0),
                               memory_space=pltpu.SMEM),
                  pl.BlockSpec((tt, d), lambda i: (i, 0)),
                  pl.BlockSpec((tt, LANES), lambda i: (i, 0)),
                  pl.BlockSpec(memory_space=pl.ANY)],
        out_specs=pl.BlockSpec((tt, d), lambda i: (i, 0)),
        scratch_shapes=[pltpu.VMEM((2, TOP_K, tt, d), F32),
                        pltpu.SemaphoreType.DMA((2,))],
        compiler_params=_cparams(("arbitrary",)),
        name="combine",
    )(pos3, pos3, x1, gates, ys)


def _routing(top_idx):
    t = top_idx.shape[0]
    n_assign = t * TOP_K
    flat_e = top_idx.reshape(-1)
    experts = jnp.arange(N_EXPERTS, dtype=jnp.int32)
    order = jnp.argsort(flat_e)
    sorted_pos = jnp.argsort(order).astype(jnp.int32)
    counts = jnp.sum((flat_e[:, None] == experts[None, :]).astype(jnp.int32), axis=0)
    starts = jnp.cumsum(counts) - counts
    padded = (counts + MOE_BLOCK - 1) // MOE_BLOCK * MOE_BLOCK
    pad_ends = jnp.cumsum(padded)
    pad_starts = pad_ends - padded
    shift = pad_starts - starts
    pos = sorted_pos + shift[flat_e]
    n_rows = n_assign + N_EXPERTS * MOE_BLOCK
    n_rows = (n_rows + GATHER_ROWS - 1) // GATHER_ROWS * GATHER_ROWS
    blk_row0 = jnp.arange(n_rows // MOE_BLOCK, dtype=jnp.int32)[:, None] * MOE_BLOCK
    owns = (blk_row0 >= pad_starts[None, :]) & (blk_row0 < pad_ends[None, :])

    def per_block(table):
        return jnp.sum(jnp.where(owns, table[None, :], 0), axis=1, keepdims=True)

    rows = jnp.arange(n_rows, dtype=jnp.int32).reshape(-1, MOE_BLOCK)
    place = rows - per_block(shift)
    valid = (rows - per_block(pad_starts)) < per_block(counts)
    row_tok = jnp.where(valid, order[jnp.clip(place, 0, n_assign - 1)].astype(jnp.int32) // TOP_K, 0)
    row_tok = row_tok.reshape(-1)

    nblk = padded // MOE_BLOCK
    blk0 = pad_starts // MOE_BLOCK
    ntile = (nblk + MOE_NBT - 1) // MOE_NBT
    tile_end = jnp.cumsum(ntile)
    n_tiles = tile_end[-1]
    ids = jnp.arange(_moe_max_tiles(t), dtype=jnp.int32)
    ids_c = jnp.minimum(ids, n_tiles - 1)
    tile_e = jnp.minimum(jnp.searchsorted(tile_end, ids_c, side='right'), N_EXPERTS - 1).astype(jnp.int32)
    local = ids_c - (tile_end - ntile)[tile_e]
    tile_b0 = (blk0[tile_e] + local * MOE_NBT).astype(jnp.int32)
    tile_nb = jnp.where(ids < n_tiles, jnp.clip(nblk[tile_e] - local * MOE_NBT, 0, MOE_NBT), 0).astype(jnp.int32)
    return row_tok, pos, tile_e, tile_b0, tile_nb, n_tiles.reshape(1).astype(jnp.int32)


def _rope_tables(seq):
    f32 = np.float32
    inv_freq = f32(1.0) / (f32(ROPE_THETA) ** (np.arange(0, HEAD_DIM, 2, dtype=f32) / f32(HEAD_DIM)))
    ang = np.arange(seq, dtype=f32)[:, None] * inv_freq[None, :]
    cos = np.cos(ang).astype(f32)
    sin = np.sin(ang).astype(f32)
    reps = LANES // HEAD_DIM
    cos_t = np.tile(np.concatenate([cos, cos], axis=-1), (1, reps))
    sin_t = np.tile(np.concatenate([-sin, sin], axis=-1), (1, reps))
    return jnp.asarray(cos_t), jnp.asarray(sin_t)


def kernel(x, norm1_g, w_in, b_in, q_norm_g, k_norm_g, attn_sinks, w_attn_o, conv_dw_w, conv_dw_b,
           conv_ln_g, conv_ln_b, w_conv_o, w_out, norm2_g, w_router, b_router, w_gate_up, b_gate_up,
           w_down, b_down):
    b, s, d = x.shape
    t = b * s
    depth = norm1_g.shape[0]
    xt = x.reshape(t, d)
    cos_t, sin_t = _rope_tables(s)
    reps = LANES // HEAD_DIM
    c0 = QKV_WIDTH
    c1 = c0 + CONV_CH
    c2 = c1 + CONV_CH
    for l in range(depth):
        h = _rmsnorm(xt, norm1_g[l], BF16)
        qkv = _mm_bias(h, w_in, b_in, l, 0, c0, F32, tm=1024, tn=512, name="inproj_qkv")
        z = _mm_glu(h, w_in, b_in, l, c0, CONV_CH, tm=1024, tn=256)
        gates = _mm_bias(h, w_in, b_in, l, c2, 2 * d, BF16, tm=1024, tn=1024, sigmoid=True,
                         name="inproj_gates")
        attn = _attention(qkv, attn_sinks[l], cos_t, sin_t,
                          jnp.tile(q_norm_g[l], reps).reshape(1, LANES),
                          jnp.tile(k_norm_g[l], reps).reshape(1, LANES), b, s)
        conv = _conformer_conv(z, conv_dw_w[l], conv_dw_b[l], conv_ln_g[l], conv_ln_b[l], b, s)
        merged = _merge(attn, conv, w_attn_o, w_conv_o, l, gates, tm=1024, tn=512)
        x1 = _mm_residual(merged, w_out, l, xt, tm=1024, tn=512)
        h2, idx_pad, gate_pad = _router(x1, norm2_g[l], w_router[l], b_router[l])
        row_tok, pos, tile_e, tile_b0, tile_nb, n_tiles = _routing(idx_pad[:, :TOP_K])
        xs = _gather_rows(h2, row_tok)
        ys = _moe_experts(xs, _moe_max_blocks(t) * MOE_BLOCK, tile_e, tile_b0, tile_nb, n_tiles,
                          w_gate_up[l], b_gate_up[l], w_down[l], b_down[l])
        xt = _combine(x1, gate_pad, ys, pos)
    return xt.reshape(b, s, d)
```

```python
import functools

import jax
import jax.numpy as jnp
import numpy as np
from jax import lax
from jax.experimental import pallas as pl
from jax.experimental.pallas import tpu as pltpu

D_MODEL = 4096
HEAD_DIM = 64
N_Q_HEADS = 32
N_KV_HEADS = 4
WINDOW = 128
ATTN_BLOCK = 128
ROPE_THETA = 10000.0
CONV_CH = 2048
CONV_WIDTH = 31
N_EXPERTS = 32
TOP_K = 4
D_FF = 2048
SWIGLU_ALPHA = 1.702
SWIGLU_LIMIT = 7.0
MOE_BLOCK = 128
NORM_EPS = 1e-5

Q_WIDTH = N_Q_HEADS * HEAD_DIM
KV_WIDTH = N_KV_HEADS * HEAD_DIM
QKV_WIDTH = Q_WIDTH + 2 * KV_WIDTH

LANES = 128
SUBLANES = 8
VMEM_LIMIT = 56 * 1024 * 1024

BF16 = jnp.bfloat16
F32 = jnp.float32


def _cparams(sem, vmem=VMEM_LIMIT):
    return pltpu.CompilerParams(dimension_semantics=sem, vmem_limit_bytes=vmem)


def _rmsnorm_kernel(x_ref, g_ref, o_ref):
    x = x_ref[...]
    ms = jnp.mean(x * x, axis=-1, keepdims=True)
    o_ref[...] = (x * lax.rsqrt(ms + NORM_EPS) * g_ref[...]).astype(o_ref.dtype)


def _rmsnorm(x, g, out_dtype, tm=512):
    t, d = x.shape
    return pl.pallas_call(
        _rmsnorm_kernel,
        out_shape=jax.ShapeDtypeStruct((t, d), out_dtype),
        grid=(t // tm,),
        in_specs=[pl.BlockSpec((tm, d), lambda i: (i, 0)),
                  pl.BlockSpec((1, d), lambda i: (0, 0))],
        out_specs=pl.BlockSpec((tm, d), lambda i: (i, 0)),
        compiler_params=_cparams(("parallel",)),
        name="rmsnorm",
    )(x, g.reshape(1, d))


def _wspec(k, tn, layer, off=0):
    return pl.BlockSpec((None, k, tn), lambda j, i: (layer, 0, j + off))


def _cast_on_first_row_block(pairs):
    @pl.when(pl.program_id(1) == 0)
    def _():
        for w_ref, wb_ref in pairs:
            wb_ref[...] = w_ref[...].astype(BF16)


def _stage_weights(w_hbm, layer, cols, tn, stage, wbs, sem):
    j = pl.program_id(0)
    nj = pl.num_programs(0)

    def copies(jj):
        return [pltpu.make_async_copy(w_hbm.at[layer, :, pl.ds(pl.multiple_of(c + jj * tn, LANES), tn)],
                                      stage.at[q], sem.at[q]) for q, c in enumerate(cols)]

    @pl.when(pl.program_id(1) == 0)
    def _():
        @pl.when(j == 0)
        def _():
            for c in copies(j):
                c.start()

        for q, c in enumerate(copies(j)):
            c.wait()
            wbs[q][...] = stage[q].astype(BF16)

        @pl.when(j + 1 < nj)
        def _():
            for c in copies(j + 1):
                c.start()


def _mm_bias_kernel(a_ref, w_hbm, b_ref, o_ref, stage, wb, sem, *, layer, col0, tn, sigmoid):
    _stage_weights(w_hbm, layer, (col0,), tn, stage, (wb,), sem)
    acc = jnp.dot(a_ref[...], wb[...], preferred_element_type=F32) + b_ref[...]
    if sigmoid:
        acc = jax.nn.sigmoid(acc)
    o_ref[...] = acc.astype(o_ref.dtype)


def _mm_bias(a, w, b, layer, col0, n, out_dtype, *, tm, tn, sigmoid=False, name):
    m, k = a.shape
    return pl.pallas_call(
        functools.partial(_mm_bias_kernel, layer=layer, col0=col0, tn=tn, sigmoid=sigmoid),
        out_shape=jax.ShapeDtypeStruct((m, n), out_dtype),
        grid=(n // tn, m // tm),
        in_specs=[pl.BlockSpec((tm, k), lambda j, i: (i, 0)),
                  pl.BlockSpec(memory_space=pl.ANY),
                  pl.BlockSpec((1, tn), lambda j, i: (0, j))],
        out_specs=pl.BlockSpec((tm, tn), lambda j, i: (i, j)),
        scratch_shapes=[pltpu.VMEM((1, k, tn), F32), pltpu.VMEM((k, tn), BF16),
                        pltpu.SemaphoreType.DMA((1,))],
        compiler_params=_cparams(("arbitrary", "arbitrary")),
        name=name,
    )(a, w, b[layer, col0:col0 + n].reshape(1, n))


def _mm_glu_kernel(a_ref, w_hbm, ba_ref, bg_ref, o_ref, stage, wab, wgb, sem, *, layer, col0, n, tn):
    _stage_weights(w_hbm, layer, (col0, col0 + n), tn, stage, (wab, wgb), sem)
    a = a_ref[...]
    u = jnp.dot(a, wab[...], preferred_element_type=F32) + ba_ref[...]
    g = jnp.dot(a, wgb[...], preferred_element_type=F32) + bg_ref[...]
    o_ref[...] = (u * jax.nn.sigmoid(g)).astype(o_ref.dtype)


def _mm_glu(a, w, b, layer, col0, n, *, tm, tn):
    m, k = a.shape
    bu = b[layer, col0:col0 + n].reshape(1, n)
    bg = b[layer, col0 + n:col0 + 2 * n].reshape(1, n)
    return pl.pallas_call(
        functools.partial(_mm_glu_kernel, layer=layer, col0=col0, n=n, tn=tn),
        out_shape=jax.ShapeDtypeStruct((m, n), F32),
        grid=(n // tn, m // tm),
        in_specs=[pl.BlockSpec((tm, k), lambda j, i: (i, 0)),
                  pl.BlockSpec(memory_space=pl.ANY),
                  pl.BlockSpec((1, tn), lambda j, i: (0, j)), pl.BlockSpec((1, tn), lambda j, i: (0, j))],
        out_specs=pl.BlockSpec((tm, tn), lambda j, i: (i, j)),
        scratch_shapes=[pltpu.VMEM((2, k, tn), F32), pltpu.VMEM((k, tn), BF16), pltpu.VMEM((k, tn), BF16),
                        pltpu.SemaphoreType.DMA((2,))],
        compiler_params=_cparams(("arbitrary", "arbitrary")),
        name="inproj_glu",
    )(a, w, bu, bg)


def _merge_kernel(a_ref, c_ref, wa_ref, wc_ref, ga_ref, gc_ref, o_ref, wab, wcb):
    _cast_on_first_row_block([(wa_ref, wab), (wc_ref, wcb)])
    pa = jnp.dot(a_ref[...], wab[...], preferred_element_type=F32)
    pc = jnp.dot(c_ref[...], wcb[...], preferred_element_type=F32)
    o_ref[...] = (ga_ref[...].astype(F32) * pa + gc_ref[...].astype(F32) * pc).astype(o_ref.dtype)


def _merge(attn, conv, wa, wc, layer, gates, *, tm, tn):
    m, k = attn.shape
    n = wa.shape[2]
    nj = n // tn
    return pl.pallas_call(
        _merge_kernel,
        out_shape=jax.ShapeDtypeStruct((m, n), BF16),
        grid=(nj, m // tm),
        in_specs=[pl.BlockSpec((tm, k), lambda j, i: (i, 0)),
                  pl.BlockSpec((tm, k), lambda j, i: (i, 0)),
                  _wspec(k, tn, layer), _wspec(k, tn, layer),
                  pl.BlockSpec((tm, tn), lambda j, i: (i, j)),
                  pl.BlockSpec((tm, tn), lambda j, i: (i, j + nj))],
        out_specs=pl.BlockSpec((tm, tn), lambda j, i: (i, j)),
        scratch_shapes=[pltpu.VMEM((k, tn), BF16), pltpu.VMEM((k, tn), BF16)],
        compiler_params=_cparams(("arbitrary", "arbitrary")),
        name="merge",
    )(attn, conv, wa, wc, gates, gates)


def _mm_residual_kernel(a_ref, w_ref, x_ref, o_ref, wb):
    _cast_on_first_row_block([(w_ref, wb)])
    o_ref[...] = x_ref[...] + jnp.dot(a_ref[...], wb[...], preferred_element_type=F32)


def _mm_residual(a, w, layer, x, *, tm, tn):
    m, k = a.shape
    n = w.shape[2]
    return pl.pallas_call(
        _mm_residual_kernel,
        out_shape=jax.ShapeDtypeStruct((m, n), F32),
        grid=(n // tn, m // tm),
        in_specs=[pl.BlockSpec((tm, k), lambda j, i: (i, 0)),
                  _wspec(k, tn, layer),
                  pl.BlockSpec((tm, tn), lambda j, i: (i, j))],
        out_specs=pl.BlockSpec((tm, tn), lambda j, i: (i, j)),
        scratch_shapes=[pltpu.VMEM((k, tn), BF16)],
        compiler_params=_cparams(("arbitrary", "arbitrary")),
        name="outproj",
    )(a, w, x)


ATTN_RC = 32


def _attn_kernel(sink_ref, q_ref, k_ref, v_ref, cos_ref, sin_ref, qg_ref, kg_ref, o_ref,
                 kwin, vwin, kb, vb, qs, bias_scr, s_scr, p_scr, inv_scr):
    n = pl.program_id(1)
    blk = ATTN_BLOCK

    @pl.when(n == 0)
    def _():
        kwin[0:blk, :] = jnp.zeros((blk, KV_WIDTH), F32)
        vwin[0:blk, :] = jnp.zeros((blk, KV_WIDTH), F32)

    cos = cos_ref[...]
    sin = sin_ref[...]
    lane = lax.broadcasted_iota(jnp.int32, (blk, LANES), 1)
    first_half = (lane % HEAD_DIM) < (HEAD_DIM // 2)
    seg_r = lax.broadcasted_iota(jnp.int32, (LANES, LANES), 0) // HEAD_DIM
    seg_c = lax.broadcasted_iota(jnp.int32, (LANES, LANES), 1) // HEAD_DIM
    seg = jnp.where(seg_r == seg_c, 1.0, 0.0).astype(BF16)

    def norm_rope(t, g):
        t2 = t * t
        hi = t2.astype(BF16)
        lo = (t2 - hi.astype(F32)).astype(BF16)
        ss = (jnp.dot(hi, seg, preferred_element_type=F32)
              + jnp.dot(lo, seg, preferred_element_type=F32))
        tn = t * lax.rsqrt(ss * (1.0 / HEAD_DIM) + NORM_EPS) * g
        rot = jnp.where(first_half, pltpu.roll(tn, LANES - HEAD_DIM // 2, 1),
                        pltpu.roll(tn, HEAD_DIM // 2, 1))
        return tn * cos + rot * sin

    kg = kg_ref[...]
    qg = qg_ref[...]
    for c in range(KV_WIDTH // LANES):
        cols = slice(c * LANES, (c + 1) * LANES)
        kwin[blk:2 * blk, cols] = norm_rope(k_ref[:, cols], kg)
    vwin[blk:2 * blk, :] = v_ref[...]

    lane2 = lax.broadcasted_iota(jnp.int32, (2 * blk, LANES), 1)
    lo_half = lane2 < HEAD_DIM
    for c in range(KV_WIDTH // LANES):
        cols = slice(c * LANES, (c + 1) * LANES)
        for win, dst in ((kwin, kb), (vwin, vb)):
            x = win[:, cols]
            xs = pltpu.roll(x, HEAD_DIM, 1)
            dst[2 * c, 0:2 * blk, :] = jnp.where(lo_half, x, 0.0).astype(BF16)
            dst[2 * c, 2 * blk:4 * blk, :] = jnp.where(lo_half, 0.0, xs).astype(BF16)
            dst[2 * c + 1, 0:2 * blk, :] = jnp.where(lo_half, xs, 0.0).astype(BF16)
            dst[2 * c + 1, 2 * blk:4 * blk, :] = jnp.where(lo_half, 0.0, x).astype(BF16)

    qi = lax.broadcasted_iota(jnp.int32, (blk, 4 * blk), 0)
    kj = lax.broadcasted_iota(jnp.int32, (blk, 4 * blk), 1) % (2 * blk)
    rel = qi + blk - kj
    first_key = jnp.where(n == 0, blk, 0)
    mask = (rel >= 0) & (rel < WINDOW) & (kj >= first_key)
    bias_scr[...] = jnp.where(mask, 0.0, -jnp.inf)

    pairs_per_kv = N_Q_HEADS // N_KV_HEADS // 2
    for p in range(N_Q_HEADS // 2):
        g, pp = divmod(p, pairs_per_kv)
        q2 = norm_rope(q_ref[:, p * LANES:(p + 1) * LANES], qg) * (HEAD_DIM ** -0.5)
        qs[g, pp * blk:(pp + 1) * blk, :] = q2.astype(BF16)

    lo_out = lax.broadcasted_iota(jnp.int32, (ATTN_RC, LANES), 1) < HEAD_DIM
    for g in range(N_KV_HEADS):
        s_scr[g] = lax.dot_general(qs[g], kb[g], (((1,), (1,)), ((), ())), preferred_element_type=F32)
        for r in range(pairs_per_kv * blk // ATTN_RC):
            rows = slice(r * ATTN_RC, (r + 1) * ATTN_RC)
            pp, q0 = divmod(r * ATTN_RC, blk)
            s = s_scr[g, rows, :] + bias_scr[q0:q0 + ATTN_RC, :]
            invs = []
            for h in range(2):
                keys = slice(h * 2 * blk, (h + 1) * 2 * blk)
                sh = s[:, keys]
                sink = sink_ref[2 * (g * pairs_per_kv + pp) + h]
                m = jnp.maximum(jnp.max(sh, axis=-1, keepdims=True), sink)
                e = jnp.exp(sh - m)
                den = jnp.sum(e, axis=-1, keepdims=True) + jnp.exp(sink - m)
                p_scr[g, rows, keys] = e.astype(BF16)
                invs.append(1.0 / den)
            inv_scr[g, rows, :] = jnp.where(lo_out, invs[0], invs[1])
        o = jnp.dot(p_scr[g], vb[g], preferred_element_type=F32) * inv_scr[g]
        for pp in range(pairs_per_kv):
            p = g * pairs_per_kv + pp
            o_ref[:, p * LANES:(p + 1) * LANES] = o[pp * blk:(pp + 1) * blk, :].astype(o_ref.dtype)

    kwin[0:blk, :] = kwin[blk:2 * blk, :]
    vwin[0:blk, :] = vwin[blk:2 * blk, :]


def _attention(qkv, sinks, cos_t, sin_t, qg, kg, batch, seq):
    t = qkv.shape[0]
    blk = ATTN_BLOCK
    nb = seq // blk
    kcol = Q_WIDTH // KV_WIDTH
    return pl.pallas_call(
        _attn_kernel,
        out_shape=jax.ShapeDtypeStruct((t, Q_WIDTH), BF16),
        grid=(batch, nb),
        in_specs=[pl.BlockSpec(memory_space=pltpu.SMEM),
                  pl.BlockSpec((blk, Q_WIDTH), lambda b, n: (b * nb + n, 0)),
                  pl.BlockSpec((blk, KV_WIDTH), lambda b, n: (b * nb + n, kcol)),
                  pl.BlockSpec((blk, KV_WIDTH), lambda b, n: (b * nb + n, kcol + 1)),
                  pl.BlockSpec((blk, LANES), lambda b, n: (n, 0)),
                  pl.BlockSpec((blk, LANES), lambda b, n: (n, 0)),
                  pl.BlockSpec((1, LANES), lambda b, n: (0, 0)),
                  pl.BlockSpec((1, LANES), lambda b, n: (0, 0))],
        out_specs=pl.BlockSpec((blk, Q_WIDTH), lambda b, n: (b * nb + n, 0)),
        scratch_shapes=[pltpu.VMEM((2 * blk, KV_WIDTH), F32),
                        pltpu.VMEM((2 * blk, KV_WIDTH), F32),
                        pltpu.VMEM((N_KV_HEADS, 4 * blk, LANES), BF16),
                        pltpu.VMEM((N_KV_HEADS, 4 * blk, LANES), BF16),
                        pltpu.VMEM((N_KV_HEADS, 4 * blk, LANES), BF16),
                        pltpu.VMEM((blk, 4 * blk), F32),
                        pltpu.VMEM((N_KV_HEADS, 4 * blk, 4 * blk), F32),
                        pltpu.VMEM((N_KV_HEADS, 4 * blk, 4 * blk), BF16),
                        pltpu.VMEM((N_KV_HEADS, 4 * blk, LANES), F32)],
        compiler_params=_cparams(("arbitrary", "arbitrary")),
        name="swa_attention",
    )(sinks, qkv, qkv, qkv, cos_t, sin_t, qg, kg)


CONV_TS = 256
CONV_HALO = 32
CONV_RB = 128


def _conv_kernel(z_ref, w_ref, b_ref, lg_ref, lb_ref, o_ref, zbuf, cbuf):
    n = pl.program_id(1)

    @pl.when(n == 0)
    def _():
        zbuf[0:CONV_HALO, :] = jnp.zeros((CONV_HALO, CONV_CH), F32)

    zbuf[CONV_HALO:CONV_HALO + CONV_TS, :] = z_ref[...]
    shift = CONV_HALO - (CONV_WIDTH - 1)

    def chunk(c, carry):
        c0 = pl.multiple_of(c * LANES, LANES)
        for rb in range(CONV_TS // CONV_RB):
            base = rb * CONV_RB
            acc = jnp.broadcast_to(b_ref[:, pl.ds(c0, LANES)], (CONV_RB, LANES))
            for res in range(SUBLANES):
                extra = SUBLANES if res else 0
                part = None
                for off in range(shift, shift + CONV_WIDTH):
                    if off % SUBLANES != res:
                        continue
                    r0 = base + off - res
                    term = (zbuf[r0:r0 + CONV_RB + extra, pl.ds(c0, LANES)]
                            * w_ref[off - shift:off - shift + 1, pl.ds(c0, LANES)])
                    part = term if part is None else part + term
                acc = acc + part[res:res + CONV_RB]
            cbuf[base:base + CONV_RB, pl.ds(c0, LANES)] = acc
        return carry

    lax.fori_loop(0, CONV_CH // LANES, chunk, 0)

    zbuf[0:CONV_HALO, :] = zbuf[CONV_TS:CONV_TS + CONV_HALO, :]

    y = cbuf[...]
    mu = jnp.mean(y, axis=-1, keepdims=True)
    yc = y - mu
    var = jnp.mean(yc * yc, axis=-1, keepdims=True)
    yn = yc * lax.rsqrt(var + NORM_EPS) * lg_ref[...] + lb_ref[...]
    o_ref[...] = (yn * jax.nn.sigmoid(yn)).astype(o_ref.dtype)


def _conformer_conv(z, dw_w, dw_b, ln_g, ln_b, batch, seq):
    t = z.shape[0]
    ns = seq // CONV_TS
    vec = lambda a: a.reshape(1, CONV_CH)
    return pl.pallas_call(
        _conv_kernel,
        out_shape=jax.ShapeDtypeStruct((t, CONV_CH), BF16),
        grid=(batch, ns),
        in_specs=[pl.BlockSpec((CONV_TS, CONV_CH), lambda b, n: (b * ns + n, 0)),
                  pl.BlockSpec((CONV_WIDTH, CONV_CH), lambda b, n: (0, 0)),
                  pl.BlockSpec((1, CONV_CH), lambda b, n: (0, 0)),
                  pl.BlockSpec((1, CONV_CH), lambda b, n: (0, 0)),
                  pl.BlockSpec((1, CONV_CH), lambda b, n: (0, 0))],
        out_specs=pl.BlockSpec((CONV_TS, CONV_CH), lambda b, n: (b * ns + n, 0)),
        scratch_shapes=[pltpu.VMEM((CONV_HALO + CONV_TS, CONV_CH), F32),
                        pltpu.VMEM((CONV_TS, CONV_CH), F32)],
        compiler_params=_cparams(("arbitrary", "arbitrary")),
        name="conformer_conv",
    )(z, dw_w, vec(dw_b), vec(ln_g), vec(ln_b))


ROUTER_TM = 256
HI16 = 0xFFFF0000


def _pack_halves(xb):
    n = xb.shape[1] // 2
    lo = lax.bitcast_convert_type(xb[:, :n].astype(F32), jnp.uint32)
    hi = lax.bitcast_convert_type(xb[:, n:].astype(F32), jnp.uint32)
    return (lo >> 16) | (hi & jnp.uint32(HI16))


def _unpack_halves(w):
    lo = lax.bitcast_convert_type(w << 16, F32).astype(BF16)
    hi = lax.bitcast_convert_type(w & jnp.uint32(HI16), F32).astype(BF16)
    return lo, hi


def _router_kernel(x_ref, g_ref, wr_ref, br_ref, h_ref, idx_ref, gate_ref):
    x = x_ref[...]
    ms = jnp.mean(x * x, axis=-1, keepdims=True)
    hb = (x * lax.rsqrt(ms + NORM_EPS) * g_ref[...]).astype(BF16)
    packed = _pack_halves(hb)
    wpt = packed.shape[1] // LANES
    for c in range(wpt):
        h_ref[pl.ds(c, packed.shape[0], stride=wpt), :] = packed[:, c * LANES:(c + 1) * LANES]
    vals = jnp.dot(hb, wr_ref[...], preferred_element_type=F32) + br_ref[...]
    lane = lax.broadcasted_iota(jnp.int32, vals.shape, 1).astype(F32)
    tops, idxs = [], []
    for _ in range(TOP_K):
        m = jnp.max(vals, axis=-1, keepdims=True)
        idx = jnp.min(jnp.where(vals == m, lane, float(LANES)), axis=-1, keepdims=True)
        tops.append(m)
        idxs.append(idx)
        vals = jnp.where(lane == idx, -jnp.inf, vals)
    es = [jnp.exp(v - tops[0]) for v in tops]
    den = es[0] + es[1] + es[2] + es[3]
    idx_out = jnp.zeros(vals.shape, F32)
    gate_out = jnp.zeros(vals.shape, F32)
    for k in range(TOP_K):
        idx_out = jnp.where(lane == float(k), idxs[k], idx_out)
        gate_out = jnp.where(lane == float(k), es[k] / den, gate_out)
    idx_ref[...] = idx_out.astype(jnp.int32)
    gate_ref[...] = gate_out


def _router(x1, g, w_router, b_router):
    t, d = x1.shape
    tm = ROUTER_TM
    wr = jnp.zeros((d, LANES), BF16).at[:, :N_EXPERTS].set(w_router.astype(BF16))
    br = jnp.full((1, LANES), -1e30, F32).at[0, :N_EXPERTS].set(b_router)
    return pl.pallas_call(
        _router_kernel,
        out_shape=(jax.ShapeDtypeStruct((t * (d // 2 // LANES), LANES), jnp.uint32),
                   jax.ShapeDtypeStruct((t, LANES), jnp.int32),
                   jax.ShapeDtypeStruct((t, LANES), F32)),
        grid=(t // tm,),
        in_specs=[pl.BlockSpec((tm, d), lambda i: (i, 0)),
                  pl.BlockSpec((1, d), lambda i: (0, 0)),
                  pl.BlockSpec((d, LANES), lambda i: (0, 0)),
                  pl.BlockSpec((1, LANES), lambda i: (0, 0))],
        out_specs=(pl.BlockSpec((tm * (d // 2 // LANES), LANES), lambda i: (i, 0)),
                   pl.BlockSpec((tm, LANES), lambda i: (i, 0)),
                   pl.BlockSpec((tm, LANES), lambda i: (i, 0))),
        compiler_params=_cparams(("parallel",)),
        name="router",
    )(x1, g.reshape(1, d), wr, br)


GATHER_ROWS = 1024
GATHER_UNROLL = 8
GATHER_CHUNK = 128
GATHER_WPT = D_MODEL // 2 // LANES


def _gather_kernel(idx_ref, nxt_ref, src_hbm, o_ref, buf, sem):
    i = pl.program_id(0)
    nsteps = pl.num_programs(0)
    slot = lax.rem(i, 2)

    def issue(ids, s):
        def body(q, carry):
            for u in range(GATHER_UNROLL):
                r = q * GATHER_UNROLL + u
                src0 = pl.multiple_of(ids[0, 0, r] * GATHER_WPT, GATHER_WPT)
                dst0 = pl.multiple_of(r * GATHER_WPT, GATHER_WPT)
                pltpu.make_async_copy(src_hbm.at[pl.ds(src0, GATHER_WPT), :],
                                      buf.at[s, pl.ds(dst0, GATHER_WPT), :], sem.at[s]).start(priority=u % 2)
            return carry
        lax.fori_loop(0, GATHER_ROWS // GATHER_UNROLL, body, 0)

    @pl.when(i == 0)
    def _():
        issue(idx_ref, 0)

    @pl.when(i + 1 < nsteps)
    def _():
        issue(nxt_ref, 1 - slot)

    pltpu.make_async_copy(src_hbm.at[pl.ds(0, GATHER_ROWS * GATHER_WPT), :], buf.at[slot], sem.at[slot]).wait()
    half = GATHER_WPT * LANES
    for ch in range(GATHER_ROWS // GATHER_CHUNK):
        rows = slice(ch * GATHER_CHUNK, (ch + 1) * GATHER_CHUNK)
        for c in range(GATHER_WPT):
            w = buf[slot, pl.ds(ch * GATHER_CHUNK * GATHER_WPT + c, GATHER_CHUNK, stride=GATHER_WPT), :]
            lo, hi = _unpack_halves(w)
            o_ref[rows, c * LANES:(c + 1) * LANES] = lo
            o_ref[rows, half + c * LANES:half + (c + 1) * LANES] = hi


def _gather_rows(src, row_idx):
    n = row_idx.shape[0]
    d = GATHER_WPT * LANES
    nsteps = n // GATHER_ROWS
    idx3 = row_idx.reshape(nsteps, 1, GATHER_ROWS)
    return pl.pallas_call(
        _gather_kernel,
        out_shape=jax.ShapeDtypeStruct((n, 2 * d), BF16),
        grid=(nsteps,),
        in_specs=[pl.BlockSpec((1, 1, GATHER_ROWS), lambda i: (i, 0, 0), memory_space=pltpu.SMEM),
                  pl.BlockSpec((1, 1, GATHER_ROWS), lambda i: (jnp.minimum(i + 1, nsteps - 1), 0, 0),
                               memory_space=pltpu.SMEM),
                  pl.BlockSpec(memory_space=pl.ANY)],
        out_specs=pl.BlockSpec((GATHER_ROWS, 2 * d), lambda i: (i, 0)),
        scratch_shapes=[pltpu.VMEM((2, GATHER_ROWS * GATHER_WPT, LANES), src.dtype),
                        pltpu.SemaphoreType.DMA((2,))],
        compiler_params=_cparams(("arbitrary",)),
        name="dispatch_gather",
    )(idx3, idx3, src)


MOE_NBT = 10
MOE_TM = MOE_NBT * MOE_BLOCK
MOE_FAST = (7, 8, 9)
MOE_PIECE = 4
MOE_FC = 256
MOE_NC = 4 * MOE_FC
MOE_S1 = D_FF // MOE_FC
MOE_S2 = D_MODEL // MOE_NC
MOE_STEPS = MOE_S1 + MOE_S2
MOE_RING = 3
MOE_AHEAD = MOE_RING - 1
assert MOE_STEPS % MOE_RING == 0 and D_MODEL == 2 * D_FF


def _moe_max_blocks(t):
    return t * TOP_K // MOE_BLOCK + N_EXPERTS


def _moe_max_tiles(t):
    return (_moe_max_blocks(t) + N_EXPERTS * (MOE_NBT - 1)) // MOE_NBT + 1


def _moe_kernel(te_ref, tb_ref, tn_ref, nt_ref,
                xs_hbm, wgu_hbm, bgu_hbm, wd_hbm, bd_hbm, ys_hbm,
                xbuf, act, obuf, ring, bgu, bdn, xsem, osem, wsem, bsem):
    i = pl.program_id(0)
    ntiles = nt_ref[0]
    half = MOE_NC // 2

    def weight_copies(tile, s, slot):
        e = te_ref[tile]
        c1 = pl.multiple_of(jnp.minimum(s, MOE_S1 - 1) * MOE_FC, MOE_FC)
        c2 = pl.multiple_of(jnp.maximum(s - MOE_S1, 0) * MOE_NC, MOE_NC)
        gate_up = [pltpu.make_async_copy(wgu_hbm.at[e, :, pl.ds(g * D_FF + c1, MOE_FC)],
                                         ring.at[slot, :, pl.ds(g * MOE_FC, MOE_FC)], wsem.at[slot])
                   for g in range(2)]
        down = [pltpu.make_async_copy(wd_hbm.at[e, :, pl.ds(c2 + h * half, half)],
                                      ring.at[slot, pl.ds(h * D_FF, D_FF), :], wsem.at[slot])
                for h in range(2)]
        return gate_up, down

    def weights(tile, s, slot, op):
        s = jnp.asarray(s, jnp.int32)
        gate_up, down = weight_copies(tile, s, slot)

        @pl.when(s < MOE_S1)
        def _():
            for c in gate_up:
                op(c)

        @pl.when(s >= MOE_S1)
        def _():
            for c in down:
                op(c)

    def bias_copies(tile):
        e = te_ref[tile]
        ts = lax.rem(tile, 2)
        return [pltpu.make_async_copy(bgu_hbm.at[e], bgu.at[ts], bsem.at[ts]),
                pltpu.make_async_copy(bd_hbm.at[e], bdn.at[ts], bsem.at[ts])]

    start = lambda c: c.start()
    wait = lambda c: c.wait()

    def xs_copy(tile, row0, rows):
        src0 = pl.multiple_of(tb_ref[tile] * MOE_BLOCK + row0, MOE_BLOCK)
        return pltpu.make_async_copy(xs_hbm.at[pl.ds(src0, rows), :],
                                     xbuf.at[pl.ds(row0, rows), :], xsem.at[0])

    def for_row_pieces(nb, emit):
        general = nb >= 0
        for f in MOE_FAST:
            general = general & (nb != f)

            @pl.when(nb == f)
            def _():
                emit(0, f * MOE_BLOCK)

        @pl.when(general)
        def _():
            nbig = nb // MOE_PIECE
            big = MOE_PIECE * MOE_BLOCK

            def big_piece(q, carry):
                emit(pl.multiple_of(q * big, big), big)
                return carry

            def small_piece(b, carry):
                emit(pl.multiple_of(b * MOE_BLOCK, MOE_BLOCK), MOE_BLOCK)
                return carry

            lax.fori_loop(0, nbig, big_piece, 0)
            lax.fori_loop(nbig * MOE_PIECE, nb, small_piece, 0)

    def out_copy(slot, row0, rows, n):
        dst0 = pl.multiple_of(tb_ref[i] * MOE_BLOCK + row0, MOE_BLOCK)
        col0 = pl.multiple_of(n * MOE_NC, MOE_NC)
        return pltpu.make_async_copy(obuf.at[slot, pl.ds(row0, rows), :],
                                     ys_hbm.at[pl.ds(dst0, rows), pl.ds(col0, MOE_NC)],
                                     osem.at[slot])

    @pl.when(i < ntiles)
    def _():
        nb = tn_ref[i]
        tslot = lax.rem(i, 2)

        @pl.when(i == 0)
        def _():
            for_row_pieces(nb, lambda row0, rows: xs_copy(i, row0, rows).start())
            for c in bias_copies(i):
                c.start()
            for s0 in range(MOE_AHEAD):
                weights(i, s0, s0, start)

        for_row_pieces(nb, lambda row0, rows: xs_copy(i, row0, rows).wait())
        for c in bias_copies(i):
            c.wait()

        def step(s, carry):
            slot = lax.rem(s, MOE_RING)
            weights(i, s, slot, wait)

            ahead = s + MOE_AHEAD
            aslot = lax.rem(ahead, MOE_RING)

            @pl.when(ahead < MOE_STEPS)
            def _():
                weights(i, ahead, aslot, start)

            @pl.when((ahead >= MOE_STEPS) & (i + 1 < ntiles))
            def _():
                weights(i + 1, ahead - MOE_STEPS, aslot, start)

                @pl.when(ahead == MOE_STEPS)
                def _():
                    for c in bias_copies(i + 1):
                        c.start()

            @pl.when(s < MOE_S1)
            def _():
                c0 = pl.multiple_of(s * MOE_FC, MOE_FC)

                def gate_up(row0, rows):
                    x = xbuf[pl.ds(row0, rows), :]
                    wg = ring[slot, :, 0:MOE_FC].astype(BF16)
                    wl = ring[slot, :, MOE_FC:2 * MOE_FC].astype(BF16)
                    glu = jnp.dot(x, wg, preferred_element_type=F32) + bgu[tslot, :, pl.ds(c0, MOE_FC)]
                    lin = jnp.dot(x, wl, preferred_element_type=F32) + bgu[tslot, :, pl.ds(D_FF + c0, MOE_FC)]
                    glu = jnp.minimum(glu, SWIGLU_LIMIT)
                    lin = jnp.clip(lin, -SWIGLU_LIMIT, SWIGLU_LIMIT)
                    a = glu * jax.nn.sigmoid(SWIGLU_ALPHA * glu) * (lin + 1.0)
                    act[pl.ds(row0, rows), pl.ds(c0, MOE_FC)] = a.astype(BF16)

                for_row_pieces(nb, gate_up)

            @pl.when(s >= MOE_S1)
            def _():
                n = s - MOE_S1
                oslot = lax.rem(n, 2)

                @pl.when((s == MOE_S1) & (i + 1 < ntiles))
                def _():
                    for_row_pieces(tn_ref[i + 1], lambda row0, rows: xs_copy(i + 1, row0, rows).start())

                def down(row0, rows):
                    a = act[pl.ds(row0, rows), :]
                    for h in range(2):
                        wd = ring[slot, h * D_FF:(h + 1) * D_FF, :].astype(BF16)
                        bd = bdn[tslot, :, pl.ds(pl.multiple_of(n * MOE_NC + h * half, half), half)]
                        obuf[oslot, pl.ds(row0, rows), h * half:(h + 1) * half] = (
                            jnp.dot(a, wd, preferred_element_type=F32) + bd)
                    out_copy(oslot, row0, rows, n).start()

                for_row_pieces(nb, down)

                @pl.when(n > 0)
                def _():
                    for_row_pieces(nb, lambda row0, rows: out_copy(1 - oslot, row0, rows, n - 1).wait())

                @pl.when(n == MOE_S2 - 1)
                def _():
                    for_row_pieces(nb, lambda row0, rows: out_copy(oslot, row0, rows, n).wait())

            return carry

        lax.fori_loop(0, MOE_STEPS, step, 0)

    @pl.when(i == ntiles)
    def _():
        obuf[0, 0:MOE_BLOCK, :] = jnp.zeros((MOE_BLOCK, MOE_NC), F32)

        def fill(blk, carry):
            row0 = pl.multiple_of(blk * MOE_BLOCK, MOE_BLOCK)
            fills = [pltpu.make_async_copy(
                obuf.at[0, pl.ds(0, MOE_BLOCK), :],
                ys_hbm.at[pl.ds(row0, MOE_BLOCK), pl.ds(c * MOE_NC, MOE_NC)], osem.at[0])
                for c in range(MOE_S2)]
            for f in fills:
                f.start()
            for f in fills:
                f.wait()
            return carry

        lax.fori_loop(tb_ref[ntiles - 1] + tn_ref[ntiles - 1], ys_hbm.shape[0] // MOE_BLOCK, fill, 0)


def _moe_experts(xs, n_rows, tile_e, tile_b0, tile_nb, n_tiles, w_gate_up, b_gate_up, w_down, b_down):
    grid_spec = pltpu.PrefetchScalarGridSpec(
        num_scalar_prefetch=4,
        grid=(tile_e.shape[0],),
        in_specs=[pl.BlockSpec(memory_space=pl.ANY)] * 5,
        out_specs=pl.BlockSpec(memory_space=pl.ANY),
        scratch_shapes=[pltpu.VMEM((MOE_TM, D_MODEL), BF16),
                        pltpu.VMEM((MOE_TM, D_FF), BF16),
                        pltpu.VMEM((2, MOE_TM, MOE_NC), F32),
                        pltpu.VMEM((MOE_RING, D_MODEL, 2 * MOE_FC), F32),
                        pltpu.VMEM((2, 1, 2 * D_FF), F32),
                        pltpu.VMEM((2, 1, D_MODEL), F32),
                        pltpu.SemaphoreType.DMA((1,)),
                        pltpu.SemaphoreType.DMA((2,)),
                        pltpu.SemaphoreType.DMA((MOE_RING,)),
                        pltpu.SemaphoreType.DMA((2,))],
    )
    return pl.pallas_call(
        _moe_kernel,
        out_shape=jax.ShapeDtypeStruct((n_rows, D_MODEL), F32),
        grid_spec=grid_spec,
        compiler_params=_cparams(("arbitrary",)),
        name="moe_experts",
    )(tile_e, tile_b0, tile_nb, n_tiles, xs, w_gate_up, b_gate_up.reshape(N_EXPERTS, 1, 2 * D_FF),
      w_down, b_down.reshape(N_EXPERTS, 1, D_MODEL))


COMBINE_TT = 128


def _combine_kernel(pos_ref, nxt_ref, x_ref, gate_ref, ys_hbm, o_ref, buf, sem):
    i = pl.program_id(0)
    nsteps = pl.num_programs(0)
    slot = lax.rem(i, 2)

    def issue(ids, s):
        def body(r, carry):
            for k in range(TOP_K):
                pltpu.make_async_copy(ys_hbm.at[pl.ds(ids[0, 0, r * TOP_K + k], 1), :],
                                      buf.at[s, k, pl.ds(r, 1), :], sem.at[s]).start(priority=k % 2)
            return carry
        lax.fori_loop(0, COMBINE_TT, body, 0, unroll=4)

    @pl.when(i == 0)
    def _():
        issue(pos_ref, 0)

    @pl.when(i + 1 < nsteps)
    def _():
        issue(nxt_ref, 1 - slot)

    for k in range(TOP_K):
        pltpu.make_async_copy(ys_hbm.at[pl.ds(0, COMBINE_TT), :], buf.at[slot, k], sem.at[slot]).wait()
    gate = gate_ref[...]
    acc = x_ref[...]
    moe = gate[:, 0:1] * buf[slot, 0]
    for k in range(1, TOP_K):
        moe = moe + gate[:, k:k + 1] * buf[slot, k]
    o_ref[...] = acc + moe


def _combine(x1, gates, ys, pos):
    t, d = x1.shape
    tt = COMBINE_TT
    nsteps = t // tt
    pos3 = pos.reshape(nsteps, 1, tt * TOP_K)
    return pl.pallas_call(
        _combine_kernel,
        out_shape=jax.ShapeDtypeStruct((t, d), F32),
        grid=(nsteps,),
        in_specs=[pl.BlockSpec((1, 1, tt * TOP_K), lambda i: (i, 0, 0), memory_space=pltpu.SMEM),
                  pl.BlockSpec((1, 1, tt * TOP_K), lambda i: (jnp.minimum(i + 1, nsteps - 1), 0, 0),
                               memory_space=pltpu.SMEM),
                  pl.BlockSpec((tt, d), lambda i: (i, 0)),
                  pl.BlockSpec((tt, LANES), lambda i: (i, 0)),
                  pl.BlockSpec(memory_space=pl.ANY)],
        out_specs=pl.BlockSpec((tt, d), lambda i: (i, 0)),
        scratch_shapes=[pltpu.VMEM((2, TOP_K, tt, d), F32),
                        pltpu.SemaphoreType.DMA((2,))],
        compiler_params=_cparams(("arbitrary",)),
        name="combine",
    )(pos3, pos3, x1, gates, ys)


def _routing(top_idx):
    t = top_idx.shape[0]
    n_assign = t * TOP_K
    flat_e = top_idx.reshape(-1)
    experts = jnp.arange(N_EXPERTS, dtype=jnp.int32)
    order = jnp.argsort(flat_e)
    sorted_pos = jnp.argsort(order).astype(jnp.int32)
    counts = jnp.sum((flat_e[:, None] == experts[None, :]).astype(jnp.int32), axis=0)
    starts = jnp.cumsum(counts) - counts
    padded = (counts + MOE_BLOCK - 1) // MOE_BLOCK * MOE_BLOCK
    pad_ends = jnp.cumsum(padded)
    pad_starts = pad_ends - padded
    shift = pad_starts - starts
    pos = sorted_pos + shift[flat_e]
    n_rows = n_assign + N_EXPERTS * MOE_BLOCK
    n_rows = (n_rows + GATHER_ROWS - 1) // GATHER_ROWS * GATHER_ROWS
    blk_row0 = jnp.arange(n_rows // MOE_BLOCK, dtype=jnp.int32)[:, None] * MOE_BLOCK
    owns = (blk_row0 >= pad_starts[None, :]) & (blk_row0 < pad_ends[None, :])

    def per_block(table):
        return jnp.sum(jnp.where(owns, table[None, :], 0), axis=1, keepdims=True)

    rows = jnp.arange(n_rows, dtype=jnp.int32).reshape(-1, MOE_BLOCK)
    place = rows - per_block(shift)
    valid = (rows - per_block(pad_starts)) < per_block(counts)
    row_tok = jnp.where(valid, order[jnp.clip(place, 0, n_assign - 1)].astype(jnp.int32) // TOP_K, 0)
    row_tok = row_tok.reshape(-1)

    nblk = padded // MOE_BLOCK
    blk0 = pad_starts // MOE_BLOCK
    ntile = (nblk + MOE_NBT - 1) // MOE_NBT
    tile_end = jnp.cumsum(ntile)
    n_tiles = tile_end[-1]
    ids = jnp.arange(_moe_max_tiles(t), dtype=jnp.int32)
    ids_c = jnp.minimum(ids, n_tiles - 1)
    tile_e = jnp.minimum(jnp.searchsorted(tile_end, ids_c, side='right'), N_EXPERTS - 1).astype(jnp.int32)
    local = ids_c - (tile_end - ntile)[tile_e]
    tile_b0 = (blk0[tile_e] + local * MOE_NBT).astype(jnp.int32)
    tile_nb = jnp.where(ids < n_tiles, jnp.clip(nblk[tile_e] - local * MOE_NBT, 0, MOE_NBT), 0).astype(jnp.int32)
    return row_tok, pos, tile_e, tile_b0, tile_nb, n_tiles.reshape(1).astype(jnp.int32)


def _rope_tables(seq):
    f32 = np.float32
    inv_freq = f32(1.0) / (f32(ROPE_THETA) ** (np.arange(0, HEAD_DIM, 2, dtype=f32) / f32(HEAD_DIM)))
    ang = np.arange(seq, dtype=f32)[:, None] * inv_freq[None, :]
    cos = np.cos(ang).astype(f32)
    sin = np.sin(ang).astype(f32)
    reps = LANES // HEAD_DIM
    cos_t = np.tile(np.concatenate([cos, cos], axis=-1), (1, reps))
    sin_t = np.tile(np.concatenate([-sin, sin], axis=-1), (1, reps))
    return jnp.asarray(cos_t), jnp.asarray(sin_t)


def kernel(x, norm1_g, w_in, b_in, q_norm_g, k_norm_g, attn_sinks, w_attn_o, conv_dw_w, conv_dw_b,
           conv_ln_g, conv_ln_b, w_conv_o, w_out, norm2_g, w_router, b_router, w_gate_up, b_gate_up,
           w_down, b_down):
    b, s, d = x.shape
    t = b * s
    depth = norm1_g.shape[0]
    xt = x.reshape(t, d)
    cos_t, sin_t = _rope_tables(s)
    reps = LANES // HEAD_DIM
    c0 = QKV_WIDTH
    c1 = c0 + CONV_CH
    c2 = c1 + CONV_CH
    for l in range(depth):
        h = _rmsnorm(xt, norm1_g[l], BF16)
        qkv = _mm_bias(h, w_in, b_in, l, 0, c0, F32, tm=1024, tn=512, name="inproj_qkv")
        z = _mm_glu(h, w_in, b_in, l, c0, CONV_CH, tm=1024, tn=256)
        gates = _mm_bias(h, w_in, b_in, l, c2, 2 * d, BF16, tm=1024, tn=1024, sigmoid=True,
                         name="inproj_gates")
        attn = _attention(qkv, attn_sinks[l], cos_t, sin_t,
                          jnp.tile(q_norm_g[l], reps).reshape(1, LANES),
                          jnp.tile(k_norm_g[l], reps).reshape(1, LANES), b, s)
        conv = _conformer_conv(z, conv_dw_w[l], conv_dw_b[l], conv_ln_g[l], conv_ln_b[l], b, s)
        merged = _merge(attn, conv, w_attn_o, w_conv_o, l, gates, tm=1024, tn=512)
        x1 = _mm_residual(merged, w_out, l, xt, tm=1024, tn=512)
        h2, idx_pad, gate_pad = _router(x1, norm2_g[l], w_router[l], b_router[l])
        row_tok, pos, tile_e, tile_b0, tile_nb, n_tiles = _routing(idx_pad[:, :TOP_K])
        xs = _gather_rows(h2, row_tok)
        ys = _moe_experts(xs, _moe_max_blocks(t) * MOE_BLOCK, tile_e, tile_b0, tile_nb, n_tiles,
                          w_gate_up[l], b_gate_up[l], w_down[l], b_down[l])
        xt = _combine(x1, gate_pad, ys, pos)
    return xt.reshape(b, s, d)
```

```python
import functools

import jax
import jax.numpy as jnp
import numpy as np
from jax import lax
from jax.experimental import pallas as pl
from jax.experimental.pallas import tpu as pltpu

D_MODEL = 4096
HEAD_DIM = 64
N_Q_HEADS = 32
N_KV_HEADS = 4
WINDOW = 128
ATTN_BLOCK = 128
ROPE_THETA = 10000.0
CONV_CH = 2048
CONV_WIDTH = 31
N_EXPERTS = 32
TOP_K = 4
D_FF = 2048
SWIGLU_ALPHA = 1.702
SWIGLU_LIMIT = 7.0
MOE_BLOCK = 128
NORM_EPS = 1e-5

Q_WIDTH = N_Q_HEADS * HEAD_DIM
KV_WIDTH = N_KV_HEADS * HEAD_DIM
QKV_WIDTH = Q_WIDTH + 2 * KV_WIDTH

LANES = 128
SUBLANES = 8
VMEM_LIMIT = 56 * 1024 * 1024

BF16 = jnp.bfloat16
F32 = jnp.float32


def _cparams(sem, vmem=VMEM_LIMIT):
    return pltpu.CompilerParams(dimension_semantics=sem, vmem_limit_bytes=vmem)


def _rmsnorm_kernel(x_ref, g_ref, o_ref):
    x = x_ref[...]
    ms = jnp.mean(x * x, axis=-1, keepdims=True)
    o_ref[...] = (x * lax.rsqrt(ms + NORM_EPS) * g_ref[...]).astype(o_ref.dtype)


def _rmsnorm(x, g, out_dtype, tm=512):
    t, d = x.shape
    return pl.pallas_call(
        _rmsnorm_kernel,
        out_shape=jax.ShapeDtypeStruct((t, d), out_dtype),
        grid=(t // tm,),
        in_specs=[pl.BlockSpec((tm, d), lambda i: (i, 0)),
                  pl.BlockSpec((1, d), lambda i: (0, 0))],
        out_specs=pl.BlockSpec((tm, d), lambda i: (i, 0)),
        compiler_params=_cparams(("parallel",)),
        name="rmsnorm",
    )(x, g.reshape(1, d))


def _wspec(k, tn, layer, off=0):
    return pl.BlockSpec((None, k, tn), lambda j, i: (layer, 0, j + off))


def _cast_on_first_row_block(pairs):
    @pl.when(pl.program_id(1) == 0)
    def _():
        for w_ref, wb_ref in pairs:
            wb_ref[...] = w_ref[...].astype(BF16)


def _stage_weights(w_hbm, layer, cols, tn, stage, wbs, sem):
    j = pl.program_id(0)
    nj = pl.num_programs(0)

    def copies(jj):
        return [pltpu.make_async_copy(w_hbm.at[layer, :, pl.ds(pl.multiple_of(c + jj * tn, LANES), tn)],
                                      stage.at[q], sem.at[q]) for q, c in enumerate(cols)]

    @pl.when(pl.program_id(1) == 0)
    def _():
        @pl.when(j == 0)
        def _():
            for c in copies(j):
                c.start()

        for q, c in enumerate(copies(j)):
            c.wait()
            wbs[q][...] = stage[q].astype(BF16)

        @pl.when(j + 1 < nj)
        def _():
            for c in copies(j + 1):
                c.start()


def _mm_bias_kernel(a_ref, w_hbm, b_ref, o_ref, stage, wb, sem, *, layer, col0, tn, sigmoid):
    _stage_weights(w_hbm, layer, (col0,), tn, stage, (wb,), sem)
    acc = jnp.dot(a_ref[...], wb[...], preferred_element_type=F32) + b_ref[...]
    if sigmoid:
        acc = jax.nn.sigmoid(acc)
    o_ref[...] = acc.astype(o_ref.dtype)


def _mm_bias(a, w, b, layer, col0, n, out_dtype, *, tm, tn, sigmoid=False, name):
    m, k = a.shape
    return pl.pallas_call(
        functools.partial(_mm_bias_kernel, layer=layer, col0=col0, tn=tn, sigmoid=sigmoid),
        out_shape=jax.ShapeDtypeStruct((m, n), out_dtype),
        grid=(n // tn, m // tm),
        in_specs=[pl.BlockSpec((tm, k), lambda j, i: (i, 0)),
                  pl.BlockSpec(memory_space=pl.ANY),
                  pl.BlockSpec((1, tn), lambda j, i: (0, j))],
        out_specs=pl.BlockSpec((tm, tn), lambda j, i: (i, j)),
        scratch_shapes=[pltpu.VMEM((1, k, tn), F32), pltpu.VMEM((k, tn), BF16),
                        pltpu.SemaphoreType.DMA((1,))],
        compiler_params=_cparams(("arbitrary", "arbitrary")),
        name=name,
    )(a, w, b[layer, col0:col0 + n].reshape(1, n))


def _mm_glu_kernel(a_ref, w_hbm, ba_ref, bg_ref, o_ref, stage, wab, wgb, sem, *, layer, col0, n, tn):
    _stage_weights(w_hbm, layer, (col0, col0 + n), tn, stage, (wab, wgb), sem)
    a = a_ref[...]
    u = jnp.dot(a, wab[...], preferred_element_type=F32) + ba_ref[...]
    g = jnp.dot(a, wgb[...], preferred_element_type=F32) + bg_ref[...]
    o_ref[...] = (u * jax.nn.sigmoid(g)).astype(o_ref.dtype)


def _mm_glu(a, w, b, layer, col0, n, *, tm, tn):
    m, k = a.shape
    bu = b[layer, col0:col0 + n].reshape(1, n)
    bg = b[layer, col0 + n:col0 + 2 * n].reshape(1, n)
    return pl.pallas_call(
        functools.partial(_mm_glu_kernel, layer=layer, col0=col0, n=n, tn=tn),
        out_shape=jax.ShapeDtypeStruct((m, n), F32),
        grid=(n // tn, m // tm),
        in_specs=[pl.BlockSpec((tm, k), lambda j, i: (i, 0)),
                  pl.BlockSpec(memory_space=pl.ANY),
                  pl.BlockSpec((1, tn), lambda j, i: (0, j)), pl.BlockSpec((1, tn), lambda j, i: (0, j))],
        out_specs=pl.BlockSpec((tm, tn), lambda j, i: (i, j)),
        scratch_shapes=[pltpu.VMEM((2, k, tn), F32), pltpu.VMEM((k, tn), BF16), pltpu.VMEM((k, tn), BF16),
                        pltpu.SemaphoreType.DMA((2,))],
        compiler_params=_cparams(("arbitrary", "arbitrary")),
        name="inproj_glu",
    )(a, w, bu, bg)


def _merge_kernel(a_ref, c_ref, wa_ref, wc_ref, ga_ref, gc_ref, o_ref, wab, wcb):
    _cast_on_first_row_block([(wa_ref, wab), (wc_ref, wcb)])
    pa = jnp.dot(a_ref[...], wab[...], preferred_element_type=F32)
    pc = jnp.dot(c_ref[...], wcb[...], preferred_element_type=F32)
    o_ref[...] = (ga_ref[...].astype(F32) * pa + gc_ref[...].astype(F32) * pc).astype(o_ref.dtype)


def _merge(attn, conv, wa, wc, layer, gates, *, tm, tn):
    m, k = attn.shape
    n = wa.shape[2]
    nj = n // tn
    return pl.pallas_call(
        _merge_kernel,
        out_shape=jax.ShapeDtypeStruct((m, n), BF16),
        grid=(nj, m // tm),
        in_specs=[pl.BlockSpec((tm, k), lambda j, i: (i, 0)),
                  pl.BlockSpec((tm, k), lambda j, i: (i, 0)),
                  _wspec(k, tn, layer), _wspec(k, tn, layer),
                  pl.BlockSpec((tm, tn), lambda j, i: (i, j)),
                  pl.BlockSpec((tm, tn), lambda j, i: (i, j + nj))],
        out_specs=pl.BlockSpec((tm, tn), lambda j, i: (i, j)),
        scratch_shapes=[pltpu.VMEM((k, tn), BF16), pltpu.VMEM((k, tn), BF16)],
        compiler_params=_cparams(("arbitrary", "arbitrary")),
        name="merge",
    )(attn, conv, wa, wc, gates, gates)


def _mm_residual_kernel(a_ref, w_ref, x_ref, o_ref, wb):
    _cast_on_first_row_block([(w_ref, wb)])
    o_ref[...] = x_ref[...] + jnp.dot(a_ref[...], wb[...], preferred_element_type=F32)


def _mm_residual(a, w, layer, x, *, tm, tn):
    m, k = a.shape
    n = w.shape[2]
    return pl.pallas_call(
        _mm_residual_kernel,
        out_shape=jax.ShapeDtypeStruct((m, n), F32),
        grid=(n // tn, m // tm),
        in_specs=[pl.BlockSpec((tm, k), lambda j, i: (i, 0)),
                  _wspec(k, tn, layer),
                  pl.BlockSpec((tm, tn), lambda j, i: (i, j))],
        out_specs=pl.BlockSpec((tm, tn), lambda j, i: (i, j)),
        scratch_shapes=[pltpu.VMEM((k, tn), BF16)],
        compiler_params=_cparams(("arbitrary", "arbitrary")),
        name="outproj",
    )(a, w, x)


ATTN_RC = 32


def _attn_kernel(sink_ref, q_ref, k_ref, v_ref, cos_ref, sin_ref, qg_ref, kg_ref, o_ref,
                 kwin, vwin, kb, vb, qs, bias_scr, s_scr, p_scr, inv_scr):
    n = pl.program_id(1)
    blk = ATTN_BLOCK

    @pl.when(n == 0)
    def _():
        kwin[0:blk, :] = jnp.zeros((blk, KV_WIDTH), F32)
        vwin[0:blk, :] = jnp.zeros((blk, KV_WIDTH), F32)

    cos = cos_ref[...]
    sin = sin_ref[...]
    lane = lax.broadcasted_iota(jnp.int32, (blk, LANES), 1)
    first_half = (lane % HEAD_DIM) < (HEAD_DIM // 2)
    seg_r = lax.broadcasted_iota(jnp.int32, (LANES, LANES), 0) // HEAD_DIM
    seg_c = lax.broadcasted_iota(jnp.int32, (LANES, LANES), 1) // HEAD_DIM
    seg = jnp.where(seg_r == seg_c, 1.0, 0.0).astype(BF16)

    def norm_rope(t, g):
        t2 = t * t
        hi = t2.astype(BF16)
        lo = (t2 - hi.astype(F32)).astype(BF16)
        ss = (jnp.dot(hi, seg, preferred_element_type=F32)
              + jnp.dot(lo, seg, preferred_element_type=F32))
        tn = t * lax.rsqrt(ss * (1.0 / HEAD_DIM) + NORM_EPS) * g
        rot = jnp.where(first_half, pltpu.roll(tn, LANES - HEAD_DIM // 2, 1),
                        pltpu.roll(tn, HEAD_DIM // 2, 1))
        return tn * cos + rot * sin

    kg = kg_ref[...]
    qg = qg_ref[...]
    for c in range(KV_WIDTH // LANES):
        cols = slice(c * LANES, (c + 1) * LANES)
        kwin[blk:2 * blk, cols] = norm_rope(k_ref[:, cols], kg)
    vwin[blk:2 * blk, :] = v_ref[...]

    lane2 = lax.broadcasted_iota(jnp.int32, (2 * blk, LANES), 1)
    lo_half = lane2 < HEAD_DIM
    for c in range(KV_WIDTH // LANES):
        cols = slice(c * LANES, (c + 1) * LANES)
        for win, dst in ((kwin, kb), (vwin, vb)):
            x = win[:, cols]
            xs = pltpu.roll(x, HEAD_DIM, 1)
            dst[2 * c, 0:2 * blk, :] = jnp.where(lo_half, x, 0.0).astype(BF16)
            dst[2 * c, 2 * blk:4 * blk, :] = jnp.where(lo_half, 0.0, xs).astype(BF16)
            dst[2 * c + 1, 0:2 * blk, :] = jnp.where(lo_half, xs, 0.0).astype(BF16)
            dst[2 * c + 1, 2 * blk:4 * blk, :] = jnp.where(lo_half, 0.0, x).astype(BF16)

    qi = lax.broadcasted_iota(jnp.int32, (blk, 4 * blk), 0)
    kj = lax.broadcasted_iota(jnp.int32, (blk, 4 * blk), 1) % (2 * blk)
    rel = qi + blk - kj
    first_key = jnp.where(n == 0, blk, 0)
    mask = (rel >= 0) & (rel < WINDOW) & (kj >= first_key)
    bias_scr[...] = jnp.where(mask, 0.0, -jnp.inf)

    pairs_per_kv = N_Q_HEADS // N_KV_HEADS // 2
    for p in range(N_Q_HEADS // 2):
        g, pp = divmod(p, pairs_per_kv)
        q2 = norm_rope(q_ref[:, p * LANES:(p + 1) * LANES], qg) * (HEAD_DIM ** -0.5)
        qs[g, pp * blk:(pp + 1) * blk, :] = q2.astype(BF16)

    lo_out = lax.broadcasted_iota(jnp.int32, (ATTN_RC, LANES), 1) < HEAD_DIM
    for g in range(N_KV_HEADS):
        s_scr[g] = lax.dot_general(qs[g], kb[g], (((1,), (1,)), ((), ())), preferred_element_type=F32)
        for r in range(pairs_per_kv * blk // ATTN_RC):
            rows = slice(r * ATTN_RC, (r + 1) * ATTN_RC)
            pp, q0 = divmod(r * ATTN_RC, blk)
            s = s_scr[g, rows, :] + bias_scr[q0:q0 + ATTN_RC, :]
            invs = []
            for h in range(2):
                keys = slice(h * 2 * blk, (h + 1) * 2 * blk)
                sh = s[:, keys]
                sink = sink_ref[2 * (g * pairs_per_kv + pp) + h]
                m = jnp.maximum(jnp.max(sh, axis=-1, keepdims=True), sink)
                e = jnp.exp(sh - m)
                den = jnp.sum(e, axis=-1, keepdims=True) + jnp.exp(sink - m)
                p_scr[g, rows, keys] = e.astype(BF16)
                invs.append(1.0 / den)
            inv_scr[g, rows, :] = jnp.where(lo_out, invs[0], invs[1])
        o = jnp.dot(p_scr[g], vb[g], preferred_element_type=F32) * inv_scr[g]
        for pp in range(pairs_per_kv):
            p = g * pairs_per_kv + pp
            o_ref[:, p * LANES:(p + 1) * LANES] = o[pp * blk:(pp + 1) * blk, :].astype(o_ref.dtype)

    kwin[0:blk, :] = kwin[blk:2 * blk, :]
    vwin[0:blk, :] = vwin[blk:2 * blk, :]


def _attention(qkv, sinks, cos_t, sin_t, qg, kg, batch, seq):
    t = qkv.shape[0]
    blk = ATTN_BLOCK
    nb = seq // blk
    kcol = Q_WIDTH // KV_WIDTH
    return pl.pallas_call(
        _attn_kernel,
        out_shape=jax.ShapeDtypeStruct((t, Q_WIDTH), BF16),
        grid=(batch, nb),
        in_specs=[pl.BlockSpec(memory_space=pltpu.SMEM),
                  pl.BlockSpec((blk, Q_WIDTH), lambda b, n: (b * nb + n, 0)),
                  pl.BlockSpec((blk, KV_WIDTH), lambda b, n: (b * nb + n, kcol)),
                  pl.BlockSpec((blk, KV_WIDTH), lambda b, n: (b * nb + n, kcol + 1)),
                  pl.BlockSpec((blk, LANES), lambda b, n: (n, 0)),
                  pl.BlockSpec((blk, LANES), lambda b, n: (n, 0)),
                  pl.BlockSpec((1, LANES), lambda b, n: (0, 0)),
                  pl.BlockSpec((1, LANES), lambda b, n: (0, 0))],
        out_specs=pl.BlockSpec((blk, Q_WIDTH), lambda b, n: (b * nb + n, 0)),
        scratch_shapes=[pltpu.VMEM((2 * blk, KV_WIDTH), F32),
                        pltpu.VMEM((2 * blk, KV_WIDTH), F32),
                        pltpu.VMEM((N_KV_HEADS, 4 * blk, LANES), BF16),
                        pltpu.VMEM((N_KV_HEADS, 4 * blk, LANES), BF16),
                        pltpu.VMEM((N_KV_HEADS, 4 * blk, LANES), BF16),
                        pltpu.VMEM((blk, 4 * blk), F32),
                        pltpu.VMEM((N_KV_HEADS, 4 * blk, 4 * blk), F32),
                        pltpu.VMEM((N_KV_HEADS, 4 * blk, 4 * blk), BF16),
                        pltpu.VMEM((N_KV_HEADS, 4 * blk, LANES), F32)],
        compiler_params=_cparams(("arbitrary", "arbitrary")),
        name="swa_attention",
    )(sinks, qkv, qkv, qkv, cos_t, sin_t, qg, kg)


CONV_TS = 256
CONV_HALO = 32
CONV_RB = 128


def _conv_kernel(z_ref, w_ref, b_ref, lg_ref, lb_ref, o_ref, zbuf, cbuf):
    n = pl.program_id(1)

    @pl.when(n == 0)
    def _():
        zbuf[0:CONV_HALO, :] = jnp.zeros((CONV_HALO, CONV_CH), F32)

    zbuf[CONV_HALO:CONV_HALO + CONV_TS, :] = z_ref[...]
    shift = CONV_HALO - (CONV_WIDTH - 1)

    def chunk(c, carry):
        c0 = pl.multiple_of(c * LANES, LANES)
        for rb in range(CONV_TS // CONV_RB):
            base = rb * CONV_RB
            acc = jnp.broadcast_to(b_ref[:, pl.ds(c0, LANES)], (CONV_RB, LANES))
            for res in range(SUBLANES):
                extra = SUBLANES if res else 0
                part = None
                for off in range(shift, shift + CONV_WIDTH):
                    if off % SUBLANES != res:
                        continue
                    r0 = base + off - res
                    term = (zbuf[r0:r0 + CONV_RB + extra, pl.ds(c0, LANES)]
                            * w_ref[off - shift:off - shift + 1, pl.ds(c0, LANES)])
                    part = term if part is None else part + term
                acc = acc + part[res:res + CONV_RB]
            cbuf[base:base + CONV_RB, pl.ds(c0, LANES)] = acc
        return carry

    lax.fori_loop(0, CONV_CH // LANES, chunk, 0)

    zbuf[0:CONV_HALO, :] = zbuf[CONV_TS:CONV_TS + CONV_HALO, :]

    y = cbuf[...]
    mu = jnp.mean(y, axis=-1, keepdims=True)
    yc = y - mu
    var = jnp.mean(yc * yc, axis=-1, keepdims=True)
    yn = yc * lax.rsqrt(var + NORM_EPS) * lg_ref[...] + lb_ref[...]
    o_ref[...] = (yn * jax.nn.sigmoid(yn)).astype(o_ref.dtype)


def _conformer_conv(z, dw_w, dw_b, ln_g, ln_b, batch, seq):
    t = z.shape[0]
    ns = seq // CONV_TS
    vec = lambda a: a.reshape(1, CONV_CH)
    return pl.pallas_call(
        _conv_kernel,
        out_shape=jax.ShapeDtypeStruct((t, CONV_CH), BF16),
        grid=(batch, ns),
        in_specs=[pl.BlockSpec((CONV_TS, CONV_CH), lambda b, n: (b * ns + n, 0)),
                  pl.BlockSpec((CONV_WIDTH, CONV_CH), lambda b, n: (0, 0)),
                  pl.BlockSpec((1, CONV_CH), lambda b, n: (0, 0)),
                  pl.BlockSpec((1, CONV_CH), lambda b, n: (0, 0)),
                  pl.BlockSpec((1, CONV_CH), lambda b, n: (0, 0))],
        out_specs=pl.BlockSpec((CONV_TS, CONV_CH), lambda b, n: (b * ns + n, 0)),
        scratch_shapes=[pltpu.VMEM((CONV_HALO + CONV_TS, CONV_CH), F32),
                        pltpu.VMEM((CONV_TS, CONV_CH), F32)],
        compiler_params=_cparams(("arbitrary", "arbitrary")),
        name="conformer_conv",
    )(z, dw_w, vec(dw_b), vec(ln_g), vec(ln_b))


ROUTER_TM = 256
HI16 = 0xFFFF0000


def _pack_halves(xb):
    n = xb.shape[1] // 2
    lo = lax.bitcast_convert_type(xb[:, :n].astype(F32), jnp.uint32)
    hi = lax.bitcast_convert_type(xb[:, n:].astype(F32), jnp.uint32)
    return (lo >> 16) | (hi & jnp.uint32(HI16))


def _unpack_halves(w):
    lo = lax.bitcast_convert_type(w << 16, F32).astype(BF16)
    hi = lax.bitcast_convert_type(w & jnp.uint32(HI16), F32).astype(BF16)
    return lo, hi


def _router_kernel(x_ref, g_ref, wr_ref, br_ref, h_ref, idx_ref, gate_ref):
    x = x_ref[...]
    ms = jnp.mean(x * x, axis=-1, keepdims=True)
    hb = (x * lax.rsqrt(ms + NORM_EPS) * g_ref[...]).astype(BF16)
    packed = _pack_halves(hb)
    wpt = packed.shape[1] // LANES
    for c in range(wpt):
        h_ref[pl.ds(c, packed.shape[0], stride=wpt), :] = packed[:, c * LANES:(c + 1) * LANES]
    vals = jnp.dot(hb, wr_ref[...], preferred_element_type=F32) + br_ref[...]
    lane = lax.broadcasted_iota(jnp.int32, vals.shape, 1).astype(F32)
    tops, idxs = [], []
    for _ in range(TOP_K):
        m = jnp.max(vals, axis=-1, keepdims=True)
        idx = jnp.min(jnp.where(vals == m, lane, float(LANES)), axis=-1, keepdims=True)
        tops.append(m)
        idxs.append(idx)
        vals = jnp.where(lane == idx, -jnp.inf, vals)
    es = [jnp.exp(v - tops[0]) for v in tops]
    den = es[0] + es[1] + es[2] + es[3]
    idx_out = jnp.zeros(vals.shape, F32)
    gate_out = jnp.zeros(vals.shape, F32)
    for k in range(TOP_K):
        idx_out = jnp.where(lane == float(k), idxs[k], idx_out)
        gate_out = jnp.where(lane == float(k), es[k] / den, gate_out)
    idx_ref[...] = idx_out.astype(jnp.int32)
    gate_ref[...] = gate_out


def _router(x1, g, w_router, b_router):
    t, d = x1.shape
    tm = ROUTER_TM
    wr = jnp.zeros((d, LANES), BF16).at[:, :N_EXPERTS].set(w_router.astype(BF16))
    br = jnp.full((1, LANES), -1e30, F32).at[0, :N_EXPERTS].set(b_router)
    return pl.pallas_call(
        _router_kernel,
        out_shape=(jax.ShapeDtypeStruct((t * (d // 2 // LANES), LANES), jnp.uint32),
                   jax.ShapeDtypeStruct((t, LANES), jnp.int32),
                   jax.ShapeDtypeStruct((t, LANES), F32)),
        grid=(t // tm,),
        in_specs=[pl.BlockSpec((tm, d), lambda i: (i, 0)),
                  pl.BlockSpec((1, d), lambda i: (0, 0)),
                  pl.BlockSpec((d, LANES), lambda i: (0, 0)),
                  pl.BlockSpec((1, LANES), lambda i: (0, 0))],
        out_specs=(pl.BlockSpec((tm * (d // 2 // LANES), LANES), lambda i: (i, 0)),
                   pl.BlockSpec((tm, LANES), lambda i: (i, 0)),
                   pl.BlockSpec((tm, LANES), lambda i: (i, 0))),
        compiler_params=_cparams(("parallel",)),
        name="router",
    )(x1, g.reshape(1, d), wr, br)


GATHER_ROWS = 1024
GATHER_UNROLL = 8
GATHER_CHUNK = 128
GATHER_WPT = D_MODEL // 2 // LANES


def _gather_kernel(idx_ref, nxt_ref, src_hbm, o_ref, buf, sem):
    i = pl.program_id(0)
    nsteps = pl.num_programs(0)
    slot = lax.rem(i, 2)

    def issue(ids, s):
        def body(q, carry):
            for u in range(GATHER_UNROLL):
                r = q * GATHER_UNROLL + u
                src0 = pl.multiple_of(ids[0, 0, r] * GATHER_WPT, GATHER_WPT)
                dst0 = pl.multiple_of(r * GATHER_WPT, GATHER_WPT)
                pltpu.make_async_copy(src_hbm.at[pl.ds(src0, GATHER_WPT), :],
                                      buf.at[s, pl.ds(dst0, GATHER_WPT), :], sem.at[s]).start(priority=u % 2)
            return carry
        lax.fori_loop(0, GATHER_ROWS // GATHER_UNROLL, body, 0)

    @pl.when(i == 0)
    def _():
        issue(idx_ref, 0)

    @pl.when(i + 1 < nsteps)
    def _():
        issue(nxt_ref, 1 - slot)

    pltpu.make_async_copy(src_hbm.at[pl.ds(0, GATHER_ROWS * GATHER_WPT), :], buf.at[slot], sem.at[slot]).wait()
    half = GATHER_WPT * LANES
    for ch in range(GATHER_ROWS // GATHER_CHUNK):
        rows = slice(ch * GATHER_CHUNK, (ch + 1) * GATHER_CHUNK)
        for c in range(GATHER_WPT):
            w = buf[slot, pl.ds(ch * GATHER_CHUNK * GATHER_WPT + c, GATHER_CHUNK, stride=GATHER_WPT), :]
            lo, hi = _unpack_halves(w)
            o_ref[rows, c * LANES:(c + 1) * LANES] = lo
            o_ref[rows, half + c * LANES:half + (c + 1) * LANES] = hi


def _gather_rows(src, row_idx):
    n = row_idx.shape[0]
    d = GATHER_WPT * LANES
    nsteps = n // GATHER_ROWS
    idx3 = row_idx.reshape(nsteps, 1, GATHER_ROWS)
    return pl.pallas_call(
        _gather_kernel,
        out_shape=jax.ShapeDtypeStruct((n, 2 * d), BF16),
        grid=(nsteps,),
        in_specs=[pl.BlockSpec((1, 1, GATHER_ROWS), lambda i: (i, 0, 0), memory_space=pltpu.SMEM),
                  pl.BlockSpec((1, 1, GATHER_ROWS), lambda i: (jnp.minimum(i + 1, nsteps - 1), 0, 0),
                               memory_space=pltpu.SMEM),
                  pl.BlockSpec(memory_space=pl.ANY)],
        out_specs=pl.BlockSpec((GATHER_ROWS, 2 * d), lambda i: (i, 0)),
        scratch_shapes=[pltpu.VMEM((2, GATHER_ROWS * GATHER_WPT, LANES), src.dtype),
                        pltpu.SemaphoreType.DMA((2,))],
        compiler_params=_cparams(("arbitrary",)),
        name="dispatch_gather",
    )(idx3, idx3, src)


MOE_NBT = 10
MOE_TM = MOE_NBT * MOE_BLOCK
MOE_FAST = (7, 8, 9)
MOE_PIECE = 4
MOE_FC = 256
MOE_NC = 4 * MOE_FC
MOE_S1 = D_FF // MOE_FC
MOE_S2 = D_MODEL // MOE_NC
MOE_STEPS = MOE_S1 + MOE_S2
MOE_RING = 3
MOE_AHEAD = MOE_RING - 1
assert MOE_STEPS % MOE_RING == 0 and D_MODEL == 2 * D_FF


def _moe_max_blocks(t):
    return t * TOP_K // MOE_BLOCK + N_EXPERTS


def _moe_max_tiles(t):
    return (_moe_max_blocks(t) + N_EXPERTS * (MOE_NBT - 1)) // MOE_NBT + 1


def _moe_kernel(te_ref, tb_ref, tn_ref, nt_ref,
                xs_hbm, wgu_hbm, bgu_hbm, wd_hbm, bd_hbm, ys_hbm,
                xbuf, act, obuf, ring, bgu, bdn, xsem, osem, wsem, bsem):
    i = pl.program_id(0)
    ntiles = nt_ref[0]
    half = MOE_NC // 2

    def weight_copies(tile, s, slot):
        e = te_ref[tile]
        c1 = pl.multiple_of(jnp.minimum(s, MOE_S1 - 1) * MOE_FC, MOE_FC)
        c2 = pl.multiple_of(jnp.maximum(s - MOE_S1, 0) * MOE_NC, MOE_NC)
        gate_up = [pltpu.make_async_copy(wgu_hbm.at[e, :, pl.ds(g * D_FF + c1, MOE_FC)],
                                         ring.at[slot, :, pl.ds(g * MOE_FC, MOE_FC)], wsem.at[slot])
                   for g in range(2)]
        down = [pltpu.make_async_copy(wd_hbm.at[e, :, pl.ds(c2 + h * half, half)],
                                      ring.at[slot, pl.ds(h * D_FF, D_FF), :], wsem.at[slot])
                for h in range(2)]
        return gate_up, down

    def weights(tile, s, slot, op):
        s = jnp.asarray(s, jnp.int32)
        gate_up, down = weight_copies(tile, s, slot)

        @pl.when(s < MOE_S1)
        def _():
            for c in gate_up:
                op(c)

        @pl.when(s >= MOE_S1)
        def _():
            for c in down:
                op(c)

    def bias_copies(tile):
        e = te_ref[tile]
        ts = lax.rem(tile, 2)
        return [pltpu.make_async_copy(bgu_hbm.at[e], bgu.at[ts], bsem.at[ts]),
                pltpu.make_async_copy(bd_hbm.at[e], bdn.at[ts], bsem.at[ts])]

    start = lambda c: c.start()
    wait = lambda c: c.wait()

    def xs_copy(tile, row0, rows):
        src0 = pl.multiple_of(tb_ref[tile] * MOE_BLOCK + row0, MOE_BLOCK)
        return pltpu.make_async_copy(xs_hbm.at[pl.ds(src0, rows), :],
                                     xbuf.at[pl.ds(row0, rows), :], xsem.at[0])

    def for_row_pieces(nb, emit):
        general = nb >= 0
        for f in MOE_FAST:
            general = general & (nb != f)

            @pl.when(nb == f)
            def _():
                emit(0, f * MOE_BLOCK)

        @pl.when(general)
        def _():
            nbig = nb // MOE_PIECE
            big = MOE_PIECE * MOE_BLOCK

            def big_piece(q, carry):
                emit(pl.multiple_of(q * big, big), big)
                return carry

            def small_piece(b, carry):
                emit(pl.multiple_of(b * MOE_BLOCK, MOE_BLOCK), MOE_BLOCK)
                return carry

            lax.fori_loop(0, nbig, big_piece, 0)
            lax.fori_loop(nbig * MOE_PIECE, nb, small_piece, 0)

    def out_copy(slot, row0, rows, n):
        dst0 = pl.multiple_of(tb_ref[i] * MOE_BLOCK + row0, MOE_BLOCK)
        col0 = pl.multiple_of(n * MOE_NC, MOE_NC)
        return pltpu.make_async_copy(obuf.at[slot, pl.ds(row0, rows), :],
                                     ys_hbm.at[pl.ds(dst0, rows), pl.ds(col0, MOE_NC)],
                                     osem.at[slot])

    @pl.when(i < ntiles)
    def _():
        nb = tn_ref[i]
        tslot = lax.rem(i, 2)

        @pl.when(i == 0)
        def _():
            for_row_pieces(nb, lambda row0, rows: xs_copy(i, row0, rows).start())
            for c in bias_copies(i):
                c.start()
            for s0 in range(MOE_AHEAD):
                weights(i, s0, s0, start)

        for_row_pieces(nb, lambda row0, rows: xs_copy(i, row0, rows).wait())
        for c in bias_copies(i):
            c.wait()

        def step(s, carry):
            slot = lax.rem(s, MOE_RING)
            weights(i, s, slot, wait)

            ahead = s + MOE_AHEAD
            aslot = lax.rem(ahead, MOE_RING)

            @pl.when(ahead < MOE_STEPS)
            def _():
                weights(i, ahead, aslot, start)

            @pl.when((ahead >= MOE_STEPS) & (i + 1 < ntiles))
            def _():
                weights(i + 1, ahead - MOE_STEPS, aslot, start)

                @pl.when(ahead == MOE_STEPS)
                def _():
                    for c in bias_copies(i + 1):
                        c.start()

            @pl.when(s < MOE_S1)
            def _():
                c0 = pl.multiple_of(s * MOE_FC, MOE_FC)

                def gate_up(row0, rows):
                    x = xbuf[pl.ds(row0, rows), :]
                    wg = ring[slot, :, 0:MOE_FC].astype(BF16)
                    wl = ring[slot, :, MOE_FC:2 * MOE_FC].astype(BF16)
                    glu = jnp.dot(x, wg, preferred_element_type=F32) + bgu[tslot, :, pl.ds(c0, MOE_FC)]
                    lin = jnp.dot(x, wl, preferred_element_type=F32) + bgu[tslot, :, pl.ds(D_FF + c0, MOE_FC)]
                    glu = jnp.minimum(glu, SWIGLU_LIMIT)
                    lin = jnp.clip(lin, -SWIGLU_LIMIT, SWIGLU_LIMIT)
                    a = glu * jax.nn.sigmoid(SWIGLU_ALPHA * glu) * (lin + 1.0)
                    act[pl.ds(row0, rows), pl.ds(c0, MOE_FC)] = a.astype(BF16)

                for_row_pieces(nb, gate_up)

            @pl.when(s >= MOE_S1)
            def _():
                n = s - MOE_S1
                oslot = lax.rem(n, 2)

                @pl.when((s == MOE_S1) & (i + 1 < ntiles))
                def _():
                    for_row_pieces(tn_ref[i + 1], lambda row0, rows: xs_copy(i + 1, row0, rows).start())

                def down(row0, rows):
                    a = act[pl.ds(row0, rows), :]
                    for h in range(2):
                        wd = ring[slot, h * D_FF:(h + 1) * D_FF, :].astype(BF16)
                        bd = bdn[tslot, :, pl.ds(pl.multiple_of(n * MOE_NC + h * half, half), half)]
                        obuf[oslot, pl.ds(row0, rows), h * half:(h + 1) * half] = (
                            jnp.dot(a, wd, preferred_element_type=F32) + bd)
                    out_copy(oslot, row0, rows, n).start()

                for_row_pieces(nb, down)

                @pl.when(n > 0)
                def _():
                    for_row_pieces(nb, lambda row0, rows: out_copy(1 - oslot, row0, rows, n - 1).wait())

                @pl.when(n == MOE_S2 - 1)
                def _():
                    for_row_pieces(nb, lambda row0, rows: out_copy(oslot, row0, rows, n).wait())

            return carry

        lax.fori_loop(0, MOE_STEPS, step, 0)

    @pl.when(i == ntiles)
    def _():
        obuf[0, 0:MOE_BLOCK, :] = jnp.zeros((MOE_BLOCK, MOE_NC), F32)

        def fill(blk, carry):
            row0 = pl.multiple_of(blk * MOE_BLOCK, MOE_BLOCK)
            fills = [pltpu.make_async_copy(
                obuf.at[0, pl.ds(0, MOE_BLOCK), :],
                ys_hbm.at[pl.ds(row0, MOE_BLOCK), pl.ds(c * MOE_NC, MOE_NC)], osem.at[0])
                for c in range(MOE_S2)]
            for f in fills:
                f.start()
            for f in fills:
                f.wait()
            return carry

        lax.fori_loop(tb_ref[ntiles - 1] + tn_ref[ntiles - 1], ys_hbm.shape[0] // MOE_BLOCK, fill, 0)


def _moe_experts(xs, n_rows, tile_e, tile_b0, tile_nb, n_tiles, w_gate_up, b_gate_up, w_down, b_down):
    grid_spec = pltpu.PrefetchScalarGridSpec(
        num_scalar_prefetch=4,
        grid=(tile_e.shape[0],),
        in_specs=[pl.BlockSpec(memory_space=pl.ANY)] * 5,
        out_specs=pl.BlockSpec(memory_space=pl.ANY),
        scratch_shapes=[pltpu.VMEM((MOE_TM, D_MODEL), BF16),
                        pltpu.VMEM((MOE_TM, D_FF), BF16),
                        pltpu.VMEM((2, MOE_TM, MOE_NC), F32),
                        pltpu.VMEM((MOE_RING, D_MODEL, 2 * MOE_FC), F32),
                        pltpu.VMEM((2, 1, 2 * D_FF), F32),
                        pltpu.VMEM((2, 1, D_MODEL), F32),
                        pltpu.SemaphoreType.DMA((1,)),
                        pltpu.SemaphoreType.DMA((2,)),
                        pltpu.SemaphoreType.DMA((MOE_RING,)),
                        pltpu.SemaphoreType.DMA((2,))],
    )
    return pl.pallas_call(
        _moe_kernel,
        out_shape=jax.ShapeDtypeStruct((n_rows, D_MODEL), F32),
        grid_spec=grid_spec,
        compiler_params=_cparams(("arbitrary",)),
        name="moe_experts",
    )(tile_e, tile_b0, tile_nb, n_tiles, xs, w_gate_up, b_gate_up.reshape(N_EXPERTS, 1, 2 * D_FF),
      w_down, b_down.reshape(N_EXPERTS, 1, D_MODEL))


COMBINE_TT = 128


def _combine_kernel(pos_ref, nxt_ref, x_ref, gate_ref, ys_hbm, o_ref, buf, sem):
    i = pl.program_id(0)
    nsteps = pl.num_programs(0)
    slot = lax.rem(i, 2)

    def issue_rows(ids, s, r0):
        for u in range(SUBLANES):
            r = r0 + u
            for k in range(TOP_K):
                pltpu.make_async_copy(ys_hbm.at[pl.ds(ids[0, 0, r * TOP_K + k], 1), :],
                                      buf.at[s, k, pl.ds(r, 1), :], sem.at[s]).start(priority=k % 2)

    def combine_rows(r0):
        rows = pl.ds(r0, SUBLANES)
        gate = gate_ref[rows, :]
        moe = gate[:, 0:1] * buf[slot, 0, rows, :]
        for k in range(1, TOP_K):
            moe = moe + gate[:, k:k + 1] * buf[slot, k, rows, :]
        o_ref[rows, :] = x_ref[rows, :] + moe

    @pl.when(i == 0)
    def _():
        def body(g, carry):
            issue_rows(pos_ref, 0, g * SUBLANES)
            return carry
        lax.fori_loop(0, COMBINE_TT // SUBLANES, body, 0)

    for k in range(TOP_K):
        pltpu.make_async_copy(ys_hbm.at[pl.ds(0, COMBINE_TT), :], buf.at[slot, k], sem.at[slot]).wait()

    @pl.when(i + 1 < nsteps)
    def _():
        def body(g, carry):
            r0 = pl.multiple_of(g * SUBLANES, SUBLANES)
            issue_rows(nxt_ref, 1 - slot, r0)
            combine_rows(r0)
            return carry
        lax.fori_loop(0, COMBINE_TT // SUBLANES, body, 0)

    @pl.when(i + 1 == nsteps)
    def _():
        def body(g, carry):
            combine_rows(pl.multiple_of(g * SUBLANES, SUBLANES))
            return carry
        lax.fori_loop(0, COMBINE_TT // SUBLANES, body, 0)


def _combine(x1, gates, ys, pos):
    t, d = x1.shape
    tt = COMBINE_TT
    nsteps = t // tt
    pos3 = pos.reshape(nsteps, 1, tt * TOP_K)
    return pl.pallas_call(
        _combine_kernel,
        out_shape=jax.ShapeDtypeStruct((t, d), F32),
        grid=(nsteps,),
        in_specs=[pl.BlockSpec((1, 1, tt * TOP_K), lambda i: (i, 0, 0), memory_space=pltpu.SMEM),
                  pl.BlockSpec((1, 1, tt * TOP_K), lambda i: (jnp.minimum(i + 1, nsteps - 1), 0, 0),
                               memory_space=pltpu.SMEM),
                  pl.BlockSpec((tt, d), lambda i: (i, 0)),
                  pl.BlockSpec((tt, LANES), lambda i: (i, 0)),
                  pl.BlockSpec(memory_space=pl.ANY)],
        out_specs=pl.BlockSpec((tt, d), lambda i: (i, 0)),
        scratch_shapes=[pltpu.VMEM((2, TOP_K, tt, d), F32),
                        pltpu.SemaphoreType.DMA((2,))],
        compiler_params=_cparams(("arbitrary",)),
        name="combine",
    )(pos3, pos3, x1, gates, ys)


def _routing(top_idx):
    t = top_idx.shape[0]
    n_assign = t * TOP_K
    flat_e = top_idx.reshape(-1)
    experts = jnp.arange(N_EXPERTS, dtype=jnp.int32)
    order = jnp.argsort(flat_e)
    sorted_pos = jnp.argsort(order).astype(jnp.int32)
    counts = jnp.sum((flat_e[:, None] == experts[None, :]).astype(jnp.int32), axis=0)
    starts = jnp.cumsum(counts) - counts
    padded = (counts + MOE_BLOCK - 1) // MOE_BLOCK * MOE_BLOCK
    pad_ends = jnp.cumsum(padded)
    pad_starts = pad_ends - padded
    shift = pad_starts - starts
    pos = sorted_pos + shift[flat_e]
    n_rows = n_assign + N_EXPERTS * MOE_BLOCK
    n_rows = (n_rows + GATHER_ROWS - 1) // GATHER_ROWS * GATHER_ROWS
    blk_row0 = jnp.arange(n_rows // MOE_BLOCK, dtype=jnp.int32)[:, None] * MOE_BLOCK
    owns = (blk_row0 >= pad_starts[None, :]) & (blk_row0 < pad_ends[None, :])

    def per_block(table):
        return jnp.sum(jnp.where(owns, table[None, :], 0), axis=1, keepdims=True)

    rows = jnp.arange(n_rows, dtype=jnp.int32).reshape(-1, MOE_BLOCK)
    place = rows - per_block(shift)
    valid = (rows - per_block(pad_starts)) < per_block(counts)
    row_tok = jnp.where(valid, order[jnp.clip(place, 0, n_assign - 1)].astype(jnp.int32) // TOP_K, 0)
    row_tok = row_tok.reshape(-1)

    nblk = padded // MOE_BLOCK
    blk0 = pad_starts // MOE_BLOCK
    ntile = (nblk + MOE_NBT - 1) // MOE_NBT
    tile_end = jnp.cumsum(ntile)
    n_tiles = tile_end[-1]
    ids = jnp.arange(_moe_max_tiles(t), dtype=jnp.int32)
    ids_c = jnp.minimum(ids, n_tiles - 1)
    tile_e = jnp.minimum(jnp.searchsorted(tile_end, ids_c, side='right'), N_EXPERTS - 1).astype(jnp.int32)
    local = ids_c - (tile_end - ntile)[tile_e]
    tile_b0 = (blk0[tile_e] + local * MOE_NBT).astype(jnp.int32)
    tile_nb = jnp.where(ids < n_tiles, jnp.clip(nblk[tile_e] - local * MOE_NBT, 0, MOE_NBT), 0).astype(jnp.int32)
    return row_tok, pos, tile_e, tile_b0, tile_nb, n_tiles.reshape(1).astype(jnp.int32)


def _rope_tables(seq):
    f32 = np.float32
    inv_freq = f32(1.0) / (f32(ROPE_THETA) ** (np.arange(0, HEAD_DIM, 2, dtype=f32) / f32(HEAD_DIM)))
    ang = np.arange(seq, dtype=f32)[:, None] * inv_freq[None, :]
    cos = np.cos(ang).astype(f32)
    sin = np.sin(ang).astype(f32)
    reps = LANES // HEAD_DIM
    cos_t = np.tile(np.concatenate([cos, cos], axis=-1), (1, reps))
    sin_t = np.tile(np.concatenate([-sin, sin], axis=-1), (1, reps))
    return jnp.asarray(cos_t), jnp.asarray(sin_t)


def kernel(x, norm1_g, w_in, b_in, q_norm_g, k_norm_g, attn_sinks, w_attn_o, conv_dw_w, conv_dw_b,
           conv_ln_g, conv_ln_b, w_conv_o, w_out, norm2_g, w_router, b_router, w_gate_up, b_gate_up,
           w_down, b_down):
    b, s, d = x.shape
    t = b * s
    depth = norm1_g.shape[0]
    xt = x.reshape(t, d)
    cos_t, sin_t = _rope_tables(s)
    reps = LANES // HEAD_DIM
    c0 = QKV_WIDTH
    c1 = c0 + CONV_CH
    c2 = c1 + CONV_CH
    for l in range(depth):
        h = _rmsnorm(xt, norm1_g[l], BF16)
        qkv = _mm_bias(h, w_in, b_in, l, 0, c0, F32, tm=1024, tn=512, name="inproj_qkv")
        z = _mm_glu(h, w_in, b_in, l, c0, CONV_CH, tm=1024, tn=256)
        gates = _mm_bias(h, w_in, b_in, l, c2, 2 * d, BF16, tm=1024, tn=1024, sigmoid=True,
                         name="inproj_gates")
        attn = _attention(qkv, attn_sinks[l], cos_t, sin_t,
                          jnp.tile(q_norm_g[l], reps).reshape(1, LANES),
                          jnp.tile(k_norm_g[l], reps).reshape(1, LANES), b, s)
        conv = _conformer_conv(z, conv_dw_w[l], conv_dw_b[l], conv_ln_g[l], conv_ln_b[l], b, s)
        merged = _merge(attn, conv, w_attn_o, w_conv_o, l, gates, tm=1024, tn=512)
        x1 = _mm_residual(merged, w_out, l, xt, tm=1024, tn=512)
        h2, idx_pad, gate_pad = _router(x1, norm2_g[l], w_router[l], b_router[l])
        row_tok, pos, tile_e, tile_b0, tile_nb, n_tiles = _routing(idx_pad[:, :TOP_K])
        xs = _gather_rows(h2, row_tok)
        ys = _moe_experts(xs, _moe_max_blocks(t) * MOE_BLOCK, tile_e, tile_b0, tile_nb, n_tiles,
                          w_gate_up[l], b_gate_up[l], w_down[l], b_down[l])
        xt = _combine(x1, gate_pad, ys, pos)
    return xt.reshape(b, s, d)
```

```python
import functools

import jax
import jax.numpy as jnp
import numpy as np
from jax import lax
from jax.experimental import pallas as pl
from jax.experimental.pallas import tpu as pltpu

D_MODEL = 4096
HEAD_DIM = 64
N_Q_HEADS = 32
N_KV_HEADS = 4
WINDOW = 128
ATTN_BLOCK = 128
ROPE_THETA = 10000.0
CONV_CH = 2048
CONV_WIDTH = 31
N_EXPERTS = 32
TOP_K = 4
D_FF = 2048
SWIGLU_ALPHA = 1.702
SWIGLU_LIMIT = 7.0
MOE_BLOCK = 128
NORM_EPS = 1e-5

Q_WIDTH = N_Q_HEADS * HEAD_DIM
KV_WIDTH = N_KV_HEADS * HEAD_DIM
QKV_WIDTH = Q_WIDTH + 2 * KV_WIDTH

LANES = 128
SUBLANES = 8
VMEM_LIMIT = 56 * 1024 * 1024

BF16 = jnp.bfloat16
F32 = jnp.float32


def _cparams(sem, vmem=VMEM_LIMIT):
    return pltpu.CompilerParams(dimension_semantics=sem, vmem_limit_bytes=vmem)


def _rmsnorm_kernel(x_ref, g_ref, o_ref):
    x = x_ref[...]
    ms = jnp.mean(x * x, axis=-1, keepdims=True)
    o_ref[...] = (x * lax.rsqrt(ms + NORM_EPS) * g_ref[...]).astype(o_ref.dtype)


def _rmsnorm(x, g, out_dtype, tm=512):
    t, d = x.shape
    return pl.pallas_call(
        _rmsnorm_kernel,
        out_shape=jax.ShapeDtypeStruct((t, d), out_dtype),
        grid=(t // tm,),
        in_specs=[pl.BlockSpec((tm, d), lambda i: (i, 0)),
                  pl.BlockSpec((1, d), lambda i: (0, 0))],
        out_specs=pl.BlockSpec((tm, d), lambda i: (i, 0)),
        compiler_params=_cparams(("parallel",)),
        name="rmsnorm",
    )(x, g.reshape(1, d))


def _wspec(k, tn, layer, off=0):
    return pl.BlockSpec((None, k, tn), lambda j, i: (layer, 0, j + off))


def _cast_on_first_row_block(pairs):
    @pl.when(pl.program_id(1) == 0)
    def _():
        for w_ref, wb_ref in pairs:
            wb_ref[...] = w_ref[...].astype(BF16)


def _stage_weights(w_hbm, layer, cols, tn, stage, wbs, sem):
    j = pl.program_id(0)
    nj = pl.num_programs(0)

    def copies(jj):
        return [pltpu.make_async_copy(w_hbm.at[layer, :, pl.ds(pl.multiple_of(c + jj * tn, LANES), tn)],
                                      stage.at[q], sem.at[q]) for q, c in enumerate(cols)]

    @pl.when(pl.program_id(1) == 0)
    def _():
        @pl.when(j == 0)
        def _():
            for c in copies(j):
                c.start()

        for q, c in enumerate(copies(j)):
            c.wait()
            wbs[q][...] = stage[q].astype(BF16)

        @pl.when(j + 1 < nj)
        def _():
            for c in copies(j + 1):
                c.start()


def _mm_bias_kernel(a_ref, w_hbm, b_ref, o_ref, stage, wb, sem, *, layer, col0, tn, sigmoid):
    _stage_weights(w_hbm, layer, (col0,), tn, stage, (wb,), sem)
    acc = jnp.dot(a_ref[...], wb[...], preferred_element_type=F32) + b_ref[...]
    if sigmoid:
        acc = jax.nn.sigmoid(acc)
    o_ref[...] = acc.astype(o_ref.dtype)


def _mm_bias(a, w, b, layer, col0, n, out_dtype, *, tm, tn, sigmoid=False, name):
    m, k = a.shape
    return pl.pallas_call(
        functools.partial(_mm_bias_kernel, layer=layer, col0=col0, tn=tn, sigmoid=sigmoid),
        out_shape=jax.ShapeDtypeStruct((m, n), out_dtype),
        grid=(n // tn, m // tm),
        in_specs=[pl.BlockSpec((tm, k), lambda j, i: (i, 0)),
                  pl.BlockSpec(memory_space=pl.ANY),
                  pl.BlockSpec((1, tn), lambda j, i: (0, j))],
        out_specs=pl.BlockSpec((tm, tn), lambda j, i: (i, j)),
        scratch_shapes=[pltpu.VMEM((1, k, tn), F32), pltpu.VMEM((k, tn), BF16),
                        pltpu.SemaphoreType.DMA((1,))],
        compiler_params=_cparams(("arbitrary", "arbitrary")),
        name=name,
    )(a, w, b[layer, col0:col0 + n].reshape(1, n))


def _mm_glu_kernel(a_ref, w_hbm, ba_ref, bg_ref, o_ref, stage, wab, wgb, sem, *, layer, col0, n, tn):
    _stage_weights(w_hbm, layer, (col0, col0 + n), tn, stage, (wab, wgb), sem)
    a = a_ref[...]
    u = jnp.dot(a, wab[...], preferred_element_type=F32) + ba_ref[...]
    g = jnp.dot(a, wgb[...], preferred_element_type=F32) + bg_ref[...]
    o_ref[...] = (u * jax.nn.sigmoid(g)).astype(o_ref.dtype)


def _mm_glu(a, w, b, layer, col0, n, *, tm, tn):
    m, k = a.shape
    bu = b[layer, col0:col0 + n].reshape(1, n)
    bg = b[layer, col0 + n:col0 + 2 * n].reshape(1, n)
    return pl.pallas_call(
        functools.partial(_mm_glu_kernel, layer=layer, col0=col0, n=n, tn=tn),
        out_shape=jax.ShapeDtypeStruct((m, n), F32),
        grid=(n // tn, m // tm),
        in_specs=[pl.BlockSpec((tm, k), lambda j, i: (i, 0)),
                  pl.BlockSpec(memory_space=pl.ANY),
                  pl.BlockSpec((1, tn), lambda j, i: (0, j)), pl.BlockSpec((1, tn), lambda j, i: (0, j))],
        out_specs=pl.BlockSpec((tm, tn), lambda j, i: (i, j)),
        scratch_shapes=[pltpu.VMEM((2, k, tn), F32), pltpu.VMEM((k, tn), BF16), pltpu.VMEM((k, tn), BF16),
                        pltpu.SemaphoreType.DMA((2,))],
        compiler_params=_cparams(("arbitrary", "arbitrary")),
        name="inproj_glu",
    )(a, w, bu, bg)


def _merge_kernel(a_ref, c_ref, wa_ref, wc_ref, ga_ref, gc_ref, o_ref, wab, wcb):
    _cast_on_first_row_block([(wa_ref, wab), (wc_ref, wcb)])
    pa = jnp.dot(a_ref[...], wab[...], preferred_element_type=F32)
    pc = jnp.dot(c_ref[...], wcb[...], preferred_element_type=F32)
    o_ref[...] = (ga_ref[...].astype(F32) * pa + gc_ref[...].astype(F32) * pc).astype(o_ref.dtype)


def _merge(attn, conv, wa, wc, layer, gates, *, tm, tn):
    m, k = attn.shape
    n = wa.shape[2]
    nj = n // tn
    return pl.pallas_call(
        _merge_kernel,
        out_shape=jax.ShapeDtypeStruct((m, n), BF16),
        grid=(nj, m // tm),
        in_specs=[pl.BlockSpec((tm, k), lambda j, i: (i, 0)),
                  pl.BlockSpec((tm, k), lambda j, i: (i, 0)),
                  _wspec(k, tn, layer), _wspec(k, tn, layer),
                  pl.BlockSpec((tm, tn), lambda j, i: (i, j)),
                  pl.BlockSpec((tm, tn), lambda j, i: (i, j + nj))],
        out_specs=pl.BlockSpec((tm, tn), lambda j, i: (i, j)),
        scratch_shapes=[pltpu.VMEM((k, tn), BF16), pltpu.VMEM((k, tn), BF16)],
        compiler_params=_cparams(("arbitrary", "arbitrary")),
        name="merge",
    )(attn, conv, wa, wc, gates, gates)


def _mm_residual_kernel(a_ref, w_ref, x_ref, o_ref, wb):
    _cast_on_first_row_block([(w_ref, wb)])
    o_ref[...] = x_ref[...] + jnp.dot(a_ref[...], wb[...], preferred_element_type=F32)


def _mm_residual(a, w, layer, x, *, tm, tn):
    m, k = a.shape
    n = w.shape[2]
    return pl.pallas_call(
        _mm_residual_kernel,
        out_shape=jax.ShapeDtypeStruct((m, n), F32),
        grid=(n // tn, m // tm),
        in_specs=[pl.BlockSpec((tm, k), lambda j, i: (i, 0)),
                  _wspec(k, tn, layer),
                  pl.BlockSpec((tm, tn), lambda j, i: (i, j))],
        out_specs=pl.BlockSpec((tm, tn), lambda j, i: (i, j)),
        scratch_shapes=[pltpu.VMEM((k, tn), BF16)],
        compiler_params=_cparams(("arbitrary", "arbitrary")),
        name="outproj",
    )(a, w, x)


ATTN_RC = 32


def _attn_kernel(sink_ref, q_ref, k_ref, v_ref, cos_ref, sin_ref, qg_ref, kg_ref, o_ref,
                 kwin, vwin, kb, vb, qs, bias_scr, s_scr, p_scr, inv_scr):
    n = pl.program_id(1)
    blk = ATTN_BLOCK

    @pl.when(n == 0)
    def _():
        kwin[0:blk, :] = jnp.zeros((blk, KV_WIDTH), F32)
        vwin[0:blk, :] = jnp.zeros((blk, KV_WIDTH), F32)

    cos = cos_ref[...]
    sin = sin_ref[...]
    lane = lax.broadcasted_iota(jnp.int32, (blk, LANES), 1)
    first_half = (lane % HEAD_DIM) < (HEAD_DIM // 2)
    seg_r = lax.broadcasted_iota(jnp.int32, (LANES, LANES), 0) // HEAD_DIM
    seg_c = lax.broadcasted_iota(jnp.int32, (LANES, LANES), 1) // HEAD_DIM
    seg = jnp.where(seg_r == seg_c, 1.0, 0.0).astype(BF16)

    def norm_rope(t, g):
        t2 = t * t
        hi = t2.astype(BF16)
        lo = (t2 - hi.astype(F32)).astype(BF16)
        ss = (jnp.dot(hi, seg, preferred_element_type=F32)
              + jnp.dot(lo, seg, preferred_element_type=F32))
        tn = t * lax.rsqrt(ss * (1.0 / HEAD_DIM) + NORM_EPS) * g
        rot = jnp.where(first_half, pltpu.roll(tn, LANES - HEAD_DIM // 2, 1),
                        pltpu.roll(tn, HEAD_DIM // 2, 1))
        return tn * cos + rot * sin

    kg = kg_ref[...]
    qg = qg_ref[...]
    for c in range(KV_WIDTH // LANES):
        cols = slice(c * LANES, (c + 1) * LANES)
        kwin[blk:2 * blk, cols] = norm_rope(k_ref[:, cols], kg)
    vwin[blk:2 * blk, :] = v_ref[...]

    lane2 = lax.broadcasted_iota(jnp.int32, (2 * blk, LANES), 1)
    lo_half = lane2 < HEAD_DIM
    for c in range(KV_WIDTH // LANES):
        cols = slice(c * LANES, (c + 1) * LANES)
        for win, dst in ((kwin, kb), (vwin, vb)):
            x = win[:, cols]
            xs = pltpu.roll(x, HEAD_DIM, 1)
            dst[2 * c, 0:2 * blk, :] = jnp.where(lo_half, x, 0.0).astype(BF16)
            dst[2 * c, 2 * blk:4 * blk, :] = jnp.where(lo_half, 0.0, xs).astype(BF16)
            dst[2 * c + 1, 0:2 * blk, :] = jnp.where(lo_half, xs, 0.0).astype(BF16)
            dst[2 * c + 1, 2 * blk:4 * blk, :] = jnp.where(lo_half, 0.0, x).astype(BF16)

    qi = lax.broadcasted_iota(jnp.int32, (blk, 4 * blk), 0)
    kj = lax.broadcasted_iota(jnp.int32, (blk, 4 * blk), 1) % (2 * blk)
    rel = qi + blk - kj
    first_key = jnp.where(n == 0, blk, 0)
    mask = (rel >= 0) & (rel < WINDOW) & (kj >= first_key)
    bias_scr[...] = jnp.where(mask, 0.0, -jnp.inf)

    pairs_per_kv = N_Q_HEADS // N_KV_HEADS // 2
    for p in range(N_Q_HEADS // 2):
        g, pp = divmod(p, pairs_per_kv)
        q2 = norm_rope(q_ref[:, p * LANES:(p + 1) * LANES], qg) * (HEAD_DIM ** -0.5)
        qs[g, pp * blk:(pp + 1) * blk, :] = q2.astype(BF16)

    lo_out = lax.broadcasted_iota(jnp.int32, (ATTN_RC, LANES), 1) < HEAD_DIM
    for g in range(N_KV_HEADS):
        s_scr[g] = lax.dot_general(qs[g], kb[g], (((1,), (1,)), ((), ())), preferred_element_type=F32)
        for r in range(pairs_per_kv * blk // ATTN_RC):
            rows = slice(r * ATTN_RC, (r + 1) * ATTN_RC)
            pp, q0 = divmod(r * ATTN_RC, blk)
            s = s_scr[g, rows, :] + bias_scr[q0:q0 + ATTN_RC, :]
            invs = []
            for h in range(2):
                keys = slice(h * 2 * blk, (h + 1) * 2 * blk)
                sh = s[:, keys]
                sink = sink_ref[2 * (g * pairs_per_kv + pp) + h]
                m = jnp.maximum(jnp.max(sh, axis=-1, keepdims=True), sink)
                e = jnp.exp(sh - m)
                den = jnp.sum(e, axis=-1, keepdims=True) + jnp.exp(sink - m)
                p_scr[g, rows, keys] = e.astype(BF16)
                invs.append(1.0 / den)
            inv_scr[g, rows, :] = jnp.where(lo_out, invs[0], invs[1])
        o = jnp.dot(p_scr[g], vb[g], preferred_element_type=F32) * inv_scr[g]
        for pp in range(pairs_per_kv):
            p = g * pairs_per_kv + pp
            o_ref[:, p * LANES:(p + 1) * LANES] = o[pp * blk:(pp + 1) * blk, :].astype(o_ref.dtype)

    kwin[0:blk, :] = kwin[blk:2 * blk, :]
    vwin[0:blk, :] = vwin[blk:2 * blk, :]


def _attention(qkv, sinks, cos_t, sin_t, qg, kg, batch, seq):
    t = qkv.shape[0]
    blk = ATTN_BLOCK
    nb = seq // blk
    kcol = Q_WIDTH // KV_WIDTH
    return pl.pallas_call(
        _attn_kernel,
        out_shape=jax.ShapeDtypeStruct((t, Q_WIDTH), BF16),
        grid=(batch, nb),
        in_specs=[pl.BlockSpec(memory_space=pltpu.SMEM),
                  pl.BlockSpec((blk, Q_WIDTH), lambda b, n: (b * nb + n, 0)),
                  pl.BlockSpec((blk, KV_WIDTH), lambda b, n: (b * nb + n, kcol)),
                  pl.BlockSpec((blk, KV_WIDTH), lambda b, n: (b * nb + n, kcol + 1)),
                  pl.BlockSpec((blk, LANES), lambda b, n: (n, 0)),
                  pl.BlockSpec((blk, LANES), lambda b, n: (n, 0)),
                  pl.BlockSpec((1, LANES), lambda b, n: (0, 0)),
                  pl.BlockSpec((1, LANES), lambda b, n: (0, 0))],
        out_specs=pl.BlockSpec((blk, Q_WIDTH), lambda b, n: (b * nb + n, 0)),
        scratch_shapes=[pltpu.VMEM((2 * blk, KV_WIDTH), F32),
                        pltpu.VMEM((2 * blk, KV_WIDTH), F32),
                        pltpu.VMEM((N_KV_HEADS, 4 * blk, LANES), BF16),
                        pltpu.VMEM((N_KV_HEADS, 4 * blk, LANES), BF16),
                        pltpu.VMEM((N_KV_HEADS, 4 * blk, LANES), BF16),
                        pltpu.VMEM((blk, 4 * blk), F32),
                        pltpu.VMEM((N_KV_HEADS, 4 * blk, 4 * blk), F32),
                        pltpu.VMEM((N_KV_HEADS, 4 * blk, 4 * blk), BF16),
                        pltpu.VMEM((N_KV_HEADS, 4 * blk, LANES), F32)],
        compiler_params=_cparams(("arbitrary", "arbitrary")),
        name="swa_attention",
    )(sinks, qkv, qkv, qkv, cos_t, sin_t, qg, kg)


CONV_TS = 256
CONV_HALO = 32
CONV_RB = 128


def _conv_kernel(z_ref, w_ref, b_ref, lg_ref, lb_ref, o_ref, zbuf, cbuf):
    n = pl.program_id(1)

    @pl.when(n == 0)
    def _():
        zbuf[0:CONV_HALO, :] = jnp.zeros((CONV_HALO, CONV_CH), F32)

    zbuf[CONV_HALO:CONV_HALO + CONV_TS, :] = z_ref[...]
    shift = CONV_HALO - (CONV_WIDTH - 1)

    def chunk(c, carry):
        c0 = pl.multiple_of(c * LANES, LANES)
        for rb in range(CONV_TS // CONV_RB):
            base = rb * CONV_RB
            acc = jnp.broadcast_to(b_ref[:, pl.ds(c0, LANES)], (CONV_RB, LANES))
            for res in range(SUBLANES):
                extra = SUBLANES if res else 0
                part = None
                for off in range(shift, shift + CONV_WIDTH):
                    if off % SUBLANES != res:
                        continue
                    r0 = base + off - res
                    term = (zbuf[r0:r0 + CONV_RB + extra, pl.ds(c0, LANES)]
                            * w_ref[off - shift:off - shift + 1, pl.ds(c0, LANES)])
                    part = term if part is None else part + term
                acc = acc + part[res:res + CONV_RB]
            cbuf[base:base + CONV_RB, pl.ds(c0, LANES)] = acc
        return carry

    lax.fori_loop(0, CONV_CH // LANES, chunk, 0)

    zbuf[0:CONV_HALO, :] = zbuf[CONV_TS:CONV_TS + CONV_HALO, :]

    y = cbuf[...]
    mu = jnp.mean(y, axis=-1, keepdims=True)
    yc = y - mu
    var = jnp.mean(yc * yc, axis=-1, keepdims=True)
    yn = yc * lax.rsqrt(var + NORM_EPS) * lg_ref[...] + lb_ref[...]
    o_ref[...] = (yn * jax.nn.sigmoid(yn)).astype(o_ref.dtype)


def _conformer_conv(z, dw_w, dw_b, ln_g, ln_b, batch, seq):
    t = z.shape[0]
    ns = seq // CONV_TS
    vec = lambda a: a.reshape(1, CONV_CH)
    return pl.pallas_call(
        _conv_kernel,
        out_shape=jax.ShapeDtypeStruct((t, CONV_CH), BF16),
        grid=(batch, ns),
        in_specs=[pl.BlockSpec((CONV_TS, CONV_CH), lambda b, n: (b * ns + n, 0)),
                  pl.BlockSpec((CONV_WIDTH, CONV_CH), lambda b, n: (0, 0)),
                  pl.BlockSpec((1, CONV_CH), lambda b, n: (0, 0)),
                  pl.BlockSpec((1, CONV_CH), lambda b, n: (0, 0)),
                  pl.BlockSpec((1, CONV_CH), lambda b, n: (0, 0))],
        out_specs=pl.BlockSpec((CONV_TS, CONV_CH), lambda b, n: (b * ns + n, 0)),
        scratch_shapes=[pltpu.VMEM((CONV_HALO + CONV_TS, CONV_CH), F32),
                        pltpu.VMEM((CONV_TS, CONV_CH), F32)],
        compiler_params=_cparams(("arbitrary", "arbitrary")),
        name="conformer_conv",
    )(z, dw_w, vec(dw_b), vec(ln_g), vec(ln_b))


ROUTER_TM = 256
HI16 = 0xFFFF0000


def _pack_halves(xb):
    n = xb.shape[1] // 2
    lo = lax.bitcast_convert_type(xb[:, :n].astype(F32), jnp.uint32)
    hi = lax.bitcast_convert_type(xb[:, n:].astype(F32), jnp.uint32)
    return (lo >> 16) | (hi & jnp.uint32(HI16))


def _unpack_halves(w):
    lo = lax.bitcast_convert_type(w << 16, F32).astype(BF16)
    hi = lax.bitcast_convert_type(w & jnp.uint32(HI16), F32).astype(BF16)
    return lo, hi


def _router_kernel(x_ref, g_ref, wr_ref, br_ref, h_ref, idx_ref, gate_ref):
    x = x_ref[...]
    ms = jnp.mean(x * x, axis=-1, keepdims=True)
    hb = (x * lax.rsqrt(ms + NORM_EPS) * g_ref[...]).astype(BF16)
    h_ref[...] = _pack_halves(hb)
    vals = jnp.dot(hb, wr_ref[...], preferred_element_type=F32) + br_ref[...]
    lane = lax.broadcasted_iota(jnp.int32, vals.shape, 1).astype(F32)
    tops, idxs = [], []
    for _ in range(TOP_K):
        m = jnp.max(vals, axis=-1, keepdims=True)
        idx = jnp.min(jnp.where(vals == m, lane, float(LANES)), axis=-1, keepdims=True)
        tops.append(m)
        idxs.append(idx)
        vals = jnp.where(lane == idx, -jnp.inf, vals)
    es = [jnp.exp(v - tops[0]) for v in tops]
    den = es[0] + es[1] + es[2] + es[3]
    idx_out = jnp.zeros(vals.shape, F32)
    gate_out = jnp.zeros(vals.shape, F32)
    for k in range(TOP_K):
        idx_out = jnp.where(lane == float(k), idxs[k], idx_out)
        gate_out = jnp.where(lane == float(k), es[k] / den, gate_out)
    idx_ref[...] = idx_out.astype(jnp.int32)
    gate_ref[...] = gate_out


def _router(x1, g, w_router, b_router):
    t, d = x1.shape
    tm = ROUTER_TM
    wr = jnp.zeros((d, LANES), BF16).at[:, :N_EXPERTS].set(w_router.astype(BF16))
    br = jnp.full((1, LANES), -1e30, F32).at[0, :N_EXPERTS].set(b_router)
    return pl.pallas_call(
        _router_kernel,
        out_shape=(jax.ShapeDtypeStruct((t, d // 2), jnp.uint32),
                   jax.ShapeDtypeStruct((t, LANES), jnp.int32),
                   jax.ShapeDtypeStruct((t, LANES), F32)),
        grid=(t // tm,),
        in_specs=[pl.BlockSpec((tm, d), lambda i: (i, 0)),
                  pl.BlockSpec((1, d), lambda i: (0, 0)),
                  pl.BlockSpec((d, LANES), lambda i: (0, 0)),
                  pl.BlockSpec((1, LANES), lambda i: (0, 0))],
        out_specs=(pl.BlockSpec((tm, d // 2), lambda i: (i, 0)),
                   pl.BlockSpec((tm, LANES), lambda i: (i, 0)),
                   pl.BlockSpec((tm, LANES), lambda i: (i, 0))),
        compiler_params=_cparams(("parallel",)),
        name="router",
    )(x1, g.reshape(1, d), wr, br)


GATHER_ROWS = 1024
GATHER_UNROLL = 8
GATHER_CHUNK = 128


def _gather_kernel(idx_ref, nxt_ref, src_hbm, o_ref, buf, sem):
    i = pl.program_id(0)
    nsteps = pl.num_programs(0)
    slot = lax.rem(i, 2)

    def issue(ids, s):
        def body(q, carry):
            for u in range(GATHER_UNROLL):
                r = q * GATHER_UNROLL + u
                pltpu.make_async_copy(src_hbm.at[pl.ds(ids[0, 0, r], 1), :],
                                      buf.at[s, pl.ds(r, 1), :], sem.at[s]).start(priority=1)
            return carry
        lax.fori_loop(0, GATHER_ROWS // GATHER_UNROLL, body, 0)

    @pl.when(i == 0)
    def _():
        issue(idx_ref, 0)

    @pl.when(i + 1 < nsteps)
    def _():
        issue(nxt_ref, 1 - slot)

    pltpu.make_async_copy(src_hbm.at[pl.ds(0, GATHER_ROWS), :], buf.at[slot], sem.at[slot]).wait()
    half = buf.shape[2]
    for c in range(GATHER_ROWS // GATHER_CHUNK):
        rows = slice(c * GATHER_CHUNK, (c + 1) * GATHER_CHUNK)
        lo, hi = _unpack_halves(buf[slot, rows, :])
        o_ref[rows, 0:half] = lo
        o_ref[rows, half:2 * half] = hi


def _gather_rows(src, row_idx):
    n = row_idx.shape[0]
    d = src.shape[1]
    nsteps = n // GATHER_ROWS
    idx3 = row_idx.reshape(nsteps, 1, GATHER_ROWS)
    return pl.pallas_call(
        _gather_kernel,
        out_shape=jax.ShapeDtypeStruct((n, 2 * d), BF16),
        grid=(nsteps,),
        in_specs=[pl.BlockSpec((1, 1, GATHER_ROWS), lambda i: (i, 0, 0), memory_space=pltpu.SMEM),
                  pl.BlockSpec((1, 1, GATHER_ROWS), lambda i: (jnp.minimum(i + 1, nsteps - 1), 0, 0),
                               memory_space=pltpu.SMEM),
                  pl.BlockSpec(memory_space=pl.ANY)],
        out_specs=pl.BlockSpec((GATHER_ROWS, 2 * d), lambda i: (i, 0)),
        scratch_shapes=[pltpu.VMEM((2, GATHER_ROWS, d), src.dtype),
                        pltpu.SemaphoreType.DMA((2,))],
        compiler_params=_cparams(("arbitrary",)),
        name="dispatch_gather",
    )(idx3, idx3, src)


MOE_NBT = 10
MOE_TM = MOE_NBT * MOE_BLOCK
MOE_FAST = (7, 8, 9)
MOE_PIECE = 4
MOE_FC = 256
MOE_NC = 4 * MOE_FC
MOE_S1 = D_FF // MOE_FC
MOE_S2 = D_MODEL // MOE_NC
MOE_STEPS = MOE_S1 + MOE_S2
MOE_RING = 3
MOE_AHEAD = MOE_RING - 1
assert MOE_STEPS % MOE_RING == 0 and D_MODEL == 2 * D_FF


def _moe_max_blocks(t):
    return t * TOP_K // MOE_BLOCK + N_EXPERTS


def _moe_max_tiles(t):
    return (_moe_max_blocks(t) + N_EXPERTS * (MOE_NBT - 1)) // MOE_NBT + 1


def _moe_kernel(te_ref, tb_ref, tn_ref, nt_ref,
                xs_hbm, wgu_hbm, bgu_hbm, wd_hbm, bd_hbm, ys_hbm,
                xbuf, act, obuf, ring, bgu, bdn, xsem, osem, wsem, bsem):
    i = pl.program_id(0)
    ntiles = nt_ref[0]
    half = MOE_NC // 2

    def weight_copies(tile, s, slot):
        e = te_ref[tile]
        c1 = pl.multiple_of(jnp.minimum(s, MOE_S1 - 1) * MOE_FC, MOE_FC)
        c2 = pl.multiple_of(jnp.maximum(s - MOE_S1, 0) * MOE_NC, MOE_NC)
        gate_up = [pltpu.make_async_copy(wgu_hbm.at[e, :, pl.ds(g * D_FF + c1, MOE_FC)],
                                         ring.at[slot, :, pl.ds(g * MOE_FC, MOE_FC)], wsem.at[slot])
                   for g in range(2)]
        down = [pltpu.make_async_copy(wd_hbm.at[e, :, pl.ds(c2 + h * half, half)],
                                      ring.at[slot, pl.ds(h * D_FF, D_FF), :], wsem.at[slot])
                for h in range(2)]
        return gate_up, down

    def weights(tile, s, slot, op):
        s = jnp.asarray(s, jnp.int32)
        gate_up, down = weight_copies(tile, s, slot)

        @pl.when(s < MOE_S1)
        def _():
            for c in gate_up:
                op(c)

        @pl.when(s >= MOE_S1)
        def _():
            for c in down:
                op(c)

    def bias_copies(tile):
        e = te_ref[tile]
        ts = lax.rem(tile, 2)
        return [pltpu.make_async_copy(bgu_hbm.at[e], bgu.at[ts], bsem.at[ts]),
                pltpu.make_async_copy(bd_hbm.at[e], bdn.at[ts], bsem.at[ts])]

    start = lambda c: c.start()
    wait = lambda c: c.wait()

    def xs_copy(tile, row0, rows):
        src0 = pl.multiple_of(tb_ref[tile] * MOE_BLOCK + row0, MOE_BLOCK)
        return pltpu.make_async_copy(xs_hbm.at[pl.ds(src0, rows), :],
                                     xbuf.at[pl.ds(row0, rows), :], xsem.at[0])

    def for_row_pieces(nb, emit):
        general = nb >= 0
        for f in MOE_FAST:
            general = general & (nb != f)

            @pl.when(nb == f)
            def _():
                emit(0, f * MOE_BLOCK)

        @pl.when(general)
        def _():
            nbig = nb // MOE_PIECE
            big = MOE_PIECE * MOE_BLOCK

            def big_piece(q, carry):
                emit(pl.multiple_of(q * big, big), big)
                return carry

            def small_piece(b, carry):
                emit(pl.multiple_of(b * MOE_BLOCK, MOE_BLOCK), MOE_BLOCK)
                return carry

            lax.fori_loop(0, nbig, big_piece, 0)
            lax.fori_loop(nbig * MOE_PIECE, nb, small_piece, 0)

    def out_copy(slot, row0, rows, n):
        dst0 = pl.multiple_of(tb_ref[i] * MOE_BLOCK + row0, MOE_BLOCK)
        col0 = pl.multiple_of(n * MOE_NC, MOE_NC)
        return pltpu.make_async_copy(obuf.at[slot, pl.ds(row0, rows), :],
                                     ys_hbm.at[pl.ds(dst0, rows), pl.ds(col0, MOE_NC)],
                                     osem.at[slot])

    @pl.when(i < ntiles)
    def _():
        nb = tn_ref[i]
        tslot = lax.rem(i, 2)

        @pl.when(i == 0)
        def _():
            for_row_pieces(nb, lambda row0, rows: xs_copy(i, row0, rows).start())
            for c in bias_copies(i):
                c.start()
            for s0 in range(MOE_AHEAD):
                weights(i, s0, s0, start)

        for_row_pieces(nb, lambda row0, rows: xs_copy(i, row0, rows).wait())
        for c in bias_copies(i):
            c.wait()

        def step(s, carry):
            slot = lax.rem(s, MOE_RING)
            weights(i, s, slot, wait)

            ahead = s + MOE_AHEAD
            aslot = lax.rem(ahead, MOE_RING)

            @pl.when(ahead < MOE_STEPS)
            def _():
                weights(i, ahead, aslot, start)

            @pl.when((ahead >= MOE_STEPS) & (i + 1 < ntiles))
            def _():
                weights(i + 1, ahead - MOE_STEPS, aslot, start)

                @pl.when(ahead == MOE_STEPS)
                def _():
                    for c in bias_copies(i + 1):
                        c.start()

            @pl.when(s < MOE_S1)
            def _():
                c0 = pl.multiple_of(s * MOE_FC, MOE_FC)

                def gate_up(row0, rows):
                    x = xbuf[pl.ds(row0, rows), :]
                    wg = ring[slot, :, 0:MOE_FC].astype(BF16)
                    wl = ring[slot, :, MOE_FC:2 * MOE_FC].astype(BF16)
                    glu = jnp.dot(x, wg, preferred_element_type=F32) + bgu[tslot, :, pl.ds(c0, MOE_FC)]
                    lin = jnp.dot(x, wl, preferred_element_type=F32) + bgu[tslot, :, pl.ds(D_FF + c0, MOE_FC)]
                    glu = jnp.minimum(glu, SWIGLU_LIMIT)
                    lin = jnp.clip(lin, -SWIGLU_LIMIT, SWIGLU_LIMIT)
                    a = glu * jax.nn.sigmoid(SWIGLU_ALPHA * glu) * (lin + 1.0)
                    act[pl.ds(row0, rows), pl.ds(c0, MOE_FC)] = a.astype(BF16)

                for_row_pieces(nb, gate_up)

            @pl.when(s >= MOE_S1)
            def _():
                n = s - MOE_S1
                oslot = lax.rem(n, 2)

                @pl.when((s == MOE_S1) & (i + 1 < ntiles))
                def _():
                    for_row_pieces(tn_ref[i + 1], lambda row0, rows: xs_copy(i + 1, row0, rows).start())

                def down(row0, rows):
                    a = act[pl.ds(row0, rows), :]
                    for h in range(2):
                        wd = ring[slot, h * D_FF:(h + 1) * D_FF, :].astype(BF16)
                        bd = bdn[tslot, :, pl.ds(pl.multiple_of(n * MOE_NC + h * half, half), half)]
                        obuf[oslot, pl.ds(row0, rows), h * half:(h + 1) * half] = (
                            jnp.dot(a, wd, preferred_element_type=F32) + bd)
                    out_copy(oslot, row0, rows, n).start()

                for_row_pieces(nb, down)

                @pl.when(n > 0)
                def _():
                    for_row_pieces(nb, lambda row0, rows: out_copy(1 - oslot, row0, rows, n - 1).wait())

                @pl.when(n == MOE_S2 - 1)
                def _():
                    for_row_pieces(nb, lambda row0, rows: out_copy(oslot, row0, rows, n).wait())

            return carry

        lax.fori_loop(0, MOE_STEPS, step, 0)

    @pl.when(i == ntiles)
    def _():
        obuf[0, 0:MOE_BLOCK, :] = jnp.zeros((MOE_BLOCK, MOE_NC), F32)

        def fill(blk, carry):
            row0 = pl.multiple_of(blk * MOE_BLOCK, MOE_BLOCK)
            fills = [pltpu.make_async_copy(
                obuf.at[0, pl.ds(0, MOE_BLOCK), :],
                ys_hbm.at[pl.ds(row0, MOE_BLOCK), pl.ds(c * MOE_NC, MOE_NC)], osem.at[0])
                for c in range(MOE_S2)]
            for f in fills:
                f.start()
            for f in fills:
                f.wait()
            return carry

        lax.fori_loop(tb_ref[ntiles - 1] + tn_ref[ntiles - 1], ys_hbm.shape[0] // MOE_BLOCK, fill, 0)


def _moe_experts(xs, n_rows, tile_e, tile_b0, tile_nb, n_tiles, w_gate_up, b_gate_up, w_down, b_down):
    grid_spec = pltpu.PrefetchScalarGridSpec(
        num_scalar_prefetch=4,
        grid=(tile_e.shape[0],),
        in_specs=[pl.BlockSpec(memory_space=pl.ANY)] * 5,
        out_specs=pl.BlockSpec(memory_space=pl.ANY),
        scratch_shapes=[pltpu.VMEM((MOE_TM, D_MODEL), BF16),
                        pltpu.VMEM((MOE_TM, D_FF), BF16),
                        pltpu.VMEM((2, MOE_TM, MOE_NC), F32),
                        pltpu.VMEM((MOE_RING, D_MODEL, 2 * MOE_FC), F32),
                        pltpu.VMEM((2, 1, 2 * D_FF), F32),
                        pltpu.VMEM((2, 1, D_MODEL), F32),
                        pltpu.SemaphoreType.DMA((1,)),
                        pltpu.SemaphoreType.DMA((2,)),
                        pltpu.SemaphoreType.DMA((MOE_RING,)),
                        pltpu.SemaphoreType.DMA((2,))],
    )
    return pl.pallas_call(
        _moe_kernel,
        out_shape=jax.ShapeDtypeStruct((n_rows, D_MODEL), F32),
        grid_spec=grid_spec,
        compiler_params=_cparams(("arbitrary",)),
        name="moe_experts",
    )(tile_e, tile_b0, tile_nb, n_tiles, xs, w_gate_up, b_gate_up.reshape(N_EXPERTS, 1, 2 * D_FF),
      w_down, b_down.reshape(N_EXPERTS, 1, D_MODEL))


COMBINE_TT = 128


def _combine_kernel(pos_ref, nxt_ref, x_ref, gate_ref, ys_hbm, o_ref, buf, sem):
    i = pl.program_id(0)
    nsteps = pl.num_programs(0)
    slot = lax.rem(i, 2)

    def issue(ids, s):
        def body(r, carry):
            for k in range(TOP_K):
                pltpu.make_async_copy(ys_hbm.at[pl.ds(ids[0, 0, r * TOP_K + k], 1), :],
                                      buf.at[s, k, pl.ds(r, 1), :], sem.at[s]).start(priority=1)
            return carry
        lax.fori_loop(0, COMBINE_TT, body, 0, unroll=4)

    @pl.when(i == 0)
    def _():
        issue(pos_ref, 0)

    @pl.when(i + 1 < nsteps)
    def _():
        issue(nxt_ref, 1 - slot)

    for k in range(TOP_K):
        pltpu.make_async_copy(ys_hbm.at[pl.ds(0, COMBINE_TT), :], buf.at[slot, k], sem.at[slot]).wait()
    gate = gate_ref[...]
    acc = x_ref[...]
    moe = gate[:, 0:1] * buf[slot, 0]
    for k in range(1, TOP_K):
        moe = moe + gate[:, k:k + 1] * buf[slot, k]
    o_ref[...] = acc + moe


def _combine(x1, gates, ys, pos):
    t, d = x1.shape
    tt = COMBINE_TT
    nsteps = t // tt
    pos3 = pos.reshape(nsteps, 1, tt * TOP_K)
    return pl.pallas_call(
        _combine_kernel,
        out_shape=jax.ShapeDtypeStruct((t, d), F32),
        grid=(nsteps,),
        in_specs=[pl.BlockSpec((1, 1, tt * TOP_K), lambda i: (i, 0, 0), memory_space=pltpu.SMEM),
                  pl.BlockSpec((1, 1, tt * TOP_K), lambda i: (jnp.minimum(i + 1, nsteps - 1), 0, 0),
                               memory_space=pltpu.SMEM),
                  pl.BlockSpec((tt, d), lambda i: (i, 0)),
                  pl.BlockSpec((tt, LANES), lambda i: (i, 0)),
                  pl.BlockSpec(memory_space=pl.ANY)],
        out_specs=pl.BlockSpec((tt, d), lambda i: (i, 0)),
        scratch_shapes=[pltpu.VMEM((2, TOP_K, tt, d), F32),
                        pltpu.SemaphoreType.DMA((2,))],
        compiler_params=_cparams(("arbitrary",)),
        name="combine",
    )(pos3, pos3, x1, gates, ys)


def _routing(top_idx):
    t = top_idx.shape[0]
    n_assign = t * TOP_K
    flat_e = top_idx.reshape(-1)
    experts = jnp.arange(N_EXPERTS, dtype=jnp.int32)
    order = jnp.argsort(flat_e)
    sorted_pos = jnp.argsort(order).astype(jnp.int32)
    counts = jnp.sum((flat_e[:, None] == experts[None, :]).astype(jnp.int32), axis=0)
    starts = jnp.cumsum(counts) - counts
    padded = (counts + MOE_BLOCK - 1) // MOE_BLOCK * MOE_BLOCK
    pad_ends = jnp.cumsum(padded)
    pad_starts = pad_ends - padded
    shift = pad_starts - starts
    pos = sorted_pos + shift[flat_e]
    n_rows = n_assign + N_EXPERTS * MOE_BLOCK
    n_rows = (n_rows + GATHER_ROWS - 1) // GATHER_ROWS * GATHER_ROWS
    blk_row0 = jnp.arange(n_rows // MOE_BLOCK, dtype=jnp.int32)[:, None] * MOE_BLOCK
    owns = (blk_row0 >= pad_starts[None, :]) & (blk_row0 < pad_ends[None, :])

    def per_block(table):
        return jnp.sum(jnp.where(owns, table[None, :], 0), axis=1, keepdims=True)

    rows = jnp.arange(n_rows, dtype=jnp.int32).reshape(-1, MOE_BLOCK)
    place = rows - per_block(shift)
    valid = (rows - per_block(pad_starts)) < per_block(counts)
    row_tok = jnp.where(valid, order[jnp.clip(place, 0, n_assign - 1)].astype(jnp.int32) // TOP_K, 0)
    row_tok = row_tok.reshape(-1)

    nblk = padded // MOE_BLOCK
    blk0 = pad_starts // MOE_BLOCK
    ntile = (nblk + MOE_NBT - 1) // MOE_NBT
    tile_end = jnp.cumsum(ntile)
    n_tiles = tile_end[-1]
    ids = jnp.arange(_moe_max_tiles(t), dtype=jnp.int32)
    ids_c = jnp.minimum(ids, n_tiles - 1)
    tile_e = jnp.minimum(jnp.searchsorted(tile_end, ids_c, side='right'), N_EXPERTS - 1).astype(jnp.int32)
    local = ids_c - (tile_end - ntile)[tile_e]
    tile_b0 = (blk0[tile_e] + local * MOE_NBT).astype(jnp.int32)
    tile_nb = jnp.where(ids < n_tiles, jnp.clip(nblk[tile_e] - local * MOE_NBT, 0, MOE_NBT), 0).astype(jnp.int32)
    return row_tok, pos, tile_e, tile_b0, tile_nb, n_tiles.reshape(1).astype(jnp.int32)


def _rope_tables(seq):
    f32 = np.float32
    inv_freq = f32(1.0) / (f32(ROPE_THETA) ** (np.arange(0, HEAD_DIM, 2, dtype=f32) / f32(HEAD_DIM)))
    ang = np.arange(seq, dtype=f32)[:, None] * inv_freq[None, :]
    cos = np.cos(ang).astype(f32)
    sin = np.sin(ang).astype(f32)
    reps = LANES // HEAD_DIM
    cos_t = np.tile(np.concatenate([cos, cos], axis=-1), (1, reps))
    sin_t = np.tile(np.concatenate([-sin, sin], axis=-1), (1, reps))
    return jnp.asarray(cos_t), jnp.asarray(sin_t)


def kernel(x, norm1_g, w_in, b_in, q_norm_g, k_norm_g, attn_sinks, w_attn_o, conv_dw_w, conv_dw_b,
           conv_ln_g, conv_ln_b, w_conv_o, w_out, norm2_g, w_router, b_router, w_gate_up, b_gate_up,
           w_down, b_down):
    b, s, d = x.shape
    t = b * s
    depth = norm1_g.shape[0]
    xt = x.reshape(t, d)
    cos_t, sin_t = _rope_tables(s)
    reps = LANES // HEAD_DIM
    c0 = QKV_WIDTH
    c1 = c0 + CONV_CH
    c2 = c1 + CONV_CH
    for l in range(depth):
        h = _rmsnorm(xt, norm1_g[l], BF16)
        qkv = _mm_bias(h, w_in, b_in, l, 0, c0, F32, tm=1024, tn=512, name="inproj_qkv")
        z = _mm_glu(h, w_in, b_in, l, c0, CONV_CH, tm=1024, tn=256)
        gates = _mm_bias(h, w_in, b_in, l, c2, 2 * d, BF16, tm=1024, tn=1024, sigmoid=True,
                         name="inproj_gates")
        attn = _attention(qkv, attn_sinks[l], cos_t, sin_t,
                          jnp.tile(q_norm_g[l], reps).reshape(1, LANES),
                          jnp.tile(k_norm_g[l], reps).reshape(1, LANES), b, s)
        conv = _conformer_conv(z, conv_dw_w[l], conv_dw_b[l], conv_ln_g[l], conv_ln_b[l], b, s)
        merged = _merge(attn, conv, w_attn_o, w_conv_o, l, gates, tm=1024, tn=512)
        x1 = _mm_residual(merged, w_out, l, xt, tm=1024, tn=512)
        h2, idx_pad, gate_pad = _router(x1, norm2_g[l], w_router[l], b_router[l])
        row_tok, pos, tile_e, tile_b0, tile_nb, n_tiles = _routing(idx_pad[:, :TOP_K])
        xs = _gather_rows(h2, row_tok)
        ys = _moe_experts(xs, _moe_max_blocks(t) * MOE_BLOCK, tile_e, tile_b0, tile_nb, n_tiles,
                          w_gate_up[l], b_gate_up[l], w_down[l], b_down[l])
        xt = _combine(x1, gate_pad, ys, pos)
    return xt.reshape(b, s, d)
```
